```python
import jax
import jax.numpy as jnp
from jax import lax
import numpy as np

D_MODEL = 2048
BATCH = 8
SEQ = 2048
DEPTH = 2

GRID_W = 64
CTX_LEN = 256
D_CONV = D_MODEL // 4
D_MLSTM = D_MODEL // 2
D_SHORT = D_MODEL // 4
MLSTM_HEADS = 8
MLSTM_HEAD_DIM = D_MLSTM // MLSTM_HEADS
MLSTM_CHUNK = 64
CONV_A_WIDTH = 31
CONV_C_WIDTH = 3
D_FF = 4 * D_MODEL
N_GATES = 4 * MLSTM_HEADS
D_IN = 2 * D_CONV + 4 * D_MLSTM + N_GATES + 3 * D_SHORT
SPLIT_POINTS = (
    D_CONV,
    2 * D_CONV,
    2 * D_CONV + D_MLSTM,
    2 * D_CONV + 2 * D_MLSTM,
    2 * D_CONV + 3 * D_MLSTM,
    2 * D_CONV + 4 * D_MLSTM,
    2 * D_CONV + 4 * D_MLSTM + N_GATES,
    2 * D_CONV + 4 * D_MLSTM + N_GATES + D_SHORT,
    2 * D_CONV + 4 * D_MLSTM + N_GATES + 2 * D_SHORT,
)
NORM_EPS = 1e-6

kernel_name = "hybrid_conformer_mlstm_shortconv_dit_block"


def _rms_norm(x, w):
    xf = x.astype(jnp.float32)
    y = xf * lax.rsqrt(jnp.mean(xf * xf, axis=-1, keepdims=True) + NORM_EPS)
    return (y * w.astype(jnp.float32)).astype(x.dtype)


def _layer_norm(x, w, b):
    xf = x.astype(jnp.float32)
    mu = jnp.mean(xf, axis=-1, keepdims=True)
    var = jnp.mean(jnp.square(xf - mu), axis=-1, keepdims=True)
    y = (xf - mu) * lax.rsqrt(var + NORM_EPS)
    return (y * w.astype(jnp.float32) + b.astype(jnp.float32)).astype(x.dtype)


def _depthwise_conv(x, w, b):
    pad = w.shape[0] // 2
    y = lax.conv_general_dilated(
        x, w[:, None, :].astype(x.dtype), window_strides=(1,), padding=[(pad, pad)],
        dimension_numbers=("NWC", "WIO", "NWC"), feature_group_count=x.shape[-1])
    return y + b


def _short_conv_axis1(x, w):
    k = w.shape[0]
    pad = k // 2
    n = x.shape[1]
    widths = [(0, 0)] * x.ndim
    widths[1] = (pad, pad)
    p = jnp.pad(x, widths)
    y = w[0] * lax.slice_in_dim(p, 0, n, axis=1)
    for tap in range(1, k):
        y = y + w[tap] * lax.slice_in_dim(p, tap, tap + n, axis=1)
    return y


def _conformer_conv(a_val, a_gate, conv_w, conv_b, ln_w, ln_b, rows):
    u = a_val * jax.nn.sigmoid(a_gate)
    bsz, n, ch = u.shape
    if rows is None:
        y = _depthwise_conv(u, conv_w, conv_b)
    else:
        y = _depthwise_conv(u.reshape(bsz * rows, GRID_W, ch), conv_w, conv_b).reshape(bsz, n, ch)
    return jax.nn.silu(_layer_norm(y, ln_w, ln_b))


def _gated_short_conv(s_in, s_b, s_c, conv_w, rows):
    u = s_c * s_in
    bsz, n, ch = u.shape
    if rows is None:
        y = _short_conv_axis1(u, conv_w)
    else:
        y = _short_conv_axis1(u.reshape(bsz, rows, GRID_W, ch), conv_w).reshape(bsz, n, ch)
    return s_b * y


def _mlstm_chunked(q, k, v, log_i, log_f, state):
    bsz, heads, t_len, dh = q.shape
    n_chunks = t_len // MLSTM_CHUNK

    def to_chunks(a):
        a = a.reshape(bsz, heads, n_chunks, MLSTM_CHUNK, *a.shape[3:])
        return jnp.moveaxis(a, 2, 0)

    mask = jnp.tril(jnp.ones((MLSTM_CHUNK, MLSTM_CHUNK), dtype=bool))

    def step(carry, inp):
        c_prev, n_prev, m_prev = carry
        qc, kc, vc, lic, lfc = inp
        b = jnp.cumsum(lfc, axis=-1)
        log_d = jnp.where(mask, b[..., :, None] - b[..., None, :] + lic[..., None, :], -jnp.inf)
        log_inter = b + m_prev[..., None]
        m_row = jnp.maximum(log_inter, jnp.max(log_d, axis=-1))
        w_intra = jnp.exp(log_d - m_row[..., None])
        w_inter = jnp.exp(log_inter - m_row)
        scores = jnp.einsum("bhjd,bhsd->bhjs", qc, kc) * w_intra
        num = (jnp.einsum("bhjs,bhse->bhje", scores, vc)
               + w_inter[..., None] * jnp.einsum("bhjd,bhde->bhje", qc, c_prev))
        den = jnp.sum(scores, axis=-1) + w_inter * jnp.einsum("bhjd,bhd->bhj", qc, n_prev)
        h = num / jnp.maximum(jnp.abs(den), jnp.exp(-m_row))[..., None]
        b_last = b[..., -1]
        log_w = b_last[..., None] - b + lic
        m_new = jnp.maximum(b_last + m_prev, jnp.max(log_w, axis=-1))
        w_tok = jnp.exp(log_w - m_new[..., None])
        decay = jnp.exp(b_last + m_prev - m_new)
        kw = kc * w_tok[..., None]
        c_new = decay[..., None, None] * c_prev + jnp.einsum("bhsd,bhse->bhde", kw, vc)
        n_new = decay[..., None] * n_prev + jnp.sum(kw, axis=2)
        return (c_new, n_new, m_new), h

    state, h = lax.scan(step, state, (to_chunks(q), to_chunks(k), to_chunks(v),
                                      to_chunks(log_i), to_chunks(log_f)))
    return jnp.moveaxis(h, 0, 2).reshape(bsz, heads, t_len, dh), state


def _heads(t):
    bsz, n, _ = t.shape
    return t.reshape(bsz, n, MLSTM_HEADS, MLSTM_HEAD_DIM).transpose(0, 2, 1, 3).astype(jnp.float32)


def _mlstm_prep(q, k, v, g, gate_b):
    g = (g.astype(jnp.float32) + gate_b.astype(jnp.float32)).transpose(0, 2, 1)
    i_f, f_f, i_b, f_b = jnp.split(g, 4, axis=1)
    return (_heads(q), _heads(k) * MLSTM_HEAD_DIM ** -0.5, _heads(v),
            i_f, jax.nn.log_sigmoid(f_f), i_b, jax.nn.log_sigmoid(f_b))


def _mlstm_bidirectional(lat, ctx_in):
    qx, kx, vx, ixf, fxf, ixb, fxb = lat
    qc, kc, vc, icf, fcf, icb, fcb = ctx_in
    bsz = qx.shape[0]
    zero = (jnp.zeros((bsz, MLSTM_HEADS, MLSTM_HEAD_DIM, MLSTM_HEAD_DIM), jnp.float32),
            jnp.zeros((bsz, MLSTM_HEADS, MLSTM_HEAD_DIM), jnp.float32),
            jnp.zeros((bsz, MLSTM_HEADS), jnp.float32))
    hc_f, st_f = _mlstm_chunked(qc, kc, vc, icf, fcf, zero)
    hx_f, _ = _mlstm_chunked(qx, kx, vx, ixf, fxf, st_f)
    rev = lambda t: jnp.flip(t, axis=2)
    hc_b, st_b = _mlstm_chunked(rev(qc), rev(kc), rev(vc), rev(icb), rev(fcb), zero)
    hx_b, _ = _mlstm_chunked(rev(qx), rev(kx), rev(vx), rev(ixb), rev(fxb), st_b)
    return hx_f + rev(hx_b), hc_f + rev(hc_b)


def _mlstm_out(h, o, norm_w):
    h = h.transpose(0, 2, 1, 3)
    mu = jnp.mean(h, axis=-1, keepdims=True)
    var = jnp.mean(jnp.square(h - mu), axis=-1, keepdims=True)
    h = (h - mu) * lax.rsqrt(var + NORM_EPS)
    bsz, n = h.shape[:2]
    h = h.reshape(bsz, n, D_MLSTM) * norm_w.astype(jnp.float32)
    return (h * jax.nn.sigmoid(o.astype(jnp.float32))).astype(o.dtype)


def _sq_relu_mlp(h, w_ff1, w_ff2):
    return jnp.square(jax.nn.relu(h @ w_ff1)) @ w_ff2


def _layer(x, ctx, c, c_ctx, w_ada, b_ada, g_pre_mix, g_post_mix, g_pre_ffn, g_post_ffn,
           w_in, b_gates, conv_a_w, conv_a_b, ln_a_w, ln_a_b, mlstm_norm_w, conv_c_w,
           w_out, w_ff1, w_ff2, rows, update_ctx):
    mod_x = (jax.nn.silu(c) @ w_ada + b_ada)[:, None, :]
    mod_c = (jax.nn.silu(c_ctx) @ w_ada + b_ada)[None, None, :]
    sh1x, sc1x, g1x, sh2x, sc2x, g2x = jnp.split(mod_x, 6, axis=-1)
    sh1c, sc1c, g1c, sh2c, sc2c, g2c = jnp.split(mod_c, 6, axis=-1)

    hx = _rms_norm(x, g_pre_mix) * (1 + sc1x) + sh1x
    hc = _rms_norm(ctx, g_pre_mix) * (1 + sc1c) + sh1c
    px = jnp.split(hx @ w_in, SPLIT_POINTS, axis=-1)
    pc = jnp.split(hc @ w_in, SPLIT_POINTS, axis=-1)

    m_x, m_c = _mlstm_bidirectional(_mlstm_prep(px[2], px[3], px[4], px[6], b_gates),
                                    _mlstm_prep(pc[2], pc[3], pc[4], pc[6], b_gates))
    mix_x = jnp.concatenate([
        _conformer_conv(px[0], px[1], conv_a_w, conv_a_b, ln_a_w, ln_a_b, rows),
        _mlstm_out(m_x, px[5], mlstm_norm_w),
        _gated_short_conv(px[7], px[8], px[9], conv_c_w, rows),
    ], axis=-1) @ w_out
    x = x + g1x * _rms_norm(mix_x, g_post_mix)
    hx2 = _rms_norm(x, g_pre_ffn) * (1 + sc2x) + sh2x
    x = x + g2x * _rms_norm(_sq_relu_mlp(hx2, w_ff1, w_ff2), g_post_ffn)

    if update_ctx:
        mix_c = jnp.concatenate([
            _conformer_conv(pc[0], pc[1], conv_a_w, conv_a_b, ln_a_w, ln_a_b, None),
            _mlstm_out(m_c, pc[5], mlstm_norm_w),
            _gated_short_conv(pc[7], pc[8], pc[9], conv_c_w, None),
        ], axis=-1) @ w_out
        ctx = ctx + g1c * _rms_norm(mix_c, g_post_mix)
        hc2 = _rms_norm(ctx, g_pre_ffn) * (1 + sc2c) + sh2c
        ctx = ctx + g2c * _rms_norm(_sq_relu_mlp(hc2, w_ff1, w_ff2), g_post_ffn)
    return x, ctx


def setup_inputs(seed: int = 0) -> dict:
    key = jax.random.key(seed)
    ks = jax.random.split(key, 21)

    def nrm(k, shape, scale):
        return jax.random.normal(k, shape, jnp.float32) * scale

    kg = jax.random.split(ks[11], 4)
    fgate_init = jnp.linspace(3.0, 6.0, MLSTM_HEADS, dtype=jnp.float32)[None, :]
    b_gates = jnp.concatenate([
        nrm(kg[0], (DEPTH, MLSTM_HEADS), 0.1),
        fgate_init + nrm(kg[1], (DEPTH, MLSTM_HEADS), 0.1),
        nrm(kg[2], (DEPTH, MLSTM_HEADS), 0.1),
        fgate_init + nrm(kg[3], (DEPTH, MLSTM_HEADS), 0.1),
    ], axis=-1)
    return {
        "x": nrm(ks[0], (BATCH, SEQ, D_MODEL), 1.0),
        "c": nrm(ks[1], (BATCH, D_MODEL), 1.0),
        "ctx": nrm(ks[2], (BATCH, CTX_LEN, D_MODEL), 1.0),
        "c_ctx": nrm(ks[3], (D_MODEL,), 1.0),
        "w_ada": nrm(ks[4], (DEPTH, D_MODEL, 6 * D_MODEL), 0.5 * D_MODEL ** -0.5),
        "b_ada": nrm(ks[5], (DEPTH, 6 * D_MODEL), 0.02),
        "g_pre_mix": 1.0 + nrm(ks[6], (DEPTH, D_MODEL), 0.02),
        "g_post_mix": 1.0 + nrm(ks[7], (DEPTH, D_MODEL), 0.02),
        "g_pre_ffn": 1.0 + nrm(ks[8], (DEPTH, D_MODEL), 0.02),
        "g_post_ffn": 1.0 + nrm(ks[9], (DEPTH, D_MODEL), 0.02),
        "w_in": nrm(ks[10], (DEPTH, D_MODEL, D_IN), D_MODEL ** -0.5),
        "b_gates": b_gates,
        "conv_a_w": nrm(ks[12], (DEPTH, CONV_A_WIDTH, D_CONV), CONV_A_WIDTH ** -0.5),
        "conv_a_b": nrm(ks[13], (DEPTH, D_CONV), 0.02),
        "ln_a_w": 1.0 + nrm(ks[14], (DEPTH, D_CONV), 0.02),
        "ln_a_b": nrm(ks[15], (DEPTH, D_CONV), 0.02),
        "mlstm_norm_w": 1.0 + nrm(ks[16], (DEPTH, D_MLSTM), 0.02),
        "conv_c_w": nrm(ks[17], (DEPTH, CONV_C_WIDTH, D_SHORT), CONV_C_WIDTH ** -0.5),
        "w_out": nrm(ks[18], (DEPTH, D_MODEL, D_MODEL), D_MODEL ** -0.5),
        "w_ff1": nrm(ks[19], (DEPTH, D_MODEL, D_FF), D_MODEL ** -0.5),
        "w_ff2": nrm(ks[20], (DEPTH, D_FF, D_MODEL), D_FF ** -0.5),
    }


def reference(x, c, ctx, c_ctx, w_ada, b_ada, g_pre_mix, g_post_mix, g_pre_ffn, g_post_ffn,
              w_in, b_gates, conv_a_w, conv_a_b, ln_a_w, ln_a_b, mlstm_norm_w, conv_c_w,
              w_out, w_ff1, w_ff2):
    rows = x.shape[1] // GRID_W
    for layer in range(DEPTH):
        x, ctx = _layer(x, ctx, c, c_ctx, w_ada[layer], b_ada[layer], g_pre_mix[layer],
                        g_post_mix[layer], g_pre_ffn[layer], g_post_ffn[layer], w_in[layer],
                        b_gates[layer], conv_a_w[layer], conv_a_b[layer], ln_a_w[layer],
                        ln_a_b[layer], mlstm_norm_w[layer], conv_c_w[layer], w_out[layer],
                        w_ff1[layer], w_ff2[layer], rows=rows, update_ctx=layer < DEPTH - 1)
    return x
```

```python
import functools

import jax
import jax.numpy as jnp
from jax import lax
from jax.experimental import pallas as pl
from jax.experimental.pallas import tpu as pltpu

GRID_W = 64
NORM_EPS = 1e-6
LANE = 128
VMEM_LIMIT_BYTES = 56 * 1024 * 1024
MOD_ROWS = 16
MASKED_LOG = -1e30

F32 = jnp.float32
BF16 = jnp.bfloat16


def _params(n_grid_axes):
    return pltpu.CompilerParams(
        dimension_semantics=("arbitrary",) * n_grid_axes,
        vmem_limit_bytes=VMEM_LIMIT_BYTES)


def _rms(y):
    return y * lax.rsqrt(jnp.mean(y * y, axis=-1, keepdims=True) + NORM_EPS)


def _mod_kernel(s_ref, w_ref, b_ref, o_ref):
    s = s_ref[...]
    a = (s * jax.nn.sigmoid(s)).astype(BF16)
    o_ref[0] = jnp.dot(a, w_ref[0].astype(BF16), preferred_element_type=F32) + b_ref[0]


def _modulation(s, w_ada, b_ada, tn=1024):
    depth, d, n = w_ada.shape
    return pl.pallas_call(
        _mod_kernel,
        grid=(depth, n // tn),
        in_specs=[pl.BlockSpec((MOD_ROWS, d), lambda l, j: (0, 0)),
                  pl.BlockSpec((1, d, tn), lambda l, j: (l, 0, j)),
                  pl.BlockSpec((1, 1, tn), lambda l, j: (l, 0, j))],
        out_specs=pl.BlockSpec((1, MOD_ROWS, tn), lambda l, j: (l, 0, j)),
        out_shape=jax.ShapeDtypeStruct((depth, MOD_ROWS, n), F32),
        compiler_params=_params(2),
        name="adaln_modulation",
    )(s, w_ada, b_ada.reshape(depth, 1, n))


def _in_proj_kernel(x_ref, g_ref, sh_ref, sc_ref, w_ref, wg_ref, wgt_ref, bg_ref, bgt_ref,
                    p_ref, gc_ref, gr_ref, hn_ref, *, q_tile_lo, q_tile_hi, tile_offset, q_scale):
    j = pl.program_id(1)

    @pl.when(j == 0)
    def _():
        h = _rms(x_ref[...]) * g_ref[...] * (1.0 + sc_ref[0]) + sh_ref[0]
        hb = h.astype(BF16)
        hn_ref[...] = hb
        gc_ref[...] = jnp.dot(hb, wg_ref[...], preferred_element_type=F32) + bg_ref[...]
        gr_ref[...] = lax.dot_general(wgt_ref[...], hb, (((1,), (1,)), ((), ())),
                                      preferred_element_type=F32) + bgt_ref[...]

    acc = jnp.dot(hn_ref[...], w_ref[...], preferred_element_type=F32)
    jj = j + tile_offset
    scale = jnp.where((jj >= q_tile_lo) & (jj < q_tile_hi), q_scale, 1.0).astype(F32)
    p_ref[...] = (acc * scale).astype(BF16)


def _in_proj(x2d, mod3, mod_row, g_pre, w_main, wg, wgt, bg, *, tm, tn, tile_lo, n_tiles,
             q_tile_lo, q_tile_hi, q_scale, name):
    n, d = x2d.shape
    ng = wg.shape[1]
    kern = functools.partial(_in_proj_kernel, q_tile_lo=q_tile_lo, q_tile_hi=q_tile_hi,
                             tile_offset=tile_lo, q_scale=q_scale)
    return pl.pallas_call(
        kern,
        grid=(n // tm, n_tiles),
        in_specs=[pl.BlockSpec((tm, d), lambda i, j: (i, 0)),
                  pl.BlockSpec((1, d), lambda i, j: (0, 0)),
                  pl.BlockSpec((1, 1, d), lambda i, j: (mod_row(i), 0, 0)),
                  pl.BlockSpec((1, 1, d), lambda i, j: (mod_row(i), 0, 1)),
                  pl.BlockSpec((d, tn), lambda i, j: (0, j + tile_lo)),
                  pl.BlockSpec((d, ng), lambda i, j: (0, 0)),
                  pl.BlockSpec((ng, d), lambda i, j: (0, 0)),
                  pl.BlockSpec((1, ng), lambda i, j: (0, 0)),
                  pl.BlockSpec((ng, 1), lambda i, j: (0, 0))],
        out_specs=[pl.BlockSpec((tm, tn), lambda i, j: (i, j)),
                   pl.BlockSpec((tm, ng), lambda i, j: (i, 0)),
                   pl.BlockSpec((ng, tm), lambda i, j: (0, i))],
        out_shape=[jax.ShapeDtypeStruct((n, n_tiles * tn), BF16),
                   jax.ShapeDtypeStruct((n, ng), F32),
                   jax.ShapeDtypeStruct((ng, n), F32)],
        scratch_shapes=[pltpu.VMEM((tm, d), BF16)],
        compiler_params=_params(2),
        name=name,
    )(x2d, g_pre.reshape(1, d), mod3, mod3, w_main, wg, wgt, bg.reshape(1, ng), bg.reshape(ng, 1))


CONV_A_PAD = 16
CONV_SUB = 64


def _conv_a_kernel(av_ref, ag_ref, w_ref, b_ref, lnw_ref, lnb_ref, o_ref, upad_ref, y_ref,
                   *, row_len, n_rows, taps):
    ch = o_ref.shape[-1]
    half = taps // 2
    zeros = jnp.zeros((CONV_A_PAD, ch), F32)
    upad_ref[0:CONV_A_PAD, :] = zeros
    upad_ref[CONV_A_PAD + row_len:CONV_A_PAD + row_len + CONV_A_PAD, :] = zeros

    def row_body(r, carry):
        r0 = pl.multiple_of(r * row_len, row_len)
        av = av_ref[pl.ds(r0, row_len), :].astype(F32)
        ag = ag_ref[pl.ds(r0, row_len), :].astype(F32)
        upad_ref[CONV_A_PAD:CONV_A_PAD + row_len, :] = av * jax.nn.sigmoid(ag)
        for sb in range(row_len // CONV_SUB):
            for c in range(ch // LANE):
                cs = slice(c * LANE, (c + 1) * LANE)
                acc = jnp.zeros((CONV_SUB, LANE), F32)
                for k in range(taps):
                    start = CONV_A_PAD - half + k + sb * CONV_SUB
                    acc = acc + w_ref[k:k + 1, cs] * upad_ref[start:start + CONV_SUB, cs]
                y_ref[:, cs] = acc + b_ref[:, cs]
            y = y_ref[...]
            mu = jnp.mean(y, axis=-1, keepdims=True)
            dlt = y - mu
            var = jnp.mean(dlt * dlt, axis=-1, keepdims=True)
            z = dlt * lax.rsqrt(var + NORM_EPS) * lnw_ref[...] + lnb_ref[...]
            o_ref[pl.ds(r0 + sb * CONV_SUB, CONV_SUB), :] = (z * jax.nn.sigmoid(z)).astype(BF16)
        return carry

    lax.fori_loop(0, n_rows, row_body, 0)


def _conv_a(p, conv_w, conv_b, ln_w, ln_b, *, n_seq, seq_len, row_len, val_blk, gate_blk, name):
    taps, ch = conv_w.shape
    n = n_seq * seq_len
    kern = functools.partial(_conv_a_kernel, row_len=row_len, n_rows=seq_len // row_len, taps=taps)
    vec = lambda a: a.reshape(1, ch)
    return pl.pallas_call(
        kern,
        grid=(n_seq,),
        in_specs=[pl.BlockSpec((seq_len, ch), lambda b: (b, val_blk)),
                  pl.BlockSpec((seq_len, ch), lambda b: (b, gate_blk)),
                  pl.BlockSpec((taps, ch), lambda b: (0, 0)),
                  pl.BlockSpec((1, ch), lambda b: (0, 0)),
                  pl.BlockSpec((1, ch), lambda b: (0, 0)),
                  pl.BlockSpec((1, ch), lambda b: (0, 0))],
        out_specs=pl.BlockSpec((seq_len, ch), lambda b: (b, 0)),
        out_shape=jax.ShapeDtypeStruct((n, ch), BF16),
        scratch_shapes=[pltpu.VMEM((row_len + 2 * CONV_A_PAD, ch), F32),
                        pltpu.VMEM((CONV_SUB, ch), F32)],
        compiler_params=_params(1),
        name=name,
    )(p, p, conv_w, vec(conv_b), vec(ln_w), vec(ln_b))


CONV_C_PAD = 64
CONV_C_BLK = 64


def _conv_c_kernel(sin_ref, sb_ref, sc_ref, w_ref, o_ref, upad_ref, *, seq_len, shift):
    ch = o_ref.shape[-1]
    zeros = jnp.zeros((CONV_C_PAD, ch), F32)
    upad_ref[0:CONV_C_PAD, :] = zeros
    upad_ref[CONV_C_PAD + seq_len:CONV_C_PAD + seq_len + CONV_C_PAD, :] = zeros
    for i in range(seq_len // CONV_C_BLK):
        t = slice(i * CONV_C_BLK, (i + 1) * CONV_C_BLK)
        upad_ref[CONV_C_PAD + i * CONV_C_BLK:CONV_C_PAD + (i + 1) * CONV_C_BLK, :] = (
            sc_ref[t, :].astype(F32) * sin_ref[t, :].astype(F32))
    for i in range(seq_len // CONV_C_BLK):
        t0 = CONV_C_PAD + i * CONV_C_BLK
        y = (w_ref[0:1, :] * upad_ref[t0 - shift:t0 - shift + CONV_C_BLK, :]
             + w_ref[1:2, :] * upad_ref[t0:t0 + CONV_C_BLK, :]
             + w_ref[2:3, :] * upad_ref[t0 + shift:t0 + shift + CONV_C_BLK, :])
        t = slice(i * CONV_C_BLK, (i + 1) * CONV_C_BLK)
        o_ref[t, :] = (sb_ref[t, :].astype(F32) * y).astype(BF16)


def _conv_c(p, conv_w, *, n_seq, seq_len, shift, in_blk, b_blk, c_blk, name):
    taps, ch = conv_w.shape
    assert taps == 3 and shift <= CONV_C_PAD
    n = n_seq * seq_len
    kern = functools.partial(_conv_c_kernel, seq_len=seq_len, shift=shift)
    return pl.pallas_call(
        kern,
        grid=(n_seq,),
        in_specs=[pl.BlockSpec((seq_len, ch), lambda b: (b, in_blk)),
                  pl.BlockSpec((seq_len, ch), lambda b: (b, b_blk)),
                  pl.BlockSpec((seq_len, ch), lambda b: (b, c_blk)),
                  pl.BlockSpec((taps, ch), lambda b: (0, 0))],
        out_specs=pl.BlockSpec((seq_len, ch), lambda b: (b, 0)),
        out_shape=jax.ShapeDtypeStruct((n, ch), BF16),
        scratch_shapes=[pltpu.VMEM((seq_len + 2 * CONV_C_PAD, ch), F32)],
        compiler_params=_params(1),
        name=name,
    )(p, p, p, conv_w)


def _log_sigmoid(z):
    return jnp.minimum(z, 0.0) - jnp.log1p(jnp.exp(-jnp.abs(z)))


def _mlstm_chunk(q, k, v, gcol, grow, c_st, n_st, m_st, *, reverse, need_h):
    chunk = k.shape[0]
    li_col, lf_col = gcol[:, 0:1], _log_sigmoid(gcol[:, 1:2])
    li_row, lf_row = grow[0:1, :], _log_sigmoid(grow[1:2, :])
    jj = lax.broadcasted_iota(jnp.int32, (chunk, chunk), 0)
    ss = lax.broadcasted_iota(jnp.int32, (chunk, chunk), 1)
    seen = (ss >= jj) if reverse else (ss <= jj)
    b_col = jnp.sum(jnp.where(seen, lf_row, 0.0), axis=1, keepdims=True)
    b_last = jnp.sum(lf_row, axis=1, keepdims=True)

    h = None
    if need_h:
        seen_t = (jj >= ss) if reverse else (jj <= ss)
        b_row = jnp.sum(jnp.where(seen_t, lf_col, 0.0), axis=0, keepdims=True)
        log_d = jnp.where(seen, b_col - b_row + li_row, MASKED_LOG)
        log_inter = b_col + m_st
        m_row = jnp.maximum(log_inter, jnp.max(log_d, axis=1, keepdims=True))
        w_intra = jnp.exp(log_d - m_row)
        w_inter = jnp.exp(log_inter - m_row)
        scores = lax.dot_general(q, k, (((1,), (1,)), ((), ())), preferred_element_type=F32) * w_intra
        num = (jnp.dot(scores.astype(BF16), v, preferred_element_type=F32)
               + w_inter * jnp.dot(q, c_st.astype(BF16), preferred_element_type=F32))
        den = (jnp.sum(scores, axis=1, keepdims=True)
               + w_inter * jnp.sum(q.astype(F32) * n_st, axis=1, keepdims=True))
        h = num / jnp.maximum(jnp.abs(den), jnp.exp(-m_row))

    log_w = b_last - b_col + li_col
    m_new = jnp.maximum(b_last + m_st, jnp.max(log_w, axis=0, keepdims=True))
    decay = jnp.exp(b_last + m_st - m_new)
    kw = k.astype(F32) * jnp.exp(log_w - m_new)
    c_new = decay * c_st + lax.dot_general(kw.astype(BF16), v, (((0,), (0,)), ((), ())),
                                           preferred_element_type=F32)
    n_new = decay * n_st + jnp.sum(kw, axis=0, keepdims=True)
    return h, c_new, n_new, m_new


def _mlstm_kernel(*refs, chunk, n_lat, n_ctx, ctx_out):
    if ctx_out:
        (ql, kl, vl, ol, qc, kc, vc, oc, gcl, grl, gcc, grc, nw,
         out_l, out_c, hf_l, hb_l, hf_c, hb_c, c_ref, n_ref, m_ref) = refs
    else:
        (ql, kl, vl, ol, kc, vc, gcl, grl, gcc, grc, nw,
         out_l, hf_l, hb_l, c_ref, n_ref, m_ref) = refs
        qc = oc = out_c = hf_c = hb_c = None

    c_ref[...] = jnp.zeros_like(c_ref)
    n_ref[...] = jnp.zeros_like(n_ref)
    m_ref[...] = jnp.zeros_like(m_ref)

    def scan(q_ref, k_ref, v_ref, gc_ref, gr_ref, hf_ref, hb_ref, n_chunks, need_h):
        def step(i, direction):
            rows = pl.ds(pl.multiple_of(i * chunk, chunk), chunk)
            q = q_ref[rows, :] if need_h else None
            gcol = gc_ref[0, 0, i][:, 2 * direction:2 * direction + 2]
            grow = gr_ref[0, 0, i][2 * direction:2 * direction + 2, :]
            h, c_new, n_new, m_new = _mlstm_chunk(
                q, k_ref[rows, :], v_ref[rows, :], gcol, grow,
                c_ref[direction], n_ref[direction], m_ref[direction],
                reverse=direction == 1, need_h=need_h)
            c_ref[direction] = c_new
            n_ref[direction] = n_new
            m_ref[direction] = m_new
            if need_h:
                (hf_ref, hb_ref)[direction][rows, :] = h

        def body(i, carry):
            step(i, 0)
            step(n_chunks - 1 - i, 1)
            return carry

        lax.fori_loop(0, n_chunks, body, 0)

    def finish(hf_ref, hb_ref, o_ref, out_ref, n_chunks):
        def body(i, carry):
            rows = pl.ds(pl.multiple_of(i * chunk, chunk), chunk)
            h = hf_ref[rows, :] + hb_ref[rows, :]
            mu = jnp.mean(h, axis=-1, keepdims=True)
            dlt = h - mu
            var = jnp.mean(dlt * dlt, axis=-1, keepdims=True)
            y = dlt * lax.rsqrt(var + NORM_EPS) * nw[...]
            out_ref[rows, :] = (y * jax.nn.sigmoid(o_ref[rows, :].astype(F32))).astype(BF16)
            return carry

        lax.fori_loop(0, n_chunks, body, 0)

    scan(qc, kc, vc, gcc, grc, hf_c, hb_c, n_ctx, ctx_out)
    scan(ql, kl, vl, gcl, grl, hf_l, hb_l, n_lat, True)
    finish(hf_l, hb_l, ol, out_l, n_lat)
    if ctx_out:
        finish(hf_c, hb_c, oc, out_c, n_ctx)


def _gate_layouts(gc, gr, n_seq, seq_len, heads, chunk):
    n_chunks = seq_len // chunk
    col = gc.reshape(n_seq, n_chunks, chunk, 2, 2, heads)
    col = col.transpose(0, 5, 1, 2, 3, 4).reshape(n_seq, heads, n_chunks, chunk, 4)
    row = gr.reshape(2, 2, heads, n_seq, n_chunks, chunk)
    row = row.transpose(3, 2, 4, 0, 1, 5).reshape(n_seq, heads, n_chunks, 4, chunk)
    return col, row


def _mlstm(p_lat, p_ctx, gates_lat, gates_ctx, norm_w, *, n_seq, t_lat, t_ctx, heads, dh, chunk,
           lat_blk, ctx_blk, ctx_out, name):
    gcl, grl = _gate_layouts(*gates_lat, n_seq, t_lat, heads, chunk)
    gcc, grc = _gate_layouts(*gates_ctx, n_seq, t_ctx, heads, chunk)
    n_lat, n_ctx = t_lat // chunk, t_ctx // chunk

    def tok_spec(t, blk):
        return pl.BlockSpec((t, dh), lambda b, h: (b, blk + h))

    def gate_specs(n_chunks):
        return [pl.BlockSpec((1, 1, n_chunks, chunk, 4), lambda b, h: (b, h, 0, 0, 0)),
                pl.BlockSpec((1, 1, n_chunks, 4, chunk), lambda b, h: (b, h, 0, 0, 0))]

    lat_in = [tok_spec(t_lat, lat_blk[name_]) for name_ in "qkvo"]
    ctx_names = "qkvo" if ctx_out else "kv"
    ctx_in = [tok_spec(t_ctx, ctx_blk[name_]) for name_ in ctx_names]
    in_specs = (lat_in + ctx_in + gate_specs(n_lat) + gate_specs(n_ctx)
                + [pl.BlockSpec((1, dh), lambda b, h: (0, h))])
    args = [p_lat] * 4 + [p_ctx] * len(ctx_names) + [gcl, grl, gcc, grc, norm_w.reshape(1, heads * dh)]

    out_specs = [pl.BlockSpec((t_lat, dh), lambda b, h: (b, h))]
    out_shape = [jax.ShapeDtypeStruct((n_seq * t_lat, heads * dh), BF16)]
    scratch = [pltpu.VMEM((t_lat, dh), F32), pltpu.VMEM((t_lat, dh), F32)]
    if ctx_out:
        out_specs.append(pl.BlockSpec((t_ctx, dh), lambda b, h: (b, h)))
        out_shape.append(jax.ShapeDtypeStruct((n_seq * t_ctx, heads * dh), BF16))
        scratch += [pltpu.VMEM((t_ctx, dh), F32), pltpu.VMEM((t_ctx, dh), F32)]
    scratch += [pltpu.VMEM((2, dh, dh), F32), pltpu.VMEM((2, 1, dh), F32), pltpu.VMEM((2, 1, 1), F32)]

    kern = functools.partial(_mlstm_kernel, chunk=chunk, n_lat=n_lat, n_ctx=n_ctx, ctx_out=ctx_out)
    return pl.pallas_call(
        kern,
        grid=(n_seq, heads),
        in_specs=in_specs,
        out_specs=out_specs,
        out_shape=out_shape,
        scratch_shapes=scratch,
        compiler_params=_params(2),
        name=name,
    )(*args)


def _out_proj_kernel(a_ref, m_ref, c_ref, w_ref, x_ref, gpost_ref, g1_ref, gpre_ref, sh2_ref, sc2_ref,
                     x1_ref, h2_ref):
    da, dm = a_ref.shape[-1], m_ref.shape[-1]
    y = (jnp.dot(a_ref[...], w_ref[0:da, :], preferred_element_type=F32)
         + jnp.dot(m_ref[...], w_ref[da:da + dm, :], preferred_element_type=F32)
         + jnp.dot(c_ref[...], w_ref[da + dm:, :], preferred_element_type=F32))
    x1 = x_ref[...] + g1_ref[0] * (_rms(y) * gpost_ref[...])
    x1_ref[...] = x1
    h2 = _rms(x1) * gpre_ref[...] * (1.0 + sc2_ref[0]) + sh2_ref[0]
    h2_ref[...] = h2.astype(BF16)


def _out_proj(a, m, cc, w_out, x2d, mod3, mod_row, g_post, g_pre_ffn, *, tm, name):
    n, d = x2d.shape
    da, dm, dc = a.shape[1], m.shape[1], cc.shape[1]
    row = lambda i: (i, 0)
    fixed = lambda i: (0, 0)
    mod_spec = lambda k: pl.BlockSpec((1, 1, d), lambda i: (mod_row(i), 0, k))
    return pl.pallas_call(
        _out_proj_kernel,
        grid=(n // tm,),
        in_specs=[pl.BlockSpec((tm, da), row), pl.BlockSpec((tm, dm), row), pl.BlockSpec((tm, dc), row),
                  pl.BlockSpec((da + dm + dc, d), fixed),
                  pl.BlockSpec((tm, d), row),
                  pl.BlockSpec((1, d), fixed),
                  mod_spec(2),
                  pl.BlockSpec((1, d), fixed),
                  mod_spec(3), mod_spec(4)],
        out_specs=[pl.BlockSpec((tm, d), row), pl.BlockSpec((tm, d), row)],
        out_shape=[jax.ShapeDtypeStruct((n, d), F32), jax.ShapeDtypeStruct((n, d), BF16)],
        compiler_params=_params(1),
        name=name,
    )(a, m, cc, w_out, x2d, g_post.reshape(1, d), mod3, g_pre_ffn.reshape(1, d), mod3, mod3)


def _ffn_kernel(h_ref, w1_ref, w2_ref, x_ref, gpost_ref, g2_ref, o_ref, acc_ref):
    j = pl.program_id(1)

    @pl.when(j == 0)
    def _():
        acc_ref[...] = jnp.zeros_like(acc_ref)

    u = jnp.maximum(jnp.dot(h_ref[...], w1_ref[...], preferred_element_type=F32), 0.0)
    acc_ref[...] += jnp.dot((u * u).astype(BF16), w2_ref[...], preferred_element_type=F32)

    @pl.when(j == pl.num_programs(1) - 1)
    def _():
        o_ref[...] = x_ref[...] + g2_ref[0] * (_rms(acc_ref[...]) * gpost_ref[...])


def _ffn(h2, w1, w2, x1, mod3, mod_row, g_post, *, tm, tf, name):
    n, d = x1.shape
    dff = w1.shape[1]
    return pl.pallas_call(
        _ffn_kernel,
        grid=(n // tm, dff // tf),
        in_specs=[pl.BlockSpec((tm, d), lambda i, j: (i, 0)),
                  pl.BlockSpec((d, tf), lambda i, j: (0, j)),
                  pl.BlockSpec((tf, d), lambda i, j: (j, 0)),
                  pl.BlockSpec((tm, d), lambda i, j: (i, 0)),
                  pl.BlockSpec((1, d), lambda i, j: (0, 0)),
                  pl.BlockSpec((1, 1, d), lambda i, j: (mod_row(i), 0, 5))],
        out_specs=pl.BlockSpec((tm, d), lambda i, j: (i, 0)),
        out_shape=jax.ShapeDtypeStruct((n, d), F32),
        scratch_shapes=[pltpu.VMEM((tm, d), F32)],
        compiler_params=_params(2),
        name=name,
    )(h2, w1, w2, x1, g_post.reshape(1, d), mod3)


def _tile(n, target):
    t = min(n, target)
    while n % t or (t % 8 and t != n):
        t -= 1
    return t


def kernel(x, c, ctx, c_ctx, w_ada, b_ada, g_pre_mix, g_post_mix, g_pre_ffn, g_post_ffn, w_in, b_gates,
           conv_a_w, conv_a_b, ln_a_w, ln_a_b, mlstm_norm_w, conv_c_w, w_out, w_ff1, w_ff2):
    bsz, t_lat, d = x.shape
    t_ctx = ctx.shape[1]
    depth = w_ada.shape[0]
    d_conv = conv_a_w.shape[-1]
    d_short = conv_c_w.shape[-1]
    d_mlstm = mlstm_norm_w.shape[-1]
    heads = b_gates.shape[-1] // 4
    dh = d_mlstm // heads
    n_gates = 4 * heads
    assert dh == LANE and d_conv == d_short and d_conv % LANE == 0
    assert bsz + 1 <= MOD_ROWS and t_lat % GRID_W == 0
    assert w_in.shape[-1] == 2 * d_conv + 4 * d_mlstm + n_gates + 3 * d_short

    tn = d_conv
    chunk = min(256, t_ctx)
    assert d_mlstm % tn == 0 and t_ctx % chunk == 0 and t_lat % chunk == 0
    gate_lo = 2 * d_conv + 4 * d_mlstm
    mt = d_mlstm // tn
    tile = {"a_val": 0, "a_gate": 1, "q": 2, "k": 2 + mt, "v": 2 + 2 * mt, "o": 2 + 3 * mt,
            "s_in": 2 + 4 * mt, "s_b": 3 + 4 * mt, "s_c": 4 + 4 * mt}
    n_tiles = 5 + 4 * mt
    lane_blk = {k_: v_ * (tn // dh) for k_, v_ in tile.items()}
    q_scale = float(dh) ** -0.5

    xl = x.reshape(bsz * t_lat, d)
    xc = ctx.reshape(bsz * t_ctx, d)

    cond = jnp.concatenate([c, c_ctx[None, :], jnp.zeros((MOD_ROWS - bsz - 1, d), F32)], axis=0)
    mod3 = _modulation(cond, w_ada, b_ada).reshape(depth * MOD_ROWS, 1, 6 * d)

    tm_lat = _tile(t_lat, 512)
    tm_ctx = _tile(bsz * t_ctx, 512)
    tm_out = _tile(t_lat, 256)
    tm_out_ctx = _tile(bsz * t_ctx, 256)

    for layer in range(depth):
        last = layer == depth - 1
        base = layer * MOD_ROWS
        w_l = w_in[layer]
        w_main = jnp.concatenate([w_l[:, :gate_lo], w_l[:, gate_lo + n_gates:]], axis=1).astype(BF16)
        wg = w_l[:, gate_lo:gate_lo + n_gates].astype(BF16)
        w_out_b = w_out[layer].astype(BF16)
        w1_b = w_ff1[layer].astype(BF16)
        w2_b = w_ff2[layer].astype(BF16)

        def lat_row(tm):
            return lambda i: base + (i * tm) // t_lat
        ctx_row = lambda i: base + bsz

        proj = functools.partial(_in_proj, mod3=mod3, g_pre=g_pre_mix[layer], w_main=w_main, wg=wg,
                                 wgt=wg.T, bg=b_gates[layer], tn=tn, q_tile_lo=tile["q"],
                                 q_tile_hi=tile["k"], q_scale=q_scale)
        p_lat, gc_lat, gr_lat = proj(xl, mod_row=lat_row(tm_lat), tm=tm_lat, tile_lo=0, n_tiles=n_tiles,
                                     name=f"in_proj_lat_{layer}")
        if last:
            p_ctx, gc_ctx, gr_ctx = proj(xc, mod_row=ctx_row, tm=tm_ctx, tile_lo=tile["k"], n_tiles=2 * mt,
                                         name=f"in_proj_ctx_{layer}")
            ctx_blk = {"k": 0, "v": lane_blk["v"] - lane_blk["k"]}
        else:
            p_ctx, gc_ctx, gr_ctx = proj(xc, mod_row=ctx_row, tm=tm_ctx, tile_lo=0, n_tiles=n_tiles,
                                         name=f"in_proj_ctx_{layer}")
            ctx_blk = lane_blk

        mres = _mlstm(p_lat, p_ctx, (gc_lat, gr_lat), (gc_ctx, gr_ctx), mlstm_norm_w[layer],
                      n_seq=bsz, t_lat=t_lat, t_ctx=t_ctx, heads=heads, dh=dh, chunk=chunk,
                      lat_blk=lane_blk, ctx_blk=ctx_blk, ctx_out=not last, name=f"mlstm_{layer}")

        conv_a = functools.partial(_conv_a, conv_w=conv_a_w[layer], conv_b=conv_a_b[layer],
                                   ln_w=ln_a_w[layer], ln_b=ln_a_b[layer], n_seq=bsz,
                                   val_blk=tile["a_val"], gate_blk=tile["a_gate"])
        conv_c = functools.partial(_conv_c, conv_w=conv_c_w[layer], n_seq=bsz, in_blk=tile["s_in"],
                                   b_blk=tile["s_b"], c_blk=tile["s_c"])
        a_lat = conv_a(p_lat, seq_len=t_lat, row_len=GRID_W, name=f"conv_a_lat_{layer}")
        c_lat = conv_c(p_lat, seq_len=t_lat, shift=GRID_W, name=f"conv_c_lat_{layer}")
        x1, h2 = _out_proj(a_lat, mres[0], c_lat, w_out_b, xl, mod3, lat_row(tm_out), g_post_mix[layer],
                           g_pre_ffn[layer], tm=tm_out, name=f"out_proj_lat_{layer}")
        xl = _ffn(h2, w1_b, w2_b, x1, mod3, lat_row(tm_lat), g_post_ffn[layer], tm=tm_lat, tf=512,
                  name=f"ffn_lat_{layer}")

        if not last:
            a_ctx = conv_a(p_ctx, seq_len=t_ctx, row_len=t_ctx, name=f"conv_a_ctx_{layer}")
            c_ctx_mix = conv_c(p_ctx, seq_len=t_ctx, shift=1, name=f"conv_c_ctx_{layer}")
            x1c, h2c = _out_proj(a_ctx, mres[1], c_ctx_mix, w_out_b, xc, mod3, ctx_row, g_post_mix[layer],
                                 g_pre_ffn[layer], tm=tm_out_ctx, name=f"out_proj_ctx_{layer}")
            xc = _ffn(h2c, w1_b, w2_b, x1c, mod3, ctx_row, g_post_ffn[layer], tm=tm_ctx, tf=512,
                      name=f"ffn_ctx_{layer}")

    return xl.reshape(bsz, t_lat, d)
```

```python
import functools

import jax
import jax.numpy as jnp
from jax import lax
from jax.experimental import pallas as pl
from jax.experimental.pallas import tpu as pltpu

GRID_W = 64
NORM_EPS = 1e-6
LANE = 128
VMEM_LIMIT_BYTES = 56 * 1024 * 1024
MOD_ROWS = 16
MASKED_LOG = -1e30
COMBOS = 16
N_SPLIT = 3

F32 = jnp.float32
BF16 = jnp.bfloat16


def _params(n_grid_axes):
    return pltpu.CompilerParams(
        dimension_semantics=("arbitrary",) * n_grid_axes,
        vmem_limit_bytes=VMEM_LIMIT_BYTES)


def _rms(y):
    return y * lax.rsqrt(jnp.mean(y * y, axis=-1, keepdims=True) + NORM_EPS)


def _split3(x):
    hi = x.astype(BF16)
    r1 = x - hi.astype(F32)
    mid = r1.astype(BF16)
    lo = (r1 - mid.astype(F32)).astype(BF16)
    return hi, mid, lo


def _mod_kernel(s_ref, w_ref, b_ref, o_ref):
    s = s_ref[...]
    a = (s * jax.nn.sigmoid(s)).astype(BF16)
    o_ref[0] = jnp.dot(a, w_ref[0].astype(BF16), preferred_element_type=F32) + b_ref[0]


def _modulation(s, w_ada, b_ada, tn=1024):
    depth, d, n = w_ada.shape
    return pl.pallas_call(
        _mod_kernel,
        grid=(depth, n // tn),
        in_specs=[pl.BlockSpec((MOD_ROWS, d), lambda l, j: (0, 0)),
                  pl.BlockSpec((1, d, tn), lambda l, j: (l, 0, j)),
                  pl.BlockSpec((1, 1, tn), lambda l, j: (l, 0, j))],
        out_specs=pl.BlockSpec((1, MOD_ROWS, tn), lambda l, j: (l, 0, j)),
        out_shape=jax.ShapeDtypeStruct((depth, MOD_ROWS, n), F32),
        compiler_params=_params(2),
        name="adaln_modulation",
    )(s, w_ada, b_ada.reshape(depth, 1, n))


def _in_proj_kernel(x_ref, g_ref, sh_ref, sc_ref, w_ref, wg_ref, bg_ref, p_ref, gates_ref,
                    *, n_tiles, tn, q_tile_lo, q_tile_hi, q_scale):
    gain = g_ref[...] * (1.0 + sc_ref[0])
    hb = (_rms(x_ref[...]) * gain + sh_ref[0]).astype(BF16)
    gates_ref[...] = jnp.dot(hb, wg_ref[...], preferred_element_type=F32) + bg_ref[...]
    for j in range(n_tiles):
        cols = slice(j * tn, (j + 1) * tn)
        acc = jnp.dot(hb, w_ref[:, cols], preferred_element_type=F32)
        if q_tile_lo <= j < q_tile_hi:
            acc = acc * q_scale
        p_ref[:, cols] = acc.astype(BF16)


def _in_proj(x2d, mod3, mod_row, g_pre, w, wg, bg, *, tm, tn, q_tile_lo, q_tile_hi, q_scale, name):
    n, d = x2d.shape
    nw = w.shape[1]
    ng = wg.shape[1]
    kern = functools.partial(_in_proj_kernel, n_tiles=nw // tn, tn=tn, q_tile_lo=q_tile_lo,
                             q_tile_hi=q_tile_hi, q_scale=q_scale)
    fixed = lambda i: (0, 0)
    return pl.pallas_call(
        kern,
        grid=(n // tm,),
        in_specs=[pl.BlockSpec((tm, d), lambda i: (i, 0)),
                  pl.BlockSpec((1, d), fixed),
                  pl.BlockSpec((1, 1, d), lambda i: (mod_row(i), 0, 0)),
                  pl.BlockSpec((1, 1, d), lambda i: (mod_row(i), 0, 1)),
                  pl.BlockSpec((d, nw), fixed, pipeline_mode=pl.Buffered(1)),
                  pl.BlockSpec((d, ng), fixed),
                  pl.BlockSpec((1, ng), fixed)],
        out_specs=[pl.BlockSpec((tm, nw), lambda i: (i, 0)),
                   pl.BlockSpec((tm, ng), lambda i: (i, 0))],
        out_shape=[jax.ShapeDtypeStruct((n, nw), BF16),
                   jax.ShapeDtypeStruct((n, ng), F32)],
        compiler_params=_params(1),
        name=name,
    )(x2d, g_pre.reshape(1, d), mod3, mod3, w, wg, bg)


CONV_A_PAD = 16
CONV_SUB = 64


def _conv_a_kernel(av_ref, ag_ref, w_ref, b_ref, lnw_ref, lnb_ref, o_ref, upad_ref, y_ref,
                   *, row_len, n_rows, taps):
    ch = o_ref.shape[-1]
    half = taps // 2
    zeros = jnp.zeros((CONV_A_PAD, ch), F32)
    upad_ref[0:CONV_A_PAD, :] = zeros
    upad_ref[CONV_A_PAD + row_len:CONV_A_PAD + row_len + CONV_A_PAD, :] = zeros

    def row_body(r, carry):
        r0 = pl.multiple_of(r * row_len, row_len)
        av = av_ref[pl.ds(r0, row_len), :].astype(F32)
        ag = ag_ref[pl.ds(r0, row_len), :].astype(F32)
        upad_ref[CONV_A_PAD:CONV_A_PAD + row_len, :] = av * jax.nn.sigmoid(ag)
        for sb in range(row_len // CONV_SUB):
            for c in range(ch // LANE):
                cs = slice(c * LANE, (c + 1) * LANE)
                acc = jnp.zeros((CONV_SUB, LANE), F32)
                for k in range(taps):
                    start = CONV_A_PAD - half + k + sb * CONV_SUB
                    acc = acc + w_ref[k:k + 1, cs] * upad_ref[start:start + CONV_SUB, cs]
                y_ref[:, cs] = acc + b_ref[:, cs]
            y = y_ref[...]
            mu = jnp.mean(y, axis=-1, keepdims=True)
            dlt = y - mu
            var = jnp.mean(dlt * dlt, axis=-1, keepdims=True)
            z = dlt * lax.rsqrt(var + NORM_EPS) * lnw_ref[...] + lnb_ref[...]
            o_ref[pl.ds(r0 + sb * CONV_SUB, CONV_SUB), :] = (z * jax.nn.sigmoid(z)).astype(BF16)
        return carry

    lax.fori_loop(0, n_rows, row_body, 0)


def _conv_a(p, conv_w, conv_b, ln_w, ln_b, *, n_seq, seq_len, row_len, val_blk, gate_blk, name):
    taps, ch = conv_w.shape
    n = n_seq * seq_len
    kern = functools.partial(_conv_a_kernel, row_len=row_len, n_rows=seq_len // row_len, taps=taps)
    vec = lambda a: a.reshape(1, ch)
    return pl.pallas_call(
        kern,
        grid=(n_seq,),
        in_specs=[pl.BlockSpec((seq_len, ch), lambda b: (b, val_blk)),
                  pl.BlockSpec((seq_len, ch), lambda b: (b, gate_blk)),
                  pl.BlockSpec((taps, ch), lambda b: (0, 0)),
                  pl.BlockSpec((1, ch), lambda b: (0, 0)),
                  pl.BlockSpec((1, ch), lambda b: (0, 0)),
                  pl.BlockSpec((1, ch), lambda b: (0, 0))],
        out_specs=pl.BlockSpec((seq_len, ch), lambda b: (b, 0)),
        out_shape=jax.ShapeDtypeStruct((n, ch), BF16),
        scratch_shapes=[pltpu.VMEM((row_len + 2 * CONV_A_PAD, ch), F32),
                        pltpu.VMEM((CONV_SUB, ch), F32)],
        compiler_params=_params(1),
        name=name,
    )(p, p, conv_w, vec(conv_b), vec(ln_w), vec(ln_b))


CONV_C_PAD = 64
CONV_C_BLK = 64


def _conv_c_kernel(sin_ref, sb_ref, sc_ref, w_ref, o_ref, upad_ref, *, seq_len, shift):
    ch = o_ref.shape[-1]
    zeros = jnp.zeros((CONV_C_PAD, ch), F32)
    upad_ref[0:CONV_C_PAD, :] = zeros
    upad_ref[CONV_C_PAD + seq_len:CONV_C_PAD + seq_len + CONV_C_PAD, :] = zeros
    for i in range(seq_len // CONV_C_BLK):
        t = slice(i * CONV_C_BLK, (i + 1) * CONV_C_BLK)
        upad_ref[CONV_C_PAD + i * CONV_C_BLK:CONV_C_PAD + (i + 1) * CONV_C_BLK, :] = (
            sc_ref[t, :].astype(F32) * sin_ref[t, :].astype(F32))
    for i in range(seq_len // CONV_C_BLK):
        t0 = CONV_C_PAD + i * CONV_C_BLK
        y = (w_ref[0:1, :] * upad_ref[t0 - shift:t0 - shift + CONV_C_BLK, :]
             + w_ref[1:2, :] * upad_ref[t0:t0 + CONV_C_BLK, :]
             + w_ref[2:3, :] * upad_ref[t0 + shift:t0 + shift + CONV_C_BLK, :])
        t = slice(i * CONV_C_BLK, (i + 1) * CONV_C_BLK)
        o_ref[t, :] = (sb_ref[t, :].astype(F32) * y).astype(BF16)


def _conv_c(p, conv_w, *, n_seq, seq_len, shift, in_blk, b_blk, c_blk, name):
    taps, ch = conv_w.shape
    assert taps == 3 and shift <= CONV_C_PAD
    n = n_seq * seq_len
    kern = functools.partial(_conv_c_kernel, seq_len=seq_len, shift=shift)
    return pl.pallas_call(
        kern,
        grid=(n_seq,),
        in_specs=[pl.BlockSpec((seq_len, ch), lambda b: (b, in_blk)),
                  pl.BlockSpec((seq_len, ch), lambda b: (b, b_blk)),
                  pl.BlockSpec((seq_len, ch), lambda b: (b, c_blk)),
                  pl.BlockSpec((taps, ch), lambda b: (0, 0))],
        out_specs=pl.BlockSpec((seq_len, ch), lambda b: (b, 0)),
        out_shape=jax.ShapeDtypeStruct((n, ch), BF16),
        scratch_shapes=[pltpu.VMEM((seq_len + 2 * CONV_C_PAD, ch), F32)],
        compiler_params=_params(1),
        name=name,
    )(p, p, p, conv_w)


def _log_sigmoid(z):
    return jnp.minimum(z, 0.0) - jnp.log1p(jnp.exp(-jnp.abs(z)))


def _gate_stats_kernel(g_ref, xs_ref, rows_ref, *, heads):
    chunk = g_ref.shape[0]
    li = g_ref[:, 0:LANE]
    lf = _log_sigmoid(g_ref[:, LANE:2 * LANE])
    lane = lax.broadcasted_iota(jnp.int32, (chunk, LANE), 1)
    row = lax.broadcasted_iota(jnp.int32, (chunk, LANE), 0)
    fwd = lane < heads

    jj = lax.broadcasted_iota(jnp.int32, (chunk, chunk), 0)
    ss = lax.broadcasted_iota(jnp.int32, (chunk, chunk), 1)
    tri_p = jnp.where(ss <= jj, 1.0, 0.0).astype(BF16)
    tri_s = jnp.where(ss >= jj, 1.0, 0.0).astype(BF16)
    b_p = jnp.zeros((chunk, LANE), F32)
    b_s = jnp.zeros((chunk, LANE), F32)
    for piece in _split3(lf):
        b_p = b_p + jnp.dot(tri_p, piece, preferred_element_type=F32)
        b_s = b_s + jnp.dot(tri_s, piece, preferred_element_type=F32)
    b = jnp.where(fwd, b_p, b_s)
    a = li - b

    run_max = a
    sh = 1
    while sh < chunk:
        up = pltpu.roll(run_max, sh, axis=0)
        dn = pltpu.roll(run_max, chunk - sh, axis=0)
        cand = jnp.where(fwd, jnp.where(row >= sh, up, MASKED_LOG),
                         jnp.where(row < chunk - sh, dn, MASKED_LOG))
        run_max = jnp.maximum(run_max, cand)
        sh *= 2

    a_rows = a.T[0:COMBOS, :]
    b_rows = b.T[0:COMBOS, :]
    rows_ref[0, 0:COMBOS, :] = a_rows
    rows_ref[0, COMBOS:2 * COMBOS, :] = jnp.broadcast_to(
        jnp.max(a_rows, axis=1, keepdims=True), (COMBOS, chunk))
    rows_ref[0, 2 * COMBOS:3 * COMBOS, :] = jnp.broadcast_to(
        jnp.min(b_rows, axis=1, keepdims=True), (COMBOS, chunk))

    packed = jnp.zeros((chunk, LANE), F32)
    pieces = _split3(run_max) + _split3(b)
    for i, piece in enumerate(pieces):
        val = jnp.where(lane < COMBOS, piece.astype(F32), 0.0)
        packed = packed + (pltpu.roll(val, COMBOS * i, axis=1) if i else val)
    xs_ref[...] = packed.astype(BF16)


def _gate_stats(gates, *, chunk, heads, name):
    n = gates.shape[0]
    assert 2 * heads <= COMBOS and 2 * N_SPLIT * COMBOS <= LANE
    return pl.pallas_call(
        functools.partial(_gate_stats_kernel, heads=heads),
        grid=(n // chunk,),
        in_specs=[pl.BlockSpec((chunk, 2 * LANE), lambda i: (i, 0))],
        out_specs=[pl.BlockSpec((chunk, LANE), lambda i: (i, 0)),
                   pl.BlockSpec((1, 3 * COMBOS, chunk), lambda i: (i, 0, 0))],
        out_shape=[jax.ShapeDtypeStruct((n, LANE), BF16),
                   jax.ShapeDtypeStruct((n // chunk, 3 * COMBOS, chunk), F32)],
        compiler_params=_params(1),
        name=name,
    )(gates)


def _twice(r):
    return jnp.concatenate([r, r], axis=1)


def _mlstm_chunk(q, k, v, xs, a_row, a_max, b_last, sel, c_ext, m_st, *, reverse, need_h):
    chunk, dh = k.shape
    v_ext = jnp.concatenate([v, jnp.ones_like(v)], axis=1)
    k_t = k.T

    h = None
    if need_h:
        stats = jnp.dot(xs, sel, preferred_element_type=F32)
        a_run, b_col = stats[:, :dh], stats[:, dh:]
        jj = lax.broadcasted_iota(jnp.int32, (chunk, chunk), 0)
        ss = lax.broadcasted_iota(jnp.int32, (chunk, chunk), 1)
        seen = (ss >= jj) if reverse else (ss <= jj)
        a_run_wide = jnp.concatenate([a_run] * (chunk // dh), axis=1)
        w0 = jnp.exp(jnp.where(seen, a_row - a_run_wide, MASKED_LOG))
        scores = jnp.dot(q, k_t, preferred_element_type=F32) * w0
        intra = jnp.dot(scores.astype(BF16), v_ext, preferred_element_type=F32)
        inter = jnp.dot(q, c_ext.astype(BF16), preferred_element_type=F32)
        gap = jnp.exp(-jnp.abs(a_run - m_st))
        intra_first = a_run >= m_st
        r_intra = jnp.where(intra_first, 1.0, gap)
        r_inter = jnp.where(intra_first, gap, 1.0)
        num = r_intra * intra[:, :dh] + r_inter * inter[:, :dh]
        den = r_intra * intra[:, dh:] + r_inter * inter[:, dh:]
        floor = jnp.exp(-(b_col + jnp.maximum(a_run, m_st)))
        h = num / jnp.maximum(jnp.abs(den), floor)

    g_last = jnp.maximum(a_max, m_st)
    new_scale = jnp.exp(a_max - g_last)
    decay = jnp.exp(m_st - g_last)
    a_max_row = jnp.concatenate([a_max] * (chunk // dh), axis=1)
    kw_t = (k_t.astype(F32) * jnp.exp(a_row - a_max_row)).astype(BF16)
    update = jnp.dot(kw_t, v_ext, preferred_element_type=F32)
    c_new = _twice(decay) * c_ext + _twice(new_scale) * update
    return h, c_new, b_last + g_last


def _mlstm_kernel(*refs, chunk, n_lat, n_ctx, ctx_out, heads):
    if ctx_out:
        (ql, kl, vl, ol, qc, kc, vc, oc, xsl, rwl, xsc, rwc, nw,
         out_l, out_c, hf_l, hb_l, hf_c, hb_c, c_ref, m_ref) = refs
    else:
        (ql, kl, vl, ol, kc, vc, xsl, rwl, xsc, rwc, nw,
         out_l, hf_l, hb_l, c_ref, m_ref) = refs
        qc = oc = out_c = hf_c = hb_c = None
    dh = nw.shape[-1]
    head = pl.program_id(1)

    c_ref[...] = jnp.zeros_like(c_ref)
    m_ref[...] = jnp.zeros_like(m_ref)

    kk = lax.broadcasted_iota(jnp.int32, (LANE, 2 * dh), 0)
    nn = lax.broadcasted_iota(jnp.int32, (LANE, 2 * dh), 1)
    piece = jnp.right_shift(kk, COMBOS.bit_length() - 1)
    first_piece = jnp.where(nn < dh, 0, N_SPLIT)
    target = (piece >= first_piece) & (piece < first_piece + N_SPLIT)
    pair_lane = jnp.bitwise_and(kk, COMBOS - 1)
    sels = [jnp.where(target & (pair_lane == direction * heads + head), 1.0, 0.0).astype(BF16)
            for direction in (0, 1)]

    def scan(q_ref, k_ref, v_ref, xs_ref, rw_ref, hf_ref, hb_ref, n_chunks, need_h):
        def step(i, direction):
            rows = pl.ds(pl.multiple_of(i * chunk, chunk), chunk)
            pair = direction * heads + head
            a_row = rw_ref[i, pl.ds(pair, 1), :]
            a_max = rw_ref[i, pl.ds(COMBOS + pair, 1), :][:, 0:dh]
            b_last = rw_ref[i, pl.ds(2 * COMBOS + pair, 1), :][:, 0:dh]
            h, c_new, m_new = _mlstm_chunk(
                q_ref[rows, :] if need_h else None, k_ref[rows, :], v_ref[rows, :], xs_ref[rows, :],
                a_row, a_max, b_last, sels[direction], c_ref[direction], m_ref[direction],
                reverse=direction == 1, need_h=need_h)
            c_ref[direction] = c_new
            m_ref[direction] = m_new
            if need_h:
                (hf_ref, hb_ref)[direction][rows, :] = h

        def body(i, carry):
            step(i, 0)
            step(n_chunks - 1 - i, 1)
            return carry

        lax.fori_loop(0, n_chunks, body, 0)

    def finish(hf_ref, hb_ref, o_ref, out_ref, n_chunks):
        def body(i, carry):
            rows = pl.ds(pl.multiple_of(i * chunk, chunk), chunk)
            h = hf_ref[rows, :] + hb_ref[rows, :]
            mu = jnp.mean(h, axis=-1, keepdims=True)
            dlt = h - mu
            var = jnp.mean(dlt * dlt, axis=-1, keepdims=True)
            y = dlt * lax.rsqrt(var + NORM_EPS) * nw[...]
            out_ref[rows, :] = (y * jax.nn.sigmoid(o_ref[rows, :].astype(F32))).astype(BF16)
            return carry

        lax.fori_loop(0, n_chunks, body, 0)

    scan(qc, kc, vc, xsc, rwc, hf_c, hb_c, n_ctx, ctx_out)
    scan(ql, kl, vl, xsl, rwl, hf_l, hb_l, n_lat, True)
    finish(hf_l, hb_l, ol, out_l, n_lat)
    if ctx_out:
        finish(hf_c, hb_c, oc, out_c, n_ctx)


def _mlstm(p_lat, p_ctx, stats_lat, stats_ctx, norm_w, *, n_seq, t_lat, t_ctx, heads, dh, chunk,
           lat_blk, ctx_blk, ctx_out, name):
    assert chunk % dh == 0
    n_lat, n_ctx = t_lat // chunk, t_ctx // chunk

    def tok_spec(t, blk):
        return pl.BlockSpec((t, dh), lambda b, h: (b, blk + h))

    def stat_specs(t, n_chunks):
        return [pl.BlockSpec((t, LANE), lambda b, h: (b, 0)),
                pl.BlockSpec((n_chunks, 3 * COMBOS, chunk), lambda b, h: (b, 0, 0))]

    lat_in = [tok_spec(t_lat, lat_blk[name_]) for name_ in "qkvo"]
    ctx_names = "qkvo" if ctx_out else "kv"
    ctx_in = [tok_spec(t_ctx, ctx_blk[name_]) for name_ in ctx_names]
    in_specs = (lat_in + ctx_in + stat_specs(t_lat, n_lat) + stat_specs(t_ctx, n_ctx)
                + [pl.BlockSpec((1, dh), lambda b, h: (0, h))])
    args = ([p_lat] * 4 + [p_ctx] * len(ctx_names) + list(stats_lat) + list(stats_ctx)
            + [norm_w.reshape(1, heads * dh)])

    out_specs = [pl.BlockSpec((t_lat, dh), lambda b, h: (b, h))]
    out_shape = [jax.ShapeDtypeStruct((n_seq * t_lat, heads * dh), BF16)]
    scratch = [pltpu.VMEM((t_lat, dh), F32), pltpu.VMEM((t_lat, dh), F32)]
    if ctx_out:
        out_specs.append(pl.BlockSpec((t_ctx, dh), lambda b, h: (b, h)))
        out_shape.append(jax.ShapeDtypeStruct((n_seq * t_ctx, heads * dh), BF16))
        scratch += [pltpu.VMEM((t_ctx, dh), F32), pltpu.VMEM((t_ctx, dh), F32)]
    scratch += [pltpu.VMEM((2, dh, 2 * dh), F32), pltpu.VMEM((2, 1, dh), F32)]

    kern = functools.partial(_mlstm_kernel, chunk=chunk, n_lat=n_lat, n_ctx=n_ctx, ctx_out=ctx_out,
                             heads=heads)
    return pl.pallas_call(
        kern,
        grid=(n_seq, heads),
        in_specs=in_specs,
        out_specs=out_specs,
        out_shape=out_shape,
        scratch_shapes=scratch,
        compiler_params=_params(2),
        name=name,
    )(*args)


def _out_proj_kernel(a_ref, m_ref, c_ref, w_ref, x_ref, gpost_ref, g1_ref, gpre_ref, sh2_ref, sc2_ref,
                     x1_ref, h2_ref):
    da, dm = a_ref.shape[-1], m_ref.shape[-1]
    y = (jnp.dot(a_ref[...], w_ref[0:da, :], preferred_element_type=F32)
         + jnp.dot(m_ref[...], w_ref[da:da + dm, :], preferred_element_type=F32)
         + jnp.dot(c_ref[...], w_ref[da + dm:, :], preferred_element_type=F32))
    x1 = x_ref[...] + g1_ref[0] * (_rms(y) * gpost_ref[...])
    x1_ref[...] = x1
    h2 = _rms(x1) * gpre_ref[...] * (1.0 + sc2_ref[0]) + sh2_ref[0]
    h2_ref[...] = h2.astype(BF16)


def _out_proj(a, m, cc, w_out, x2d, mod3, mod_row, g_post, g_pre_ffn, *, tm, name):
    n, d = x2d.shape
    da, dm, dc = a.shape[1], m.shape[1], cc.shape[1]
    row = lambda i: (i, 0)
    fixed = lambda i: (0, 0)
    mod_spec = lambda k: pl.BlockSpec((1, 1, d), lambda i: (mod_row(i), 0, k))
    return pl.pallas_call(
        _out_proj_kernel,
        grid=(n // tm,),
        in_specs=[pl.BlockSpec((tm, da), row), pl.BlockSpec((tm, dm), row), pl.BlockSpec((tm, dc), row),
                  pl.BlockSpec((da + dm + dc, d), fixed),
                  pl.BlockSpec((tm, d), row),
                  pl.BlockSpec((1, d), fixed),
                  mod_spec(2),
                  pl.BlockSpec((1, d), fixed),
                  mod_spec(3), mod_spec(4)],
        out_specs=[pl.BlockSpec((tm, d), row), pl.BlockSpec((tm, d), row)],
        out_shape=[jax.ShapeDtypeStruct((n, d), F32), jax.ShapeDtypeStruct((n, d), BF16)],
        compiler_params=_params(1),
        name=name,
    )(a, m, cc, w_out, x2d, g_post.reshape(1, d), mod3, g_pre_ffn.reshape(1, d), mod3, mod3)


def _ffn_kernel(h_ref, w1_ref, w2_ref, x_ref, gpost_ref, g2_ref, o_ref, acc_ref):
    j = pl.program_id(1)

    @pl.when(j == 0)
    def _():
        acc_ref[...] = jnp.zeros_like(acc_ref)

    u = jnp.maximum(jnp.dot(h_ref[...], w1_ref[...], preferred_element_type=F32), 0.0)
    acc_ref[...] += jnp.dot((u * u).astype(BF16), w2_ref[...], preferred_element_type=F32)

    @pl.when(j == pl.num_programs(1) - 1)
    def _():
        o_ref[...] = x_ref[...] + g2_ref[0] * (_rms(acc_ref[...]) * gpost_ref[...])


def _ffn(h2, w1, w2, x1, mod3, mod_row, g_post, *, tm, tf, name):
    n, d = x1.shape
    dff = w1.shape[1]
    return pl.pallas_call(
        _ffn_kernel,
        grid=(n // tm, dff // tf),
        in_specs=[pl.BlockSpec((tm, d), lambda i, j: (i, 0)),
                  pl.BlockSpec((d, tf), lambda i, j: (0, j)),
                  pl.BlockSpec((tf, d), lambda i, j: (j, 0)),
                  pl.BlockSpec((tm, d), lambda i, j: (i, 0)),
                  pl.BlockSpec((1, d), lambda i, j: (0, 0)),
                  pl.BlockSpec((1, 1, d), lambda i, j: (mod_row(i), 0, 5))],
        out_specs=pl.BlockSpec((tm, d), lambda i, j: (i, 0)),
        out_shape=jax.ShapeDtypeStruct((n, d), F32),
        scratch_shapes=[pltpu.VMEM((tm, d), F32)],
        compiler_params=_params(2),
        name=name,
    )(h2, w1, w2, x1, g_post.reshape(1, d), mod3)


def _tile(n, target):
    t = min(n, target)
    while n % t or (t % 8 and t != n):
        t -= 1
    return t


def kernel(x, c, ctx, c_ctx, w_ada, b_ada, g_pre_mix, g_post_mix, g_pre_ffn, g_post_ffn, w_in, b_gates,
           conv_a_w, conv_a_b, ln_a_w, ln_a_b, mlstm_norm_w, conv_c_w, w_out, w_ff1, w_ff2):
    bsz, t_lat, d = x.shape
    t_ctx = ctx.shape[1]
    depth = w_ada.shape[0]
    d_conv = conv_a_w.shape[-1]
    d_short = conv_c_w.shape[-1]
    d_mlstm = mlstm_norm_w.shape[-1]
    heads = b_gates.shape[-1] // 4
    dh = d_mlstm // heads
    n_gates = 4 * heads
    assert dh == LANE and d_conv == d_short and d_conv % LANE == 0
    assert bsz + 1 <= MOD_ROWS and t_lat % GRID_W == 0
    assert w_in.shape[-1] == 2 * d_conv + 4 * d_mlstm + n_gates + 3 * d_short

    tn = d_conv
    chunk = min(256, t_ctx)
    assert d_mlstm % tn == 0 and t_ctx % chunk == 0 and t_lat % chunk == 0
    gate_lo = 2 * d_conv + 4 * d_mlstm
    mt = d_mlstm // tn
    tile = {"a_val": 0, "a_gate": 1, "q": 2, "k": 2 + mt, "v": 2 + 2 * mt, "o": 2 + 3 * mt,
            "s_in": 2 + 4 * mt, "s_b": 3 + 4 * mt, "s_c": 4 + 4 * mt}
    lane_blk = {k_: v_ * (tn // dh) for k_, v_ in tile.items()}
    q_scale = float(dh) ** -0.5

    xl = x.reshape(bsz * t_lat, d)
    xc = ctx.reshape(bsz * t_ctx, d)

    cond = jnp.concatenate([c, c_ctx[None, :], jnp.zeros((MOD_ROWS - bsz - 1, d), F32)], axis=0)
    mod3 = _modulation(cond, w_ada, b_ada).reshape(depth * MOD_ROWS, 1, 6 * d)

    tm_in = _tile(t_lat, 256)
    tm_in_ctx = _tile(bsz * t_ctx, 256)
    tm_lat = _tile(t_lat, 512)
    tm_ctx = _tile(bsz * t_ctx, 512)
    tm_out = _tile(t_lat, 256)
    tm_out_ctx = _tile(bsz * t_ctx, 256)

    def gate_cols(g):
        i_f, f_f, i_b, f_b = jnp.split(g, 4, axis=-1)
        pad = jnp.zeros(g.shape[:-1] + (LANE - 2 * heads,), g.dtype)
        return jnp.concatenate([i_f, i_b, pad, f_f, f_b, pad], axis=-1)

    for layer in range(depth):
        last = layer == depth - 1
        base = layer * MOD_ROWS
        w_l = w_in[layer]
        w_main = jnp.concatenate([w_l[:, :gate_lo], w_l[:, gate_lo + n_gates:]], axis=1).astype(BF16)
        wg = gate_cols(w_l[:, gate_lo:gate_lo + n_gates]).astype(BF16)
        bg = gate_cols(b_gates[layer]).reshape(1, 2 * LANE)
        w_out_b = w_out[layer].astype(BF16)
        w1_b = w_ff1[layer].astype(BF16)
        w2_b = w_ff2[layer].astype(BF16)

        def lat_row(tm):
            return lambda i: base + (i * tm) // t_lat
        ctx_row = lambda i: base + bsz

        proj = functools.partial(_in_proj, mod3=mod3, g_pre=g_pre_mix[layer], wg=wg, bg=bg, tn=tn,
                                 q_scale=q_scale)
        p_lat, g_lat = proj(xl, mod_row=lat_row(tm_in), w=w_main, tm=tm_in, q_tile_lo=tile["q"],
                            q_tile_hi=tile["k"], name=f"in_proj_lat_{layer}")
        if last:
            w_kv = w_main[:, tile["k"] * tn:tile["o"] * tn]
            p_ctx, g_ctx = proj(xc, mod_row=ctx_row, w=w_kv, tm=tm_in_ctx, q_tile_lo=0, q_tile_hi=0,
                                name=f"in_proj_ctx_{layer}")
            ctx_blk = {"k": 0, "v": lane_blk["v"] - lane_blk["k"]}
        else:
            p_ctx, g_ctx = proj(xc, mod_row=ctx_row, w=w_main, tm=tm_in_ctx, q_tile_lo=tile["q"],
                                q_tile_hi=tile["k"], name=f"in_proj_ctx_{layer}")
            ctx_blk = lane_blk

        stats_lat = _gate_stats(g_lat, chunk=chunk, heads=heads, name=f"gate_stats_lat_{layer}")
        stats_ctx = _gate_stats(g_ctx, chunk=chunk, heads=heads, name=f"gate_stats_ctx_{layer}")
        mres = _mlstm(p_lat, p_ctx, stats_lat, stats_ctx, mlstm_norm_w[layer],
                      n_seq=bsz, t_lat=t_lat, t_ctx=t_ctx, heads=heads, dh=dh, chunk=chunk,
                      lat_blk=lane_blk, ctx_blk=ctx_blk, ctx_out=not last, name=f"mlstm_{layer}")

        conv_a = functools.partial(_conv_a, conv_w=conv_a_w[layer], conv_b=conv_a_b[layer],
                                   ln_w=ln_a_w[layer], ln_b=ln_a_b[layer], n_seq=bsz,
                                   val_blk=tile["a_val"], gate_blk=tile["a_gate"])
        conv_c = functools.partial(_conv_c, conv_w=conv_c_w[layer], n_seq=bsz, in_blk=tile["s_in"],
                                   b_blk=tile["s_b"], c_blk=tile["s_c"])
        a_lat = conv_a(p_lat, seq_len=t_lat, row_len=GRID_W, name=f"conv_a_lat_{layer}")
        c_lat = conv_c(p_lat, seq_len=t_lat, shift=GRID_W, name=f"conv_c_lat_{layer}")
        x1, h2 = _out_proj(a_lat, mres[0], c_lat, w_out_b, xl, mod3, lat_row(tm_out), g_post_mix[layer],
                           g_pre_ffn[layer], tm=tm_out, name=f"out_proj_lat_{layer}")
        xl = _ffn(h2, w1_b, w2_b, x1, mod3, lat_row(tm_lat), g_post_ffn[layer], tm=tm_lat, tf=512,
                  name=f"ffn_lat_{layer}")

        if not last:
            a_ctx = conv_a(p_ctx, seq_len=t_ctx, row_len=t_ctx, name=f"conv_a_ctx_{layer}")
            c_ctx_mix = conv_c(p_ctx, seq_len=t_ctx, shift=1, name=f"conv_c_ctx_{layer}")
            x1c, h2c = _out_proj(a_ctx, mres[1], c_ctx_mix, w_out_b, xc, mod3, ctx_row, g_post_mix[layer],
                                 g_pre_ffn[layer], tm=tm_out_ctx, name=f"out_proj_ctx_{layer}")
            xc = _ffn(h2c, w1_b, w2_b, x1c, mod3, ctx_row, g_post_ffn[layer], tm=tm_ctx, tf=512,
                      name=f"ffn_ctx_{layer}")

    return xl.reshape(bsz, t_lat, d)
```

```python
import functools

import jax
import jax.numpy as jnp
from jax import lax
from jax.experimental import pallas as pl
from jax.experimental.pallas import tpu as pltpu

GRID_W = 64
NORM_EPS = 1e-6
LANE = 128
SUBLANE = 8
VMEM_LIMIT_BYTES = 56 * 1024 * 1024
MOD_ROWS = 16
MASKED_LOG = -1e30
COMBOS = 16
N_SPLIT = 3

F32 = jnp.float32
BF16 = jnp.bfloat16


def _params(n_grid_axes):
    return pltpu.CompilerParams(
        dimension_semantics=("arbitrary",) * n_grid_axes,
        vmem_limit_bytes=VMEM_LIMIT_BYTES)


def _rms(y):
    return y * lax.rsqrt(jnp.mean(y * y, axis=-1, keepdims=True) + NORM_EPS)


def _split3(x):
    hi = x.astype(BF16)
    r1 = x - hi.astype(F32)
    mid = r1.astype(BF16)
    lo = (r1 - mid.astype(F32)).astype(BF16)
    return hi, mid, lo


def _mod_kernel(s_ref, w_ref, b_ref, o_ref):
    s = s_ref[...]
    a = (s * jax.nn.sigmoid(s)).astype(BF16)
    o_ref[0] = jnp.dot(a, w_ref[0].astype(BF16), preferred_element_type=F32) + b_ref[0]


def _modulation(s, w_ada, b_ada, tn=1024):
    depth, d, n = w_ada.shape
    return pl.pallas_call(
        _mod_kernel,
        grid=(depth, n // tn),
        in_specs=[pl.BlockSpec((MOD_ROWS, d), lambda l, j: (0, 0)),
                  pl.BlockSpec((1, d, tn), lambda l, j: (l, 0, j)),
                  pl.BlockSpec((1, 1, tn), lambda l, j: (l, 0, j))],
        out_specs=pl.BlockSpec((1, MOD_ROWS, tn), lambda l, j: (l, 0, j)),
        out_shape=jax.ShapeDtypeStruct((depth, MOD_ROWS, n), F32),
        compiler_params=_params(2),
        name="adaln_modulation",
    )(s, w_ada, b_ada.reshape(depth, 1, n))


def _in_proj_kernel(x_ref, g_ref, sh_ref, sc_ref, w_ref, wg_ref, bg_ref, p_ref, gates_ref,
                    *, n_tiles, tn, q_tile_lo, q_tile_hi, q_scale):
    gain = g_ref[...] * (1.0 + sc_ref[0])
    hb = (_rms(x_ref[...]) * gain + sh_ref[0]).astype(BF16)
    gates_ref[...] = jnp.dot(hb, wg_ref[...], preferred_element_type=F32) + bg_ref[...]
    for j in range(n_tiles):
        cols = slice(j * tn, (j + 1) * tn)
        acc = jnp.dot(hb, w_ref[:, cols], preferred_element_type=F32)
        if q_tile_lo <= j < q_tile_hi:
            acc = acc * q_scale
        p_ref[:, cols] = acc.astype(BF16)


def _in_proj(x2d, mod3, mod_row, g_pre, w, wg, bg, *, layer, nw, col_blk, tm, tn, q_tile_lo, q_tile_hi,
             q_scale, name):
    n, d = x2d.shape
    ng = wg.shape[1]
    kern = functools.partial(_in_proj_kernel, n_tiles=nw // tn, tn=tn, q_tile_lo=q_tile_lo,
                             q_tile_hi=q_tile_hi, q_scale=q_scale)
    fixed = lambda i: (0, 0)
    return pl.pallas_call(
        kern,
        grid=(n // tm,),
        in_specs=[pl.BlockSpec((tm, d), lambda i: (i, 0)),
                  pl.BlockSpec((1, d), fixed),
                  pl.BlockSpec((1, 1, d), lambda i: (mod_row(i), 0, 0)),
                  pl.BlockSpec((1, 1, d), lambda i: (mod_row(i), 0, 1)),
                  pl.BlockSpec((None, d, nw), lambda i: (layer, 0, col_blk), pipeline_mode=pl.Buffered(1)),
                  pl.BlockSpec((d, ng), fixed),
                  pl.BlockSpec((1, ng), fixed)],
        out_specs=[pl.BlockSpec((tm, nw), lambda i: (i, 0)),
                   pl.BlockSpec((tm, ng), lambda i: (i, 0))],
        out_shape=[jax.ShapeDtypeStruct((n, nw), BF16),
                   jax.ShapeDtypeStruct((n, ng), F32)],
        compiler_params=_params(1),
        name=name,
    )(x2d, g_pre.reshape(1, d), mod3, mod3, w, wg, bg)


CONV_A_PAD = 16
CONV_SUB = 64


def _conv_a_kernel(av_ref, ag_ref, w_ref, b_ref, lnw_ref, lnb_ref, o_ref, upad_ref, y_ref,
                   *, row_len, n_rows, taps):
    ch = o_ref.shape[-1]
    half = taps // 2
    zeros = jnp.zeros((CONV_A_PAD, ch), F32)
    upad_ref[0:CONV_A_PAD, :] = zeros
    upad_ref[CONV_A_PAD + row_len:CONV_A_PAD + row_len + CONV_A_PAD, :] = zeros

    def row_body(r, carry):
        r0 = pl.multiple_of(r * row_len, row_len)
        av = av_ref[pl.ds(r0, row_len), :].astype(F32)
        ag = ag_ref[pl.ds(r0, row_len), :].astype(F32)
        upad_ref[CONV_A_PAD:CONV_A_PAD + row_len, :] = av * jax.nn.sigmoid(ag)
        for sb in range(row_len // CONV_SUB):
            for c in range(ch // LANE):
                cs = slice(c * LANE, (c + 1) * LANE)
                acc = None
                for res in range(SUBLANE):
                    z = None
                    for k in range(taps):
                        if (k - half) % SUBLANE != res:
                            continue
                        start = CONV_A_PAD + sb * CONV_SUB + (k - half) - res
                        term = w_ref[k:k + 1, cs] * upad_ref[start:start + CONV_SUB + SUBLANE, cs]
                        z = term if z is None else z + term
                    if z is None:
                        continue
                    z = z[res:res + CONV_SUB, :]
                    acc = z if acc is None else acc + z
                y_ref[:, cs] = acc + b_ref[:, cs]
            y = y_ref[...]
            mu = jnp.mean(y, axis=-1, keepdims=True)
            dlt = y - mu
            var = jnp.mean(dlt * dlt, axis=-1, keepdims=True)
            z = dlt * lax.rsqrt(var + NORM_EPS) * lnw_ref[...] + lnb_ref[...]
            o_ref[pl.ds(r0 + sb * CONV_SUB, CONV_SUB), :] = (z * jax.nn.sigmoid(z)).astype(BF16)
        return carry

    lax.fori_loop(0, n_rows, row_body, 0)


def _conv_a(p, conv_w, conv_b, ln_w, ln_b, *, n_seq, seq_len, row_len, val_blk, gate_blk, name):
    taps, ch = conv_w.shape
    n = n_seq * seq_len
    kern = functools.partial(_conv_a_kernel, row_len=row_len, n_rows=seq_len // row_len, taps=taps)
    vec = lambda a: a.reshape(1, ch)
    return pl.pallas_call(
        kern,
        grid=(n_seq,),
        in_specs=[pl.BlockSpec((seq_len, ch), lambda b: (b, val_blk)),
                  pl.BlockSpec((seq_len, ch), lambda b: (b, gate_blk)),
                  pl.BlockSpec((taps, ch), lambda b: (0, 0)),
                  pl.BlockSpec((1, ch), lambda b: (0, 0)),
                  pl.BlockSpec((1, ch), lambda b: (0, 0)),
                  pl.BlockSpec((1, ch), lambda b: (0, 0))],
        out_specs=pl.BlockSpec((seq_len, ch), lambda b: (b, 0)),
        out_shape=jax.ShapeDtypeStruct((n, ch), BF16),
        scratch_shapes=[pltpu.VMEM((row_len + 2 * CONV_A_PAD, ch), F32),
                        pltpu.VMEM((CONV_SUB, ch), F32)],
        compiler_params=_params(1),
        name=name,
    )(p, p, conv_w, vec(conv_b), vec(ln_w), vec(ln_b))


CONV_C_PAD = 64
CONV_C_BLK = 64


def _conv_c_kernel(sin_ref, sb_ref, sc_ref, w_ref, o_ref, upad_ref, *, seq_len, shift):
    ch = o_ref.shape[-1]
    zeros = jnp.zeros((CONV_C_PAD, ch), F32)
    upad_ref[0:CONV_C_PAD, :] = zeros
    upad_ref[CONV_C_PAD + seq_len:CONV_C_PAD + seq_len + CONV_C_PAD, :] = zeros
    for i in range(seq_len // CONV_C_BLK):
        t = slice(i * CONV_C_BLK, (i + 1) * CONV_C_BLK)
        upad_ref[CONV_C_PAD + i * CONV_C_BLK:CONV_C_PAD + (i + 1) * CONV_C_BLK, :] = (
            sc_ref[t, :].astype(F32) * sin_ref[t, :].astype(F32))
    for i in range(seq_len // CONV_C_BLK):
        t0 = CONV_C_PAD + i * CONV_C_BLK
        y = (w_ref[0:1, :] * upad_ref[t0 - shift:t0 - shift + CONV_C_BLK, :]
             + w_ref[1:2, :] * upad_ref[t0:t0 + CONV_C_BLK, :]
             + w_ref[2:3, :] * upad_ref[t0 + shift:t0 + shift + CONV_C_BLK, :])
        t = slice(i * CONV_C_BLK, (i + 1) * CONV_C_BLK)
        o_ref[t, :] = (sb_ref[t, :].astype(F32) * y).astype(BF16)


def _conv_c(p, conv_w, *, n_seq, seq_len, shift, in_blk, b_blk, c_blk, name):
    taps, ch = conv_w.shape
    assert taps == 3 and shift <= CONV_C_PAD
    n = n_seq * seq_len
    kern = functools.partial(_conv_c_kernel, seq_len=seq_len, shift=shift)
    return pl.pallas_call(
        kern,
        grid=(n_seq,),
        in_specs=[pl.BlockSpec((seq_len, ch), lambda b: (b, in_blk)),
                  pl.BlockSpec((seq_len, ch), lambda b: (b, b_blk)),
                  pl.BlockSpec((seq_len, ch), lambda b: (b, c_blk)),
                  pl.BlockSpec((taps, ch), lambda b: (0, 0))],
        out_specs=pl.BlockSpec((seq_len, ch), lambda b: (b, 0)),
        out_shape=jax.ShapeDtypeStruct((n, ch), BF16),
        scratch_shapes=[pltpu.VMEM((seq_len + 2 * CONV_C_PAD, ch), F32)],
        compiler_params=_params(1),
        name=name,
    )(p, p, p, conv_w)


def _log_sigmoid(z):
    return jnp.minimum(z, 0.0) - jnp.log1p(jnp.exp(-jnp.abs(z)))


def _gate_stats_kernel(g_ref, xs_ref, rows_ref, *, heads):
    chunk = g_ref.shape[0]
    li = g_ref[:, 0:LANE]
    lf = _log_sigmoid(g_ref[:, LANE:2 * LANE])
    lane = lax.broadcasted_iota(jnp.int32, (chunk, LANE), 1)
    row = lax.broadcasted_iota(jnp.int32, (chunk, LANE), 0)
    fwd = lane < heads

    jj = lax.broadcasted_iota(jnp.int32, (chunk, chunk), 0)
    ss = lax.broadcasted_iota(jnp.int32, (chunk, chunk), 1)
    tri_p = jnp.where(ss <= jj, 1.0, 0.0).astype(BF16)
    tri_s = jnp.where(ss >= jj, 1.0, 0.0).astype(BF16)
    b_p = jnp.zeros((chunk, LANE), F32)
    b_s = jnp.zeros((chunk, LANE), F32)
    for piece in _split3(lf):
        b_p = b_p + jnp.dot(tri_p, piece, preferred_element_type=F32)
        b_s = b_s + jnp.dot(tri_s, piece, preferred_element_type=F32)
    b = jnp.where(fwd, b_p, b_s)
    a = li - b

    run_max = a
    sh = 1
    while sh < chunk:
        up = pltpu.roll(run_max, sh, axis=0)
        dn = pltpu.roll(run_max, chunk - sh, axis=0)
        cand = jnp.where(fwd, jnp.where(row >= sh, up, MASKED_LOG),
                         jnp.where(row < chunk - sh, dn, MASKED_LOG))
        run_max = jnp.maximum(run_max, cand)
        sh *= 2

    a_rows = a.T[0:COMBOS, :]
    b_rows = b.T[0:COMBOS, :]
    rows_ref[0, 0:COMBOS, :] = a_rows
    rows_ref[0, COMBOS:2 * COMBOS, :] = jnp.broadcast_to(
        jnp.max(a_rows, axis=1, keepdims=True), (COMBOS, chunk))
    rows_ref[0, 2 * COMBOS:3 * COMBOS, :] = jnp.broadcast_to(
        jnp.min(b_rows, axis=1, keepdims=True), (COMBOS, chunk))

    packed = jnp.zeros((chunk, LANE), F32)
    pieces = _split3(run_max) + _split3(b)
    for i, piece in enumerate(pieces):
        val = jnp.where(lane < COMBOS, piece.astype(F32), 0.0)
        packed = packed + (pltpu.roll(val, COMBOS * i, axis=1) if i else val)
    xs_ref[...] = packed.astype(BF16)


def _gate_stats(gates, *, chunk, heads, name):
    n = gates.shape[0]
    assert 2 * heads <= COMBOS and 2 * N_SPLIT * COMBOS <= LANE
    return pl.pallas_call(
        functools.partial(_gate_stats_kernel, heads=heads),
        grid=(n // chunk,),
        in_specs=[pl.BlockSpec((chunk, 2 * LANE), lambda i: (i, 0))],
        out_specs=[pl.BlockSpec((chunk, LANE), lambda i: (i, 0)),
                   pl.BlockSpec((1, 3 * COMBOS, chunk), lambda i: (i, 0, 0))],
        out_shape=[jax.ShapeDtypeStruct((n, LANE), BF16),
                   jax.ShapeDtypeStruct((n // chunk, 3 * COMBOS, chunk), F32)],
        compiler_params=_params(1),
        name=name,
    )(gates)


def _twice(r):
    return jnp.concatenate([r, r], axis=1)


def _mlstm_chunk(q, k, v, xs, a_row, a_max, b_last, sel, c_ext, m_st, *, reverse, need_h):
    chunk, dh = k.shape
    v_ext = jnp.concatenate([v, jnp.ones_like(v)], axis=1)
    k_t = k.T

    h = None
    if need_h:
        stats = jnp.dot(xs, sel, preferred_element_type=F32)
        a_run, b_col = stats[:, :dh], stats[:, dh:]
        jj = lax.broadcasted_iota(jnp.int32, (chunk, chunk), 0)
        ss = lax.broadcasted_iota(jnp.int32, (chunk, chunk), 1)
        seen = (ss >= jj) if reverse else (ss <= jj)
        a_run_wide = jnp.concatenate([a_run] * (chunk // dh), axis=1)
        w0 = jnp.exp(jnp.where(seen, a_row - a_run_wide, MASKED_LOG))
        scores = jnp.dot(q, k_t, preferred_element_type=F32) * w0
        intra = jnp.dot(scores.astype(BF16), v_ext, preferred_element_type=F32)
        inter = jnp.dot(q, c_ext.astype(BF16), preferred_element_type=F32)
        gap = jnp.exp(-jnp.abs(a_run - m_st))
        intra_first = a_run >= m_st
        r_intra = jnp.where(intra_first, 1.0, gap)
        r_inter = jnp.where(intra_first, gap, 1.0)
        num = r_intra * intra[:, :dh] + r_inter * inter[:, :dh]
        den = r_intra * intra[:, dh:] + r_inter * inter[:, dh:]
        floor = jnp.exp(-(b_col + jnp.maximum(a_run, m_st)))
        h = num / jnp.maximum(jnp.abs(den), floor)

    g_last = jnp.maximum(a_max, m_st)
    new_scale = jnp.exp(a_max - g_last)
    decay = jnp.exp(m_st - g_last)
    a_max_row = jnp.concatenate([a_max] * (chunk // dh), axis=1)
    kw_t = (k_t.astype(F32) * jnp.exp(a_row - a_max_row)).astype(BF16)
    update = jnp.dot(kw_t, v_ext, preferred_element_type=F32)
    c_new = _twice(decay) * c_ext + _twice(new_scale) * update
    return h, c_new, b_last + g_last


def _mlstm_kernel(*refs, chunk, n_lat, n_ctx, ctx_out, heads):
    if ctx_out:
        (ql, kl, vl, ol, qc, kc, vc, oc, xsl, rwl, xsc, rwc, nw,
         out_l, out_c, hf_l, hb_l, hf_c, hb_c, c_ref, m_ref) = refs
    else:
        (ql, kl, vl, ol, kc, vc, xsl, rwl, xsc, rwc, nw,
         out_l, hf_l, hb_l, c_ref, m_ref) = refs
        qc = oc = out_c = hf_c = hb_c = None
    dh = nw.shape[-1]
    head = pl.program_id(1)

    c_ref[...] = jnp.zeros_like(c_ref)
    m_ref[...] = jnp.zeros_like(m_ref)

    kk = lax.broadcasted_iota(jnp.int32, (LANE, 2 * dh), 0)
    nn = lax.broadcasted_iota(jnp.int32, (LANE, 2 * dh), 1)
    piece = jnp.right_shift(kk, COMBOS.bit_length() - 1)
    first_piece = jnp.where(nn < dh, 0, N_SPLIT)
    target = (piece >= first_piece) & (piece < first_piece + N_SPLIT)
    pair_lane = jnp.bitwise_and(kk, COMBOS - 1)
    sels = [jnp.where(target & (pair_lane == direction * heads + head), 1.0, 0.0).astype(BF16)
            for direction in (0, 1)]

    def scan(q_ref, k_ref, v_ref, xs_ref, rw_ref, hf_ref, hb_ref, n_chunks, need_h):
        def step(i, direction):
            rows = pl.ds(pl.multiple_of(i * chunk, chunk), chunk)
            pair = direction * heads + head
            a_row = rw_ref[i, pl.ds(pair, 1), :]
            a_max = rw_ref[i, pl.ds(COMBOS + pair, 1), :][:, 0:dh]
            b_last = rw_ref[i, pl.ds(2 * COMBOS + pair, 1), :][:, 0:dh]
            h, c_new, m_new = _mlstm_chunk(
                q_ref[rows, :] if need_h else None, k_ref[rows, :], v_ref[rows, :], xs_ref[rows, :],
                a_row, a_max, b_last, sels[direction], c_ref[direction], m_ref[direction],
                reverse=direction == 1, need_h=need_h)
            c_ref[direction] = c_new
            m_ref[direction] = m_new
            if need_h:
                (hf_ref, hb_ref)[direction][rows, :] = h

        def body(i, carry):
            step(i, 0)
            step(n_chunks - 1 - i, 1)
            return carry

        lax.fori_loop(0, n_chunks, body, 0, unroll=4 if n_chunks % 4 == 0 else 1)

    def finish(hf_ref, hb_ref, o_ref, out_ref, n_chunks):
        mean_w = jnp.full((dh, dh), 1.0 / dh, BF16)

        def body(i, carry):
            rows = pl.ds(pl.multiple_of(i * chunk, chunk), chunk)
            h = hf_ref[rows, :] + hb_ref[rows, :]
            mu = jnp.dot(h.astype(BF16), mean_w, preferred_element_type=F32)
            dlt = h - mu
            var = jnp.dot((dlt * dlt).astype(BF16), mean_w, preferred_element_type=F32)
            y = dlt * lax.rsqrt(var + NORM_EPS) * nw[...]
            out_ref[rows, :] = (y * jax.nn.sigmoid(o_ref[rows, :].astype(F32))).astype(BF16)
            return carry

        lax.fori_loop(0, n_chunks, body, 0, unroll=2 if n_chunks % 2 == 0 else 1)

    scan(qc, kc, vc, xsc, rwc, hf_c, hb_c, n_ctx, ctx_out)
    scan(ql, kl, vl, xsl, rwl, hf_l, hb_l, n_lat, True)
    finish(hf_l, hb_l, ol, out_l, n_lat)
    if ctx_out:
        finish(hf_c, hb_c, oc, out_c, n_ctx)


def _mlstm(p_lat, p_ctx, stats_lat, stats_ctx, norm_w, *, n_seq, t_lat, t_ctx, heads, dh, chunk,
           lat_blk, ctx_blk, ctx_out, name):
    assert chunk % dh == 0
    n_lat, n_ctx = t_lat // chunk, t_ctx // chunk

    def tok_spec(t, blk):
        return pl.BlockSpec((t, dh), lambda b, h: (b, blk + h))

    def stat_specs(t, n_chunks):
        return [pl.BlockSpec((t, LANE), lambda b, h: (b, 0)),
                pl.BlockSpec((n_chunks, 3 * COMBOS, chunk), lambda b, h: (b, 0, 0))]

    lat_in = [tok_spec(t_lat, lat_blk[name_]) for name_ in "qkvo"]
    ctx_names = "qkvo" if ctx_out else "kv"
    ctx_in = [tok_spec(t_ctx, ctx_blk[name_]) for name_ in ctx_names]
    in_specs = (lat_in + ctx_in + stat_specs(t_lat, n_lat) + stat_specs(t_ctx, n_ctx)
                + [pl.BlockSpec((1, dh), lambda b, h: (0, h))])
    args = ([p_lat] * 4 + [p_ctx] * len(ctx_names) + list(stats_lat) + list(stats_ctx)
            + [norm_w.reshape(1, heads * dh)])

    out_specs = [pl.BlockSpec((t_lat, dh), lambda b, h: (b, h))]
    out_shape = [jax.ShapeDtypeStruct((n_seq * t_lat, heads * dh), BF16)]
    scratch = [pltpu.VMEM((t_lat, dh), F32), pltpu.VMEM((t_lat, dh), F32)]
    if ctx_out:
        out_specs.append(pl.BlockSpec((t_ctx, dh), lambda b, h: (b, h)))
        out_shape.append(jax.ShapeDtypeStruct((n_seq * t_ctx, heads * dh), BF16))
        scratch += [pltpu.VMEM((t_ctx, dh), F32), pltpu.VMEM((t_ctx, dh), F32)]
    scratch += [pltpu.VMEM((2, dh, 2 * dh), F32), pltpu.VMEM((2, 1, dh), F32)]

    kern = functools.partial(_mlstm_kernel, chunk=chunk, n_lat=n_lat, n_ctx=n_ctx, ctx_out=ctx_out,
                             heads=heads)
    return pl.pallas_call(
        kern,
        grid=(n_seq, heads),
        in_specs=in_specs,
        out_specs=out_specs,
        out_shape=out_shape,
        scratch_shapes=scratch,
        compiler_params=_params(2),
        name=name,
    )(*args)


OUT_PROJ_SUB_ROWS = 256


def _out_proj_kernel(a_ref, m_ref, c_ref, w_ref, x_ref, gpost_ref, g1_ref, gpre_ref, sh2_ref, sc2_ref,
                     x1_ref, h2_ref, *, sub_rows):
    da, dm = a_ref.shape[-1], m_ref.shape[-1]
    tm = x_ref.shape[0]
    gain1 = g1_ref[0] * gpost_ref[...]
    gain2 = gpre_ref[...] * (1.0 + sc2_ref[0])
    for r in range(tm // sub_rows):
        rows = slice(r * sub_rows, (r + 1) * sub_rows)
        y = (jnp.dot(a_ref[rows, :], w_ref[0:da, :], preferred_element_type=F32)
             + jnp.dot(m_ref[rows, :], w_ref[da:da + dm, :], preferred_element_type=F32)
             + jnp.dot(c_ref[rows, :], w_ref[da + dm:, :], preferred_element_type=F32))
        x1 = x_ref[rows, :] + _rms(y) * gain1
        x1_ref[rows, :] = x1
        h2_ref[rows, :] = (_rms(x1) * gain2 + sh2_ref[0]).astype(BF16)


def _out_proj(a, m, cc, w_out, x2d, mod3, mod_row, g_post, g_pre_ffn, *, layer, tm, name):
    n, d = x2d.shape
    da, dm, dc = a.shape[1], m.shape[1], cc.shape[1]
    row = lambda i: (i, 0)
    fixed = lambda i: (0, 0)
    mod_spec = lambda k: pl.BlockSpec((1, 1, d), lambda i: (mod_row(i), 0, k))
    return pl.pallas_call(
        functools.partial(_out_proj_kernel, sub_rows=min(tm, OUT_PROJ_SUB_ROWS)),
        grid=(n // tm,),
        in_specs=[pl.BlockSpec((tm, da), row), pl.BlockSpec((tm, dm), row), pl.BlockSpec((tm, dc), row),
                  pl.BlockSpec((None, da + dm + dc, d), lambda i: (layer, 0, 0),
                               pipeline_mode=pl.Buffered(1)),
                  pl.BlockSpec((tm, d), row),
                  pl.BlockSpec((1, d), fixed),
                  mod_spec(2),
                  pl.BlockSpec((1, d), fixed),
                  mod_spec(3), mod_spec(4)],
        out_specs=[pl.BlockSpec((tm, d), row), pl.BlockSpec((tm, d), row)],
        out_shape=[jax.ShapeDtypeStruct((n, d), F32), jax.ShapeDtypeStruct((n, d), BF16)],
        compiler_params=_params(1),
        name=name,
    )(a, m, cc, w_out, x2d, g_post.reshape(1, d), mod3, g_pre_ffn.reshape(1, d), mod3, mod3)


FFN_TF = 1024


def _ffn_kernel(h_ref, w1_ref, w2_ref, x_ref, gpost_ref, g2_ref, o_ref, acc_ref):
    j = pl.program_id(1)

    @pl.when(j == 0)
    def _():
        acc_ref[...] = jnp.zeros_like(acc_ref)

    u = jnp.maximum(jnp.dot(h_ref[...], w1_ref[...], preferred_element_type=F32), 0.0)
    acc_ref[...] += jnp.dot((u * u).astype(BF16), w2_ref[...], preferred_element_type=F32)

    @pl.when(j == pl.num_programs(1) - 1)
    def _():
        o_ref[...] = x_ref[...] + g2_ref[0] * (_rms(acc_ref[...]) * gpost_ref[...])


def _ffn(h2, w1, w2, x1, mod3, mod_row, g_post, *, layer, tm, tf, name):
    n, d = x1.shape
    dff = w1.shape[2]
    return pl.pallas_call(
        _ffn_kernel,
        grid=(n // tm, dff // tf),
        in_specs=[pl.BlockSpec((tm, d), lambda i, j: (i, 0)),
                  pl.BlockSpec((None, d, tf), lambda i, j: (layer, 0, j)),
                  pl.BlockSpec((None, tf, d), lambda i, j: (layer, j, 0)),
                  pl.BlockSpec((tm, d), lambda i, j: (i, 0)),
                  pl.BlockSpec((1, d), lambda i, j: (0, 0)),
                  pl.BlockSpec((1, 1, d), lambda i, j: (mod_row(i), 0, 5))],
        out_specs=pl.BlockSpec((tm, d), lambda i, j: (i, 0)),
        out_shape=jax.ShapeDtypeStruct((n, d), F32),
        scratch_shapes=[pltpu.VMEM((tm, d), F32)],
        compiler_params=_params(2),
        name=name,
    )(h2, w1, w2, x1, g_post.reshape(1, d), mod3)


def _tile(n, target):
    t = min(n, target)
    while n % t or (t % 8 and t != n):
        t -= 1
    return t


def kernel(x, c, ctx, c_ctx, w_ada, b_ada, g_pre_mix, g_post_mix, g_pre_ffn, g_post_ffn, w_in, b_gates,
           conv_a_w, conv_a_b, ln_a_w, ln_a_b, mlstm_norm_w, conv_c_w, w_out, w_ff1, w_ff2):
    bsz, t_lat, d = x.shape
    t_ctx = ctx.shape[1]
    depth = w_ada.shape[0]
    d_conv = conv_a_w.shape[-1]
    d_short = conv_c_w.shape[-1]
    d_mlstm = mlstm_norm_w.shape[-1]
    heads = b_gates.shape[-1] // 4
    dh = d_mlstm // heads
    n_gates = 4 * heads
    assert dh == LANE and d_conv == d_short and d_conv % LANE == 0
    assert bsz + 1 <= MOD_ROWS and t_lat % GRID_W == 0
    assert w_in.shape[-1] == 2 * d_conv + 4 * d_mlstm + n_gates + 3 * d_short

    tn = d_conv
    chunk = min(256, t_ctx)
    assert d_mlstm % tn == 0 and t_ctx % chunk == 0 and t_lat % chunk == 0
    gate_lo = 2 * d_conv + 4 * d_mlstm
    mt = d_mlstm // tn
    tile = {"a_val": 0, "a_gate": 1, "q": 2, "k": 2 + mt, "v": 2 + 2 * mt, "o": 2 + 3 * mt,
            "s_in": 2 + 4 * mt, "s_b": 3 + 4 * mt, "s_c": 4 + 4 * mt}
    lane_blk = {k_: v_ * (tn // dh) for k_, v_ in tile.items()}
    q_scale = float(dh) ** -0.5

    xl = x.reshape(bsz * t_lat, d)
    xc = ctx.reshape(bsz * t_ctx, d)

    cond = jnp.concatenate([c, c_ctx[None, :], jnp.zeros((MOD_ROWS - bsz - 1, d), F32)], axis=0)
    mod3 = _modulation(cond, w_ada, b_ada).reshape(depth * MOD_ROWS, 1, 6 * d)

    tm_in = _tile(t_lat, 256)
    tm_in_ctx = _tile(bsz * t_ctx, 256)
    tm_lat = _tile(t_lat, 512)
    tm_ctx = _tile(bsz * t_ctx, 512)
    tm_out = _tile(t_lat, 512)
    tm_out_ctx = _tile(bsz * t_ctx, 512)

    def gate_cols(g):
        i_f, f_f, i_b, f_b = jnp.split(g, 4, axis=-1)
        pad = jnp.zeros(g.shape[:-1] + (LANE - 2 * heads,), g.dtype)
        return jnp.concatenate([i_f, i_b, pad, f_f, f_b, pad], axis=-1)

    w_main = jnp.concatenate([w_in[:, :, :gate_lo], w_in[:, :, gate_lo + n_gates:]], axis=2).astype(BF16)
    n_main = w_main.shape[2]
    kv_cols = (tile["o"] - tile["k"]) * tn
    assert (tile["k"] * tn) % kv_cols == 0
    w_out_b = w_out.astype(BF16)
    w1_b = w_ff1.astype(BF16)
    w2_b = w_ff2.astype(BF16)

    for layer in range(depth):
        last = layer == depth - 1
        base = layer * MOD_ROWS
        wg = gate_cols(w_in[layer][:, gate_lo:gate_lo + n_gates]).astype(BF16)
        bg = gate_cols(b_gates[layer]).reshape(1, 2 * LANE)

        def lat_row(tm):
            return lambda i: base + (i * tm) // t_lat
        ctx_row = lambda i: base + bsz

        proj = functools.partial(_in_proj, mod3=mod3, g_pre=g_pre_mix[layer], w=w_main, wg=wg, bg=bg,
                                 layer=layer, tn=tn, q_scale=q_scale)
        p_lat, g_lat = proj(xl, mod_row=lat_row(tm_in), nw=n_main, col_blk=0, tm=tm_in,
                            q_tile_lo=tile["q"], q_tile_hi=tile["k"], name=f"in_proj_lat_{layer}")
        if last:
            p_ctx, g_ctx = proj(xc, mod_row=ctx_row, nw=kv_cols, col_blk=tile["k"] * tn // kv_cols,
                                tm=tm_in_ctx, q_tile_lo=0, q_tile_hi=0, name=f"in_proj_ctx_{layer}")
            ctx_blk = {"k": 0, "v": lane_blk["v"] - lane_blk["k"]}
        else:
            p_ctx, g_ctx = proj(xc, mod_row=ctx_row, nw=n_main, col_blk=0, tm=tm_in_ctx,
                                q_tile_lo=tile["q"], q_tile_hi=tile["k"], name=f"in_proj_ctx_{layer}")
            ctx_blk = lane_blk

        stats_lat = _gate_stats(g_lat, chunk=chunk, heads=heads, name=f"gate_stats_lat_{layer}")
        stats_ctx = _gate_stats(g_ctx, chunk=chunk, heads=heads, name=f"gate_stats_ctx_{layer}")
        mres = _mlstm(p_lat, p_ctx, stats_lat, stats_ctx, mlstm_norm_w[layer],
                      n_seq=bsz, t_lat=t_lat, t_ctx=t_ctx, heads=heads, dh=dh, chunk=chunk,
                      lat_blk=lane_blk, ctx_blk=ctx_blk, ctx_out=not last, name=f"mlstm_{layer}")

        conv_a = functools.partial(_conv_a, conv_w=conv_a_w[layer], conv_b=conv_a_b[layer],
                                   ln_w=ln_a_w[layer], ln_b=ln_a_b[layer], n_seq=bsz,
                                   val_blk=tile["a_val"], gate_blk=tile["a_gate"])
        conv_c = functools.partial(_conv_c, conv_w=conv_c_w[layer], n_seq=bsz, in_blk=tile["s_in"],
                                   b_blk=tile["s_b"], c_blk=tile["s_c"])
        a_lat = conv_a(p_lat, seq_len=t_lat, row_len=GRID_W, name=f"conv_a_lat_{layer}")
        c_lat = conv_c(p_lat, seq_len=t_lat, shift=GRID_W, name=f"conv_c_lat_{layer}")
        x1, h2 = _out_proj(a_lat, mres[0], c_lat, w_out_b, xl, mod3, lat_row(tm_out), g_post_mix[layer],
                           g_pre_ffn[layer], layer=layer, tm=tm_out, name=f"out_proj_lat_{layer}")
        xl = _ffn(h2, w1_b, w2_b, x1, mod3, lat_row(tm_lat), g_post_ffn[layer], layer=layer, tm=tm_lat,
                  tf=FFN_TF,
                  name=f"ffn_lat_{layer}")

        if not last:
            a_ctx = conv_a(p_ctx, seq_len=t_ctx, row_len=t_ctx, name=f"conv_a_ctx_{layer}")
            c_ctx_mix = conv_c(p_ctx, seq_len=t_ctx, shift=1, name=f"conv_c_ctx_{layer}")
            x1c, h2c = _out_proj(a_ctx, mres[1], c_ctx_mix, w_out_b, xc, mod3, ctx_row, g_post_mix[layer],
                                 g_pre_ffn[layer], layer=layer, tm=tm_out_ctx, name=f"out_proj_ctx_{layer}")
            xc = _ffn(h2c, w1_b, w2_b, x1c, mod3, ctx_row, g_post_ffn[layer], layer=layer, tm=tm_ctx,
                      tf=FFN_TF,
                      name=f"ffn_ctx_{layer}")

    return xl.reshape(bsz, t_lat, d)
```

```python
import functools

import jax
import jax.numpy as jnp
from jax import lax
from jax.experimental import pallas as pl
from jax.experimental.pallas import tpu as pltpu

GRID_W = 64
NORM_EPS = 1e-6
LANE = 128
SUBLANE = 8
VMEM_LIMIT_BYTES = 56 * 1024 * 1024
MOD_ROWS = 16
MASKED_LOG = -1e30
COMBOS = 16
N_SPLIT = 3

F32 = jnp.float32
BF16 = jnp.bfloat16


def _params(n_grid_axes):
    return pltpu.CompilerParams(
        dimension_semantics=("arbitrary",) * n_grid_axes,
        vmem_limit_bytes=VMEM_LIMIT_BYTES)


def _rms(y):
    return y * lax.rsqrt(jnp.mean(y * y, axis=-1, keepdims=True) + NORM_EPS)


def _split3(x):
    hi = x.astype(BF16)
    r1 = x - hi.astype(F32)
    mid = r1.astype(BF16)
    lo = (r1 - mid.astype(F32)).astype(BF16)
    return hi, mid, lo


def _mod_kernel(s_ref, w_ref, b_ref, o_ref):
    s = s_ref[...]
    a = (s * jax.nn.sigmoid(s)).astype(BF16)
    o_ref[0] = jnp.dot(a, w_ref[0].astype(BF16), preferred_element_type=F32) + b_ref[0]


def _modulation(s, w_ada, b_ada, tn=1024):
    depth, d, n = w_ada.shape
    return pl.pallas_call(
        _mod_kernel,
        grid=(depth, n // tn),
        in_specs=[pl.BlockSpec((MOD_ROWS, d), lambda l, j: (0, 0)),
                  pl.BlockSpec((1, d, tn), lambda l, j: (l, 0, j)),
                  pl.BlockSpec((1, 1, tn), lambda l, j: (l, 0, j))],
        out_specs=pl.BlockSpec((1, MOD_ROWS, tn), lambda l, j: (l, 0, j)),
        out_shape=jax.ShapeDtypeStruct((depth, MOD_ROWS, n), F32),
        compiler_params=_params(2),
        name="adaln_modulation",
    )(s, w_ada, b_ada.reshape(depth, 1, n))


def _in_proj_kernel(*refs, n_parts, chunk, heads, tn, q_tile_lo, q_tile_hi, q_scale):
    x_ref, g_ref, sh_ref, sc_ref, wg_ref, bg_ref = refs[:6]
    w_refs = refs[6:6 + n_parts]
    p_ref, xs_ref, rows_ref = refs[6 + n_parts:]
    gain = g_ref[...] * (1.0 + sc_ref[0])
    n_sub = x_ref.shape[0] // chunk
    tiles = [(w_ref, j) for w_ref in w_refs for j in range(w_ref.shape[1] // tn)]
    last = len(tiles) - 1

    def normed(r):
        return (_rms(x_ref[r * chunk:(r + 1) * chunk, :]) * gain + sh_ref[0]).astype(BF16)

    hb = normed(0)
    for r in range(n_sub):
        rows = slice(r * chunk, (r + 1) * chunk)
        gates = jnp.dot(hb, wg_ref[...], preferred_element_type=F32) + bg_ref[...]
        hb_next = None
        for t, (w_ref, j) in enumerate(tiles):
            acc = jnp.dot(hb, w_ref[:, j * tn:(j + 1) * tn], preferred_element_type=F32)
            if q_tile_lo <= t < q_tile_hi:
                acc = acc * q_scale
            p_ref[rows, t * tn:(t + 1) * tn] = acc.astype(BF16)
            if t == 0:
                pieces = _forget_pieces(gates)
            if t == min(2, last):
                b = _forget_cumsum(pieces, heads)
            if t == min(4, last):
                xs_ref[rows, :], rows_ref[r] = _gate_stats(gates, b, heads)
            if t == min(6, last) and r + 1 < n_sub:
                hb_next = normed(r + 1)
        hb = hb_next


def _in_proj(x2d, mod3, mod_row, g_pre, w_parts, wg, bg, *, layer, chunk, heads, tm, tn, q_tile_lo,
             q_tile_hi, q_scale, name):
    n, d = x2d.shape
    ng = wg.shape[1]
    nw = sum(width for _, width, _ in w_parts)
    assert tm % chunk == 0 and all(width % tn == 0 for _, width, _ in w_parts)
    kern = functools.partial(_in_proj_kernel, n_parts=len(w_parts), chunk=chunk, heads=heads, tn=tn,
                             q_tile_lo=q_tile_lo, q_tile_hi=q_tile_hi, q_scale=q_scale)
    fixed = lambda i: (0, 0)

    def w_spec(width, blk):
        return pl.BlockSpec((None, d, width), lambda i: (layer, 0, blk), pipeline_mode=pl.Buffered(1))

    return pl.pallas_call(
        kern,
        grid=(n // tm,),
        in_specs=[pl.BlockSpec((tm, d), lambda i: (i, 0)),
                  pl.BlockSpec((1, d), fixed),
                  pl.BlockSpec((1, 1, d), lambda i: (mod_row(i), 0, 0)),
                  pl.BlockSpec((1, 1, d), lambda i: (mod_row(i), 0, 1)),
                  pl.BlockSpec((d, ng), fixed),
                  pl.BlockSpec((1, ng), fixed)]
                 + [w_spec(width, blk) for _, width, blk in w_parts],
        out_specs=[pl.BlockSpec((tm, nw), lambda i: (i, 0)),
                   pl.BlockSpec((tm, LANE), lambda i: (i, 0)),
                   pl.BlockSpec((tm // chunk, 3 * COMBOS, chunk), lambda i: (i, 0, 0))],
        out_shape=[jax.ShapeDtypeStruct((n, nw), BF16),
                   jax.ShapeDtypeStruct((n, LANE), BF16),
                   jax.ShapeDtypeStruct((n // chunk, 3 * COMBOS, chunk), F32)],
        compiler_params=_params(1),
        name=name,
    )(x2d, g_pre.reshape(1, d), mod3, mod3, wg, bg, *[arr for arr, _, _ in w_parts])


CONV_A_PAD = 16
CONV_SUB = 64


def _conv_a_kernel(av_ref, ag_ref, w_ref, b_ref, lnw_ref, lnb_ref, o_ref, upad_ref, y_ref,
                   *, row_len, n_rows, taps):
    ch = o_ref.shape[-1]
    half = taps // 2
    zeros = jnp.zeros((CONV_A_PAD, ch), F32)
    upad_ref[0:CONV_A_PAD, :] = zeros
    upad_ref[CONV_A_PAD + row_len:CONV_A_PAD + row_len + CONV_A_PAD, :] = zeros

    def row_body(r, carry):
        r0 = pl.multiple_of(r * row_len, row_len)
        av = av_ref[pl.ds(r0, row_len), :].astype(F32)
        ag = ag_ref[pl.ds(r0, row_len), :].astype(F32)
        upad_ref[CONV_A_PAD:CONV_A_PAD + row_len, :] = av * jax.nn.sigmoid(ag)
        for sb in range(row_len // CONV_SUB):
            for c in range(ch // LANE):
                cs = slice(c * LANE, (c + 1) * LANE)
                acc = None
                for res in range(SUBLANE):
                    z = None
                    for k in range(taps):
                        if (k - half) % SUBLANE != res:
                            continue
                        start = CONV_A_PAD + sb * CONV_SUB + (k - half) - res
                        term = w_ref[k:k + 1, cs] * upad_ref[start:start + CONV_SUB + SUBLANE, cs]
                        z = term if z is None else z + term
                    if z is None:
                        continue
                    z = z[res:res + CONV_SUB, :]
                    acc = z if acc is None else acc + z
                y_ref[:, cs] = acc + b_ref[:, cs]
            y = y_ref[...]
            mu = jnp.mean(y, axis=-1, keepdims=True)
            dlt = y - mu
            var = jnp.mean(dlt * dlt, axis=-1, keepdims=True)
            z = dlt * lax.rsqrt(var + NORM_EPS) * lnw_ref[...] + lnb_ref[...]
            o_ref[pl.ds(r0 + sb * CONV_SUB, CONV_SUB), :] = (z * jax.nn.sigmoid(z)).astype(BF16)
        return carry

    lax.fori_loop(0, n_rows, row_body, 0)


def _conv_a(p, conv_w, conv_b, ln_w, ln_b, *, n_seq, seq_len, row_len, val_blk, gate_blk, name):
    taps, ch = conv_w.shape
    n = n_seq * seq_len
    kern = functools.partial(_conv_a_kernel, row_len=row_len, n_rows=seq_len // row_len, taps=taps)
    vec = lambda a: a.reshape(1, ch)
    return pl.pallas_call(
        kern,
        grid=(n_seq,),
        in_specs=[pl.BlockSpec((seq_len, ch), lambda b: (b, val_blk)),
                  pl.BlockSpec((seq_len, ch), lambda b: (b, gate_blk)),
                  pl.BlockSpec((taps, ch), lambda b: (0, 0)),
                  pl.BlockSpec((1, ch), lambda b: (0, 0)),
                  pl.BlockSpec((1, ch), lambda b: (0, 0)),
                  pl.BlockSpec((1, ch), lambda b: (0, 0))],
        out_specs=pl.BlockSpec((seq_len, ch), lambda b: (b, 0)),
        out_shape=jax.ShapeDtypeStruct((n, ch), BF16),
        scratch_shapes=[pltpu.VMEM((row_len + 2 * CONV_A_PAD, ch), F32),
                        pltpu.VMEM((CONV_SUB, ch), F32)],
        compiler_params=_params(1),
        name=name,
    )(p, p, conv_w, vec(conv_b), vec(ln_w), vec(ln_b))


CONV_C_PAD = 64
CONV_C_BLK = 64


def _conv_c_kernel(sin_ref, sb_ref, sc_ref, w_ref, o_ref, upad_ref, *, seq_len, shift):
    ch = o_ref.shape[-1]
    zeros = jnp.zeros((CONV_C_PAD, ch), F32)
    upad_ref[0:CONV_C_PAD, :] = zeros
    upad_ref[CONV_C_PAD + seq_len:CONV_C_PAD + seq_len + CONV_C_PAD, :] = zeros
    for i in range(seq_len // CONV_C_BLK):
        t = slice(i * CONV_C_BLK, (i + 1) * CONV_C_BLK)
        upad_ref[CONV_C_PAD + i * CONV_C_BLK:CONV_C_PAD + (i + 1) * CONV_C_BLK, :] = (
            sc_ref[t, :].astype(F32) * sin_ref[t, :].astype(F32))
    for i in range(seq_len // CONV_C_BLK):
        t0 = CONV_C_PAD + i * CONV_C_BLK
        y = (w_ref[0:1, :] * upad_ref[t0 - shift:t0 - shift + CONV_C_BLK, :]
             + w_ref[1:2, :] * upad_ref[t0:t0 + CONV_C_BLK, :]
             + w_ref[2:3, :] * upad_ref[t0 + shift:t0 + shift + CONV_C_BLK, :])
        t = slice(i * CONV_C_BLK, (i + 1) * CONV_C_BLK)
        o_ref[t, :] = (sb_ref[t, :].astype(F32) * y).astype(BF16)


def _conv_c(p, conv_w, *, n_seq, seq_len, shift, in_blk, b_blk, c_blk, name):
    taps, ch = conv_w.shape
    assert taps == 3 and shift <= CONV_C_PAD
    n = n_seq * seq_len
    kern = functools.partial(_conv_c_kernel, seq_len=seq_len, shift=shift)
    return pl.pallas_call(
        kern,
        grid=(n_seq,),
        in_specs=[pl.BlockSpec((seq_len, ch), lambda b: (b, in_blk)),
                  pl.BlockSpec((seq_len, ch), lambda b: (b, b_blk)),
                  pl.BlockSpec((seq_len, ch), lambda b: (b, c_blk)),
                  pl.BlockSpec((taps, ch), lambda b: (0, 0))],
        out_specs=pl.BlockSpec((seq_len, ch), lambda b: (b, 0)),
        out_shape=jax.ShapeDtypeStruct((n, ch), BF16),
        scratch_shapes=[pltpu.VMEM((seq_len + 2 * CONV_C_PAD, ch), F32)],
        compiler_params=_params(1),
        name=name,
    )(p, p, p, conv_w)


def _log_sigmoid(z):
    return jnp.minimum(z, 0.0) - jnp.log1p(jnp.exp(-jnp.abs(z)))


def _forget_pieces(gates):
    return _split3(_log_sigmoid(gates[:, LANE:2 * LANE]))


def _forget_cumsum(pieces, heads):
    chunk = pieces[0].shape[0]
    jj = lax.broadcasted_iota(jnp.int32, (chunk, chunk), 0)
    ss = lax.broadcasted_iota(jnp.int32, (chunk, chunk), 1)
    tri_p = jnp.where(ss <= jj, 1.0, 0.0).astype(BF16)
    tri_s = jnp.where(ss >= jj, 1.0, 0.0).astype(BF16)
    b_p = jnp.zeros((chunk, LANE), F32)
    b_s = jnp.zeros((chunk, LANE), F32)
    for piece in pieces:
        b_p = b_p + jnp.dot(tri_p, piece, preferred_element_type=F32)
        b_s = b_s + jnp.dot(tri_s, piece, preferred_element_type=F32)
    lane = lax.broadcasted_iota(jnp.int32, (chunk, LANE), 1)
    return jnp.where(lane < heads, b_p, b_s)


def _gate_stats(gates, b, heads):
    chunk = gates.shape[0]
    assert 2 * heads <= COMBOS and 2 * N_SPLIT * COMBOS <= LANE
    lane = lax.broadcasted_iota(jnp.int32, (chunk, LANE), 1)
    row = lax.broadcasted_iota(jnp.int32, (chunk, LANE), 0)
    fwd = lane < heads
    a = gates[:, 0:LANE] - b

    run_max = a
    sh = 1
    while sh < chunk:
        up = pltpu.roll(run_max, sh, axis=0)
        dn = pltpu.roll(run_max, chunk - sh, axis=0)
        cand = jnp.where(fwd, jnp.where(row >= sh, up, MASKED_LOG),
                         jnp.where(row < chunk - sh, dn, MASKED_LOG))
        run_max = jnp.maximum(run_max, cand)
        sh *= 2

    a_rows = a.T[0:COMBOS, :]
    b_rows = b.T[0:COMBOS, :]
    rows = jnp.concatenate(
        [a_rows,
         jnp.broadcast_to(jnp.max(a_rows, axis=1, keepdims=True), (COMBOS, chunk)),
         jnp.broadcast_to(jnp.min(b_rows, axis=1, keepdims=True), (COMBOS, chunk))], axis=0)

    packed = jnp.zeros((chunk, LANE), F32)
    pieces = _split3(run_max) + _split3(b)
    for i, piece in enumerate(pieces):
        val = jnp.where(lane < COMBOS, piece.astype(F32), 0.0)
        packed = packed + (pltpu.roll(val, COMBOS * i, axis=1) if i else val)
    return packed.astype(BF16), rows


SCAN_GROUP = 4


def _twice(r):
    return jnp.concatenate([r, r], axis=1)


def _chunk_front(q, k, v, xs, a_row, a_max, sel, *, need_h):
    chunk, dh = k.shape
    v_ext = jnp.concatenate([v, jnp.ones_like(v)], axis=1)
    k_t = k.T
    front = {"q": q, "v_ext": v_ext, "a_row": a_row, "a_max": a_max}
    if need_h:
        front["stats"] = jnp.dot(xs, sel, preferred_element_type=F32)
        front["raw"] = jnp.dot(q, k_t, preferred_element_type=F32)
    a_max_row = jnp.concatenate([a_max] * (chunk // dh), axis=1)
    kw_t = (k_t.astype(F32) * jnp.exp(a_row - a_max_row)).astype(BF16)
    front["update"] = jnp.dot(kw_t, v_ext, preferred_element_type=F32)
    return front


def _chunk_back(front, b_last, c_ext, m_st, *, reverse, need_h):
    q, v_ext, a_row, a_max = front["q"], front["v_ext"], front["a_row"], front["a_max"]
    chunk, dh = v_ext.shape[0], v_ext.shape[1] // 2
    h = None
    if need_h:
        a_run, b_col = front["stats"][:, :dh], front["stats"][:, dh:]
        raw = front["raw"]
        g_row = jnp.maximum(a_run, m_st)
        jj = lax.broadcasted_iota(jnp.int32, (dh, dh), 0)
        ss = lax.broadcasted_iota(jnp.int32, (dh, dh), 1)
        diag_seen = (ss >= jj) if reverse else (ss <= jj)
        n_blk = chunk // dh
        score_rows = []
        for rb in range(n_blk):
            r_sl = slice(rb * dh, (rb + 1) * dh)
            blocks = []
            for cb in range(n_blk):
                c_sl = slice(cb * dh, (cb + 1) * dh)
                if (cb < rb) if reverse else (cb > rb):
                    blocks.append(jnp.zeros((dh, dh), BF16))
                    continue
                arg = a_row[:, c_sl] - g_row[r_sl, :]
                if cb == rb:
                    arg = jnp.where(diag_seen, arg, MASKED_LOG)
                blocks.append((raw[r_sl, c_sl] * jnp.exp(arg)).astype(BF16))
            score_rows.append(jnp.concatenate(blocks, axis=1))
        scores = jnp.concatenate(score_rows, axis=0)
        q_inter = (q.astype(F32) * jnp.exp(m_st - g_row)).astype(BF16)
        both = jnp.dot(jnp.concatenate([scores, q_inter], axis=1),
                       jnp.concatenate([v_ext, c_ext.astype(BF16)], axis=0),
                       preferred_element_type=F32)
        floor = jnp.exp(-(b_col + g_row))
        h = both[:, :dh] / jnp.maximum(jnp.abs(both[:, dh:]), floor)

    g_last = jnp.maximum(a_max, m_st)
    c_new = (_twice(jnp.exp(m_st - g_last)) * c_ext
             + _twice(jnp.exp(a_max - g_last)) * front["update"])
    return h, c_new, b_last + g_last


def _mlstm_kernel(*refs, chunk, n_lat, n_ctx, ctx_out, heads):
    if ctx_out:
        (ql, kl, vl, ol, qc, kc, vc, oc, xsl, rwl, xsc, rwc, nw,
         out_l, out_c, hf_l, hb_l, hf_c, hb_c, c_ref, m_ref) = refs
    else:
        (ql, kl, vl, ol, kc, vc, xsl, rwl, xsc, rwc, nw,
         out_l, hf_l, hb_l, c_ref, m_ref) = refs
        qc = oc = out_c = hf_c = hb_c = None
    dh = nw.shape[-1]
    head = pl.program_id(1)

    c_ref[...] = jnp.zeros_like(c_ref)
    m_ref[...] = jnp.zeros_like(m_ref)

    kk = lax.broadcasted_iota(jnp.int32, (LANE, 2 * dh), 0)
    nn = lax.broadcasted_iota(jnp.int32, (LANE, 2 * dh), 1)
    piece = jnp.right_shift(kk, COMBOS.bit_length() - 1)
    first_piece = jnp.where(nn < dh, 0, N_SPLIT)
    target = (piece >= first_piece) & (piece < first_piece + N_SPLIT)
    pair_lane = jnp.bitwise_and(kk, COMBOS - 1)
    sels = [jnp.where(target & (pair_lane == direction * heads + head), 1.0, 0.0).astype(BF16)
            for direction in (0, 1)]

    def scan(q_ref, k_ref, v_ref, xs_ref, rw_ref, hf_ref, hb_ref, n_chunks, need_h):
        group = SCAN_GROUP if n_chunks % SCAN_GROUP == 0 else 1

        def front(i, direction):
            rows = pl.ds(pl.multiple_of(i * chunk, chunk), chunk)
            pair = direction * heads + head
            a_row = rw_ref[i, pl.ds(pair, 1), :]
            a_max = rw_ref[i, pl.ds(COMBOS + pair, 1), :][:, 0:dh]
            return _chunk_front(q_ref[rows, :] if need_h else None, k_ref[rows, :], v_ref[rows, :],
                                xs_ref[rows, :], a_row, a_max, sels[direction], need_h=need_h)

        def back(i, direction, fr):
            pair = direction * heads + head
            b_last = rw_ref[i, pl.ds(2 * COMBOS + pair, 1), :][:, 0:dh]
            h, c_new, m_new = _chunk_back(fr, b_last, c_ref[direction], m_ref[direction],
                                          reverse=direction == 1, need_h=need_h)
            c_ref[direction] = c_new
            m_ref[direction] = m_new
            if need_h:
                rows = pl.ds(pl.multiple_of(i * chunk, chunk), chunk)
                (hf_ref, hb_ref)[direction][rows, :] = h

        def body(gi, carry):
            steps = []
            for u in range(group):
                i = gi * group + u
                steps += [(i, 0), (n_chunks - 1 - i, 1)]
            fr = front(*steps[0])
            for s_idx, (i, direction) in enumerate(steps):
                fr_next = front(*steps[s_idx + 1]) if s_idx + 1 < len(steps) else None
                back(i, direction, fr)
                fr = fr_next
            return carry

        lax.fori_loop(0, n_chunks // group, body, 0)

    def finish(hf_ref, hb_ref, o_ref, out_ref, n_chunks):
        mean_w = jnp.full((dh, dh), 1.0 / dh, BF16)

        def body(i, carry):
            rows = pl.ds(pl.multiple_of(i * chunk, chunk), chunk)
            h = hf_ref[rows, :] + hb_ref[rows, :]
            mu = jnp.dot(h.astype(BF16), mean_w, preferred_element_type=F32)
            dlt = h - mu
            var = jnp.dot((dlt * dlt).astype(BF16), mean_w, preferred_element_type=F32)
            y = dlt * lax.rsqrt(var + NORM_EPS) * nw[...]
            out_ref[rows, :] = (y * jax.nn.sigmoid(o_ref[rows, :].astype(F32))).astype(BF16)
            return carry

        lax.fori_loop(0, n_chunks, body, 0, unroll=2 if n_chunks % 2 == 0 else 1)

    scan(qc, kc, vc, xsc, rwc, hf_c, hb_c, n_ctx, ctx_out)
    scan(ql, kl, vl, xsl, rwl, hf_l, hb_l, n_lat, True)
    finish(hf_l, hb_l, ol, out_l, n_lat)
    if ctx_out:
        finish(hf_c, hb_c, oc, out_c, n_ctx)


def _mlstm(p_lat, p_ctx, stats_lat, stats_ctx, norm_w, *, n_seq, t_lat, t_ctx, heads, dh, chunk,
           lat_blk, ctx_blk, ctx_out, name):
    assert chunk % dh == 0
    n_lat, n_ctx = t_lat // chunk, t_ctx // chunk

    def tok_spec(t, blk):
        return pl.BlockSpec((t, dh), lambda b, h: (b, blk + h))

    def stat_specs(t, n_chunks):
        return [pl.BlockSpec((t, LANE), lambda b, h: (b, 0)),
                pl.BlockSpec((n_chunks, 3 * COMBOS, chunk), lambda b, h: (b, 0, 0))]

    lat_in = [tok_spec(t_lat, lat_blk[name_]) for name_ in "qkvo"]
    ctx_names = "qkvo" if ctx_out else "kv"
    ctx_in = [tok_spec(t_ctx, ctx_blk[name_]) for name_ in ctx_names]
    in_specs = (lat_in + ctx_in + stat_specs(t_lat, n_lat) + stat_specs(t_ctx, n_ctx)
                + [pl.BlockSpec((1, dh), lambda b, h: (0, h))])
    args = ([p_lat] * 4 + [p_ctx] * len(ctx_names) + list(stats_lat) + list(stats_ctx)
            + [norm_w.reshape(1, heads * dh)])

    out_specs = [pl.BlockSpec((t_lat, dh), lambda b, h: (b, h))]
    out_shape = [jax.ShapeDtypeStruct((n_seq * t_lat, heads * dh), BF16)]
    scratch = [pltpu.VMEM((t_lat, dh), F32), pltpu.VMEM((t_lat, dh), F32)]
    if ctx_out:
        out_specs.append(pl.BlockSpec((t_ctx, dh), lambda b, h: (b, h)))
        out_shape.append(jax.ShapeDtypeStruct((n_seq * t_ctx, heads * dh), BF16))
        scratch += [pltpu.VMEM((t_ctx, dh), F32), pltpu.VMEM((t_ctx, dh), F32)]
    scratch += [pltpu.VMEM((2, dh, 2 * dh), F32), pltpu.VMEM((2, 1, dh), F32)]

    kern = functools.partial(_mlstm_kernel, chunk=chunk, n_lat=n_lat, n_ctx=n_ctx, ctx_out=ctx_out,
                             heads=heads)
    return pl.pallas_call(
        kern,
        grid=(n_seq, heads),
        in_specs=in_specs,
        out_specs=out_specs,
        out_shape=out_shape,
        scratch_shapes=scratch,
        compiler_params=_params(2),
        name=name,
    )(*args)


OUT_PROJ_SUB_ROWS = 256


def _out_proj_kernel(a_ref, m_ref, c_ref, w_ref, x_ref, gpost_ref, g1_ref, gpre_ref, sh2_ref, sc2_ref,
                     x1_ref, h2_ref, *, sub_rows):
    da, dm = a_ref.shape[-1], m_ref.shape[-1]
    tm = x_ref.shape[0]
    gain1 = g1_ref[0] * gpost_ref[...]
    gain2 = gpre_ref[...] * (1.0 + sc2_ref[0])
    for r in range(tm // sub_rows):
        rows = slice(r * sub_rows, (r + 1) * sub_rows)
        y = (jnp.dot(a_ref[rows, :], w_ref[0:da, :], preferred_element_type=F32)
             + jnp.dot(m_ref[rows, :], w_ref[da:da + dm, :], preferred_element_type=F32)
             + jnp.dot(c_ref[rows, :], w_ref[da + dm:, :], preferred_element_type=F32))
        x1 = x_ref[rows, :] + _rms(y) * gain1
        x1_ref[rows, :] = x1
        h2_ref[rows, :] = (_rms(x1) * gain2 + sh2_ref[0]).astype(BF16)


def _out_proj(a, m, cc, w_out, x2d, mod3, mod_row, g_post, g_pre_ffn, *, layer, tm, name):
    n, d = x2d.shape
    da, dm, dc = a.shape[1], m.shape[1], cc.shape[1]
    row = lambda i: (i, 0)
    fixed = lambda i: (0, 0)
    mod_spec = lambda k: pl.BlockSpec((1, 1, d), lambda i: (mod_row(i), 0, k))
    return pl.pallas_call(
        functools.partial(_out_proj_kernel, sub_rows=min(tm, OUT_PROJ_SUB_ROWS)),
        grid=(n // tm,),
        in_specs=[pl.BlockSpec((tm, da), row), pl.BlockSpec((tm, dm), row), pl.BlockSpec((tm, dc), row),
                  pl.BlockSpec((None, da + dm + dc, d), lambda i: (layer, 0, 0),
                               pipeline_mode=pl.Buffered(1)),
                  pl.BlockSpec((tm, d), row),
                  pl.BlockSpec((1, d), fixed),
                  mod_spec(2),
                  pl.BlockSpec((1, d), fixed),
                  mod_spec(3), mod_spec(4)],
        out_specs=[pl.BlockSpec((tm, d), row), pl.BlockSpec((tm, d), row)],
        out_shape=[jax.ShapeDtypeStruct((n, d), F32), jax.ShapeDtypeStruct((n, d), BF16)],
        compiler_params=_params(1),
        name=name,
    )(a, m, cc, w_out, x2d, g_post.reshape(1, d), mod3, g_pre_ffn.reshape(1, d), mod3, mod3)


FFN_TF = 1024


def _ffn_kernel(h_ref, w1_ref, w2_ref, x_ref, gpost_ref, g2_ref, o_ref, acc_ref):
    j = pl.program_id(1)

    @pl.when(j == 0)
    def _():
        acc_ref[...] = jnp.zeros_like(acc_ref)

    u = jnp.maximum(jnp.dot(h_ref[...], w1_ref[...], preferred_element_type=F32), 0.0)
    acc_ref[...] += jnp.dot((u * u).astype(BF16), w2_ref[...], preferred_element_type=F32)

    @pl.when(j == pl.num_programs(1) - 1)
    def _():
        o_ref[...] = x_ref[...] + g2_ref[0] * (_rms(acc_ref[...]) * gpost_ref[...])


def _ffn(h2, w1, w2, x1, mod3, mod_row, g_post, *, layer, tm, tf, name):
    n, d = x1.shape
    dff = w1.shape[2]
    return pl.pallas_call(
        _ffn_kernel,
        grid=(n // tm, dff // tf),
        in_specs=[pl.BlockSpec((tm, d), lambda i, j: (i, 0)),
                  pl.BlockSpec((None, d, tf), lambda i, j: (layer, 0, j)),
                  pl.BlockSpec((None, tf, d), lambda i, j: (layer, j, 0)),
                  pl.BlockSpec((tm, d), lambda i, j: (i, 0)),
                  pl.BlockSpec((1, d), lambda i, j: (0, 0)),
                  pl.BlockSpec((1, 1, d), lambda i, j: (mod_row(i), 0, 5))],
        out_specs=pl.BlockSpec((tm, d), lambda i, j: (i, 0)),
        out_shape=jax.ShapeDtypeStruct((n, d), F32),
        scratch_shapes=[pltpu.VMEM((tm, d), F32)],
        compiler_params=_params(2),
        name=name,
    )(h2, w1, w2, x1, g_post.reshape(1, d), mod3)


def _tile(n, target):
    t = min(n, target)
    while n % t or (t % 8 and t != n):
        t -= 1
    return t


def kernel(x, c, ctx, c_ctx, w_ada, b_ada, g_pre_mix, g_post_mix, g_pre_ffn, g_post_ffn, w_in, b_gates,
           conv_a_w, conv_a_b, ln_a_w, ln_a_b, mlstm_norm_w, conv_c_w, w_out, w_ff1, w_ff2):
    bsz, t_lat, d = x.shape
    t_ctx = ctx.shape[1]
    depth = w_ada.shape[0]
    d_conv = conv_a_w.shape[-1]
    d_short = conv_c_w.shape[-1]
    d_mlstm = mlstm_norm_w.shape[-1]
    heads = b_gates.shape[-1] // 4
    dh = d_mlstm // heads
    n_gates = 4 * heads
    assert dh == LANE and d_conv == d_short and d_conv % LANE == 0
    assert bsz + 1 <= MOD_ROWS and t_lat % GRID_W == 0
    assert w_in.shape[-1] == 2 * d_conv + 4 * d_mlstm + n_gates + 3 * d_short

    tn = d_conv
    chunk = min(256, t_ctx)
    assert d_mlstm % tn == 0 and t_ctx % chunk == 0 and t_lat % chunk == 0
    gate_lo = 2 * d_conv + 4 * d_mlstm
    mt = d_mlstm // tn
    tile = {"a_val": 0, "a_gate": 1, "q": 2, "k": 2 + mt, "v": 2 + 2 * mt, "o": 2 + 3 * mt,
            "s_in": 2 + 4 * mt, "s_b": 3 + 4 * mt, "s_c": 4 + 4 * mt}
    lane_blk = {k_: v_ * (tn // dh) for k_, v_ in tile.items()}
    q_scale = float(dh) ** -0.5

    xl = x.reshape(bsz * t_lat, d)
    xc = ctx.reshape(bsz * t_ctx, d)

    cond = jnp.concatenate([c, c_ctx[None, :], jnp.zeros((MOD_ROWS - bsz - 1, d), F32)], axis=0)
    mod3 = _modulation(cond, w_ada, b_ada).reshape(depth * MOD_ROWS, 1, 6 * d)

    tm_in = _tile(t_lat, 512)
    tm_in_ctx = _tile(bsz * t_ctx, 512)
    tm_lat = _tile(t_lat, 512)
    tm_ctx = _tile(bsz * t_ctx, 512)
    tm_out = _tile(t_lat, 512)
    tm_out_ctx = _tile(bsz * t_ctx, 512)

    def gate_cols(g):
        i_f, f_f, i_b, f_b = jnp.split(g, 4, axis=-1)
        pad = jnp.zeros(g.shape[:-1] + (LANE - 2 * heads,), g.dtype)
        return jnp.concatenate([i_f, i_b, pad, f_f, f_b, pad], axis=-1)

    w_in_b = w_in.astype(BF16)
    w_tail = w_in_b[:, :, gate_lo + n_gates:]
    kv_cols = (tile["o"] - tile["k"]) * tn
    assert (tile["k"] * tn) % kv_cols == 0
    full_parts = [(w_in_b, gate_lo, 0), (w_tail, w_tail.shape[2], 0)]
    kv_parts = [(w_in_b, kv_cols, tile["k"] * tn // kv_cols)]
    w_out_b = w_out.astype(BF16)
    w1_b = w_ff1.astype(BF16)
    w2_b = w_ff2.astype(BF16)

    for layer in range(depth):
        last = layer == depth - 1
        base = layer * MOD_ROWS
        wg = gate_cols(w_in[layer][:, gate_lo:gate_lo + n_gates]).astype(BF16)
        bg = gate_cols(b_gates[layer]).reshape(1, 2 * LANE)

        def lat_row(tm):
            return lambda i: base + (i * tm) // t_lat
        ctx_row = lambda i: base + bsz

        proj = functools.partial(_in_proj, mod3=mod3, g_pre=g_pre_mix[layer], wg=wg, bg=bg, layer=layer,
                                 chunk=chunk, heads=heads, tn=tn, q_scale=q_scale)
        p_lat, *stats_lat = proj(xl, mod_row=lat_row(tm_in), w_parts=full_parts, tm=tm_in,
                                 q_tile_lo=tile["q"], q_tile_hi=tile["k"], name=f"in_proj_lat_{layer}")
        if last:
            p_ctx, *stats_ctx = proj(xc, mod_row=ctx_row, w_parts=kv_parts, tm=tm_in_ctx,
                                     q_tile_lo=0, q_tile_hi=0, name=f"in_proj_ctx_{layer}")
            ctx_blk = {"k": 0, "v": lane_blk["v"] - lane_blk["k"]}
        else:
            p_ctx, *stats_ctx = proj(xc, mod_row=ctx_row, w_parts=full_parts, tm=tm_in_ctx,
                                     q_tile_lo=tile["q"], q_tile_hi=tile["k"], name=f"in_proj_ctx_{layer}")
            ctx_blk = lane_blk

        mres = _mlstm(p_lat, p_ctx, stats_lat, stats_ctx, mlstm_norm_w[layer],
                      n_seq=bsz, t_lat=t_lat, t_ctx=t_ctx, heads=heads, dh=dh, chunk=chunk,
                      lat_blk=lane_blk, ctx_blk=ctx_blk, ctx_out=not last, name=f"mlstm_{layer}")

        conv_a = functools.partial(_conv_a, conv_w=conv_a_w[layer], conv_b=conv_a_b[layer],
                                   ln_w=ln_a_w[layer], ln_b=ln_a_b[layer], n_seq=bsz,
                                   val_blk=tile["a_val"], gate_blk=tile["a_gate"])
        conv_c = functools.partial(_conv_c, conv_w=conv_c_w[layer], n_seq=bsz, in_blk=tile["s_in"],
                                   b_blk=tile["s_b"], c_blk=tile["s_c"])
        a_lat = conv_a(p_lat, seq_len=t_lat, row_len=GRID_W, name=f"conv_a_lat_{layer}")
        c_lat = conv_c(p_lat, seq_len=t_lat, shift=GRID_W, name=f"conv_c_lat_{layer}")
        x1, h2 = _out_proj(a_lat, mres[0], c_lat, w_out_b, xl, mod3, lat_row(tm_out), g_post_mix[layer],
                           g_pre_ffn[layer], layer=layer, tm=tm_out, name=f"out_proj_lat_{layer}")
        xl = _ffn(h2, w1_b, w2_b, x1, mod3, lat_row(tm_lat), g_post_ffn[layer], layer=layer, tm=tm_lat,
                  tf=FFN_TF,
                  name=f"ffn_lat_{layer}")

        if not last:
            a_ctx = conv_a(p_ctx, seq_len=t_ctx, row_len=t_ctx, name=f"conv_a_ctx_{layer}")
            c_ctx_mix = conv_c(p_ctx, seq_len=t_ctx, shift=1, name=f"conv_c_ctx_{layer}")
            x1c, h2c = _out_proj(a_ctx, mres[1], c_ctx_mix, w_out_b, xc, mod3, ctx_row, g_post_mix[layer],
                                 g_pre_ffn[layer], layer=layer, tm=tm_out_ctx, name=f"out_proj_ctx_{layer}")
            xc = _ffn(h2c, w1_b, w2_b, x1c, mod3, ctx_row, g_post_ffn[layer], layer=layer, tm=tm_ctx,
                      tf=FFN_TF,
                      name=f"ffn_ctx_{layer}")

    return xl.reshape(bsz, t_lat, d)
```

```python
import functools

import jax
import jax.numpy as jnp
from jax import lax
from jax.experimental import pallas as pl
from jax.experimental.pallas import tpu as pltpu

GRID_W = 64
NORM_EPS = 1e-6
LANE = 128
SUBLANE = 8
VMEM_LIMIT_BYTES = 56 * 1024 * 1024
MOD_ROWS = 16
MASKED_LOG = -1e30
COMBOS = 16
N_SPLIT = 3

F32 = jnp.float32
BF16 = jnp.bfloat16


def _params(n_grid_axes):
    return pltpu.CompilerParams(
        dimension_semantics=("arbitrary",) * n_grid_axes,
        vmem_limit_bytes=VMEM_LIMIT_BYTES)


def _rms(y):
    return y * lax.rsqrt(jnp.mean(y * y, axis=-1, keepdims=True) + NORM_EPS)


def _split3(x):
    hi = x.astype(BF16)
    r1 = x - hi.astype(F32)
    mid = r1.astype(BF16)
    lo = (r1 - mid.astype(F32)).astype(BF16)
    return hi, mid, lo


def _mod_kernel(s_ref, w_ref, b_ref, o_ref):
    s = s_ref[...]
    a = (s * jax.nn.sigmoid(s)).astype(BF16)
    o_ref[0] = jnp.dot(a, w_ref[0].astype(BF16), preferred_element_type=F32) + b_ref[0]


def _modulation(s, w_ada, b_ada, tn=1024):
    depth, d, n = w_ada.shape
    return pl.pallas_call(
        _mod_kernel,
        grid=(depth, n // tn),
        in_specs=[pl.BlockSpec((MOD_ROWS, d), lambda l, j: (0, 0)),
                  pl.BlockSpec((1, d, tn), lambda l, j: (l, 0, j)),
                  pl.BlockSpec((1, 1, tn), lambda l, j: (l, 0, j))],
        out_specs=pl.BlockSpec((1, MOD_ROWS, tn), lambda l, j: (l, 0, j)),
        out_shape=jax.ShapeDtypeStruct((depth, MOD_ROWS, n), F32),
        compiler_params=_params(2),
        name="adaln_modulation",
    )(s, w_ada, b_ada.reshape(depth, 1, n))


def _in_proj_kernel(*refs, n_parts, chunk, heads, tn, q_tile_lo, q_tile_hi, q_scale, conv_row_len,
                    conv_taps):
    x_ref, g_ref, sh_ref, sc_ref, wg_ref, bg_ref = refs[:6]
    n_in = 6
    conv = conv_row_len is not None
    if conv:
        cw_ref, cb_ref, lnw_ref, lnb_ref = refs[n_in:n_in + 4]
        n_in += 4
    w_refs = refs[n_in:n_in + n_parts]
    outs = refs[n_in + n_parts:]
    if conv:
        p_ref, xs_ref, rows_ref, a_ref, upad_ref, y_ref = outs
    else:
        p_ref, xs_ref, rows_ref = outs
    n_conv = CONV_IN_TILES if conv else 0

    gain = g_ref[...] * (1.0 + sc_ref[0])
    n_sub = x_ref.shape[0] // chunk
    tiles = [(w_ref, j) for w_ref in w_refs for j in range(w_ref.shape[1] // tn)]
    last = len(tiles) - 1
    if conv:
        rows_per_sub = chunk // conv_row_len
        units = [(g, sb) for g in range(rows_per_sub) for sb in range(conv_row_len // CONV_SUB)]
        unit_at = {}
        for pos, unit in zip(CONV_UNIT_TILES, units):
            unit_at.setdefault(min(pos, last), []).append(unit)
        assert len(units) <= len(CONV_UNIT_TILES)
        zeros = jnp.zeros((CONV_A_PAD, upad_ref.shape[-1]), F32)
        for row in range(n_sub * rows_per_sub):
            upad_ref[row, 0:CONV_A_PAD, :] = zeros
            upad_ref[row, CONV_A_PAD + conv_row_len:CONV_A_PAD + conv_row_len + CONV_A_PAD, :] = zeros

    def normed(r):
        return (_rms(x_ref[r * chunk:(r + 1) * chunk, :]) * gain + sh_ref[0]).astype(BF16)

    hb = normed(0)
    for r in range(n_sub):
        rows = slice(r * chunk, (r + 1) * chunk)
        gates = jnp.dot(hb, wg_ref[...], preferred_element_type=F32) + bg_ref[...]
        hb_next = None
        conv_in = []
        for t, (w_ref, j) in enumerate(tiles):
            acc = jnp.dot(hb, w_ref[:, j * tn:(j + 1) * tn], preferred_element_type=F32)
            if t < n_conv:
                conv_in.append(acc)
                if t == n_conv - 1:
                    u = conv_in[0] * jax.nn.sigmoid(conv_in[1])
                    for g in range(rows_per_sub):
                        upad_ref[r * rows_per_sub + g, CONV_A_PAD:CONV_A_PAD + conv_row_len, :] = (
                            u[g * conv_row_len:(g + 1) * conv_row_len, :])
            else:
                if q_tile_lo <= t < q_tile_hi:
                    acc = acc * q_scale
                p_ref[rows, (t - n_conv) * tn:(t - n_conv + 1) * tn] = acc.astype(BF16)
            if t == 0:
                pieces = _forget_pieces(gates)
            if t == min(2, last):
                b = _forget_cumsum(pieces, heads)
            if t == min(4, last):
                xs_ref[rows, :], rows_ref[r] = _gate_stats(gates, b, heads)
            if t == min(6, last) and r + 1 < n_sub:
                hb_next = normed(r + 1)
            if conv:
                for g, sb in unit_at.get(t, []):
                    _conv_a_unit(upad_ref, y_ref.at[(r * rows_per_sub + g) % y_ref.shape[0]],
                                 cw_ref, cb_ref, lnw_ref, lnb_ref, a_ref, r * rows_per_sub + g, sb,
                                 r * chunk + g * conv_row_len + sb * CONV_SUB, conv_taps)
        hb = hb_next


def _in_proj(x2d, mod3, mod_row, g_pre, w_parts, wg, bg, conv, *, layer, chunk, heads, tm, tn, q_tile_lo,
             q_tile_hi, q_scale, name):
    n, d = x2d.shape
    ng = wg.shape[1]
    nw = sum(width for _, width, _ in w_parts) - (CONV_IN_TILES * tn if conv else 0)
    assert tm % chunk == 0 and all(width % tn == 0 for _, width, _ in w_parts)
    fixed = lambda i: (0, 0)

    def w_spec(width, blk):
        return pl.BlockSpec((None, d, width), lambda i: (layer, 0, blk), pipeline_mode=pl.Buffered(1))

    in_specs = [pl.BlockSpec((tm, d), lambda i: (i, 0)),
                pl.BlockSpec((1, d), fixed),
                pl.BlockSpec((1, 1, d), lambda i: (mod_row(i), 0, 0)),
                pl.BlockSpec((1, 1, d), lambda i: (mod_row(i), 0, 1)),
                pl.BlockSpec((d, ng), fixed),
                pl.BlockSpec((1, ng), fixed)]
    args = [x2d, g_pre.reshape(1, d), mod3, mod3, wg, bg]
    out_specs = [pl.BlockSpec((tm, nw), lambda i: (i, 0)),
                 pl.BlockSpec((tm, LANE), lambda i: (i, 0)),
                 pl.BlockSpec((tm // chunk, 3 * COMBOS, chunk), lambda i: (i, 0, 0))]
    out_shape = [jax.ShapeDtypeStruct((n, nw), BF16),
                 jax.ShapeDtypeStruct((n, LANE), BF16),
                 jax.ShapeDtypeStruct((n // chunk, 3 * COMBOS, chunk), F32)]
    scratch = []
    row_len = taps = None
    if conv:
        row_len, conv_w, conv_b, ln_w, ln_b = conv
        taps = conv_w.shape[0]
        assert conv_w.shape[1] == tn and chunk % row_len == 0 and row_len % CONV_SUB == 0
        assert taps // 2 < CONV_A_PAD
        in_specs += [pl.BlockSpec((taps, tn), fixed)] + [pl.BlockSpec((1, tn), fixed)] * 3
        args += [conv_w, conv_b.reshape(1, tn), ln_w.reshape(1, tn), ln_b.reshape(1, tn)]
        out_specs.append(pl.BlockSpec((tm, tn), lambda i: (i, 0)))
        out_shape.append(jax.ShapeDtypeStruct((n, tn), BF16))
        conv_rows = tm // row_len
        scratch = [pltpu.VMEM((conv_rows, row_len + 2 * CONV_A_PAD, tn), F32),
                   pltpu.VMEM((min(conv_rows, 4), CONV_SUB, tn), F32)]
    in_specs += [w_spec(width, blk) for _, width, blk in w_parts]
    args += [arr for arr, _, _ in w_parts]

    kern = functools.partial(_in_proj_kernel, n_parts=len(w_parts), chunk=chunk, heads=heads, tn=tn,
                             q_tile_lo=q_tile_lo, q_tile_hi=q_tile_hi, q_scale=q_scale,
                             conv_row_len=row_len, conv_taps=taps)
    return pl.pallas_call(
        kern,
        grid=(n // tm,),
        in_specs=in_specs,
        out_specs=out_specs,
        out_shape=out_shape,
        scratch_shapes=scratch,
        compiler_params=_params(1),
        name=name,
    )(*args)


CONV_A_PAD = 16
CONV_SUB = 64
CONV_IN_TILES = 2
CONV_UNIT_TILES = (3, 5, 8, 10)


def _conv_a_unit(upad_ref, y_ref, w_ref, b_ref, lnw_ref, lnb_ref, o_ref, row, sb, out_row, taps):
    ch = o_ref.shape[-1]
    half = taps // 2
    for c in range(ch // LANE):
        cs = slice(c * LANE, (c + 1) * LANE)
        acc = None
        for res in range(SUBLANE):
            z = None
            for k in range(taps):
                if (k - half) % SUBLANE != res:
                    continue
                start = CONV_A_PAD + sb * CONV_SUB + (k - half) - res
                term = w_ref[k:k + 1, cs] * upad_ref[row, start:start + CONV_SUB + SUBLANE, cs]
                z = term if z is None else z + term
            if z is None:
                continue
            z = z[res:res + CONV_SUB, :]
            acc = z if acc is None else acc + z
        y_ref[:, cs] = acc + b_ref[:, cs]
    y = y_ref[...]
    mu = jnp.mean(y, axis=-1, keepdims=True)
    dlt = y - mu
    var = jnp.mean(dlt * dlt, axis=-1, keepdims=True)
    z = dlt * lax.rsqrt(var + NORM_EPS) * lnw_ref[...] + lnb_ref[...]
    o_ref[out_row:out_row + CONV_SUB, :] = (z * jax.nn.sigmoid(z)).astype(BF16)


CONV_C_PAD = 64
CONV_C_BLK = 64


def _conv_c_kernel(sin_ref, sb_ref, sc_ref, w_ref, o_ref, upad_ref, *, seq_len, shift):
    ch = o_ref.shape[-1]
    zeros = jnp.zeros((CONV_C_PAD, ch), F32)
    upad_ref[0:CONV_C_PAD, :] = zeros
    upad_ref[CONV_C_PAD + seq_len:CONV_C_PAD + seq_len + CONV_C_PAD, :] = zeros
    for i in range(seq_len // CONV_C_BLK):
        t = slice(i * CONV_C_BLK, (i + 1) * CONV_C_BLK)
        upad_ref[CONV_C_PAD + i * CONV_C_BLK:CONV_C_PAD + (i + 1) * CONV_C_BLK, :] = (
            sc_ref[t, :].astype(F32) * sin_ref[t, :].astype(F32))
    for i in range(seq_len // CONV_C_BLK):
        t0 = CONV_C_PAD + i * CONV_C_BLK
        y = (w_ref[0:1, :] * upad_ref[t0 - shift:t0 - shift + CONV_C_BLK, :]
             + w_ref[1:2, :] * upad_ref[t0:t0 + CONV_C_BLK, :]
             + w_ref[2:3, :] * upad_ref[t0 + shift:t0 + shift + CONV_C_BLK, :])
        t = slice(i * CONV_C_BLK, (i + 1) * CONV_C_BLK)
        o_ref[t, :] = (sb_ref[t, :].astype(F32) * y).astype(BF16)


def _conv_c(p, conv_w, *, n_seq, seq_len, shift, in_blk, b_blk, c_blk, name):
    taps, ch = conv_w.shape
    assert taps == 3 and shift <= CONV_C_PAD
    n = n_seq * seq_len
    kern = functools.partial(_conv_c_kernel, seq_len=seq_len, shift=shift)
    return pl.pallas_call(
        kern,
        grid=(n_seq,),
        in_specs=[pl.BlockSpec((seq_len, ch), lambda b: (b, in_blk)),
                  pl.BlockSpec((seq_len, ch), lambda b: (b, b_blk)),
                  pl.BlockSpec((seq_len, ch), lambda b: (b, c_blk)),
                  pl.BlockSpec((taps, ch), lambda b: (0, 0))],
        out_specs=pl.BlockSpec((seq_len, ch), lambda b: (b, 0)),
        out_shape=jax.ShapeDtypeStruct((n, ch), BF16),
        scratch_shapes=[pltpu.VMEM((seq_len + 2 * CONV_C_PAD, ch), F32)],
        compiler_params=_params(1),
        name=name,
    )(p, p, p, conv_w)


def _log_sigmoid(z):
    return jnp.minimum(z, 0.0) - jnp.log1p(jnp.exp(-jnp.abs(z)))


def _forget_pieces(gates):
    return _split3(_log_sigmoid(gates[:, LANE:2 * LANE]))


def _forget_cumsum(pieces, heads):
    chunk = pieces[0].shape[0]
    jj = lax.broadcasted_iota(jnp.int32, (chunk, chunk), 0)
    ss = lax.broadcasted_iota(jnp.int32, (chunk, chunk), 1)
    tri_p = jnp.where(ss <= jj, 1.0, 0.0).astype(BF16)
    tri_s = jnp.where(ss >= jj, 1.0, 0.0).astype(BF16)
    b_p = jnp.zeros((chunk, LANE), F32)
    b_s = jnp.zeros((chunk, LANE), F32)
    for piece in pieces:
        b_p = b_p + jnp.dot(tri_p, piece, preferred_element_type=F32)
        b_s = b_s + jnp.dot(tri_s, piece, preferred_element_type=F32)
    lane = lax.broadcasted_iota(jnp.int32, (chunk, LANE), 1)
    return jnp.where(lane < heads, b_p, b_s)


def _gate_stats(gates, b, heads):
    chunk = gates.shape[0]
    assert 2 * heads <= COMBOS and 2 * N_SPLIT * COMBOS <= LANE
    lane = lax.broadcasted_iota(jnp.int32, (chunk, LANE), 1)
    row = lax.broadcasted_iota(jnp.int32, (chunk, LANE), 0)
    fwd = lane < heads
    a = gates[:, 0:LANE] - b

    run_max = a
    sh = 1
    while sh < chunk:
        up = pltpu.roll(run_max, sh, axis=0)
        dn = pltpu.roll(run_max, chunk - sh, axis=0)
        cand = jnp.where(fwd, jnp.where(row >= sh, up, MASKED_LOG),
                         jnp.where(row < chunk - sh, dn, MASKED_LOG))
        run_max = jnp.maximum(run_max, cand)
        sh *= 2

    a_rows = a.T[0:COMBOS, :]
    b_rows = b.T[0:COMBOS, :]
    rows = jnp.concatenate(
        [a_rows,
         jnp.broadcast_to(jnp.max(a_rows, axis=1, keepdims=True), (COMBOS, chunk)),
         jnp.broadcast_to(jnp.min(b_rows, axis=1, keepdims=True), (COMBOS, chunk))], axis=0)

    packed = jnp.zeros((chunk, LANE), F32)
    pieces = _split3(run_max) + _split3(b)
    for i, piece in enumerate(pieces):
        val = jnp.where(lane < COMBOS, piece.astype(F32), 0.0)
        packed = packed + (pltpu.roll(val, COMBOS * i, axis=1) if i else val)
    return packed.astype(BF16), rows


SCAN_GROUP = 4


def _twice(r):
    return jnp.concatenate([r, r], axis=1)


def _chunk_front(q, k, v, xs, a_row, a_max, sel, *, need_h):
    chunk, dh = k.shape
    v_ext = jnp.concatenate([v, jnp.ones_like(v)], axis=1)
    k_t = k.T
    front = {"q": q, "v_ext": v_ext, "a_row": a_row, "a_max": a_max}
    if need_h:
        front["stats"] = jnp.dot(xs, sel, preferred_element_type=F32)
        front["raw"] = jnp.dot(q, k_t, preferred_element_type=F32)
    a_max_row = jnp.concatenate([a_max] * (chunk // dh), axis=1)
    kw_t = (k_t.astype(F32) * jnp.exp(a_row - a_max_row)).astype(BF16)
    front["update"] = jnp.dot(kw_t, v_ext, preferred_element_type=F32)
    return front


def _chunk_back(front, b_last, c_ext, m_st, *, reverse, need_h):
    q, v_ext, a_row, a_max = front["q"], front["v_ext"], front["a_row"], front["a_max"]
    chunk, dh = v_ext.shape[0], v_ext.shape[1] // 2
    h = None
    if need_h:
        a_run, b_col = front["stats"][:, :dh], front["stats"][:, dh:]
        raw = front["raw"]
        g_row = jnp.maximum(a_run, m_st)
        jj = lax.broadcasted_iota(jnp.int32, (dh, dh), 0)
        ss = lax.broadcasted_iota(jnp.int32, (dh, dh), 1)
        diag_seen = (ss >= jj) if reverse else (ss <= jj)
        n_blk = chunk // dh
        score_rows = []
        for rb in range(n_blk):
            r_sl = slice(rb * dh, (rb + 1) * dh)
            blocks = []
            for cb in range(n_blk):
                c_sl = slice(cb * dh, (cb + 1) * dh)
                if (cb < rb) if reverse else (cb > rb):
                    blocks.append(jnp.zeros((dh, dh), BF16))
                    continue
                arg = a_row[:, c_sl] - g_row[r_sl, :]
                if cb == rb:
                    arg = jnp.where(diag_seen, arg, MASKED_LOG)
                blocks.append((raw[r_sl, c_sl] * jnp.exp(arg)).astype(BF16))
            score_rows.append(jnp.concatenate(blocks, axis=1))
        scores = jnp.concatenate(score_rows, axis=0)
        q_inter = (q.astype(F32) * jnp.exp(m_st - g_row)).astype(BF16)
        both = jnp.dot(jnp.concatenate([scores, q_inter], axis=1),
                       jnp.concatenate([v_ext, c_ext.astype(BF16)], axis=0),
                       preferred_element_type=F32)
        floor = jnp.exp(-(b_col + g_row))
        h = both[:, :dh] / jnp.maximum(jnp.abs(both[:, dh:]), floor)

    g_last = jnp.maximum(a_max, m_st)
    c_new = (_twice(jnp.exp(m_st - g_last)) * c_ext
             + _twice(jnp.exp(a_max - g_last)) * front["update"])
    return h, c_new, b_last + g_last


def _mlstm_kernel(*refs, chunk, n_lat, n_ctx, ctx_out, heads):
    if ctx_out:
        (ql, kl, vl, ol, qc, kc, vc, oc, xsl, rwl, xsc, rwc, nw,
         out_l, out_c, hf_l, hb_l, hf_c, hb_c, c_ref, m_ref) = refs
    else:
        (ql, kl, vl, ol, kc, vc, xsl, rwl, xsc, rwc, nw,
         out_l, hf_l, hb_l, c_ref, m_ref) = refs
        qc = oc = out_c = hf_c = hb_c = None
    dh = nw.shape[-1]
    head = pl.program_id(1)

    c_ref[...] = jnp.zeros_like(c_ref)
    m_ref[...] = jnp.zeros_like(m_ref)

    kk = lax.broadcasted_iota(jnp.int32, (LANE, 2 * dh), 0)
    nn = lax.broadcasted_iota(jnp.int32, (LANE, 2 * dh), 1)
    piece = jnp.right_shift(kk, COMBOS.bit_length() - 1)
    first_piece = jnp.where(nn < dh, 0, N_SPLIT)
    target = (piece >= first_piece) & (piece < first_piece + N_SPLIT)
    pair_lane = jnp.bitwise_and(kk, COMBOS - 1)
    sels = [jnp.where(target & (pair_lane == direction * heads + head), 1.0, 0.0).astype(BF16)
            for direction in (0, 1)]

    def scan(q_ref, k_ref, v_ref, xs_ref, rw_ref, hf_ref, hb_ref, n_chunks, need_h):
        group = SCAN_GROUP if n_chunks % SCAN_GROUP == 0 else 1

        def front(i, direction):
            rows = pl.ds(pl.multiple_of(i * chunk, chunk), chunk)
            pair = direction * heads + head
            a_row = rw_ref[i, pl.ds(pair, 1), :]
            a_max = rw_ref[i, pl.ds(COMBOS + pair, 1), :][:, 0:dh]
            return _chunk_front(q_ref[rows, :] if need_h else None, k_ref[rows, :], v_ref[rows, :],
                                xs_ref[rows, :], a_row, a_max, sels[direction], need_h=need_h)

        def back(i, direction, fr):
            pair = direction * heads + head
            b_last = rw_ref[i, pl.ds(2 * COMBOS + pair, 1), :][:, 0:dh]
            h, c_new, m_new = _chunk_back(fr, b_last, c_ref[direction], m_ref[direction],
                                          reverse=direction == 1, need_h=need_h)
            c_ref[direction] = c_new
            m_ref[direction] = m_new
            if need_h:
                rows = pl.ds(pl.multiple_of(i * chunk, chunk), chunk)
                (hf_ref, hb_ref)[direction][rows, :] = h

        def body(gi, carry):
            steps = []
            for u in range(group):
                i = gi * group + u
                steps += [(i, 0), (n_chunks - 1 - i, 1)]
            fr = front(*steps[0])
            for s_idx, (i, direction) in enumerate(steps):
                fr_next = front(*steps[s_idx + 1]) if s_idx + 1 < len(steps) else None
                back(i, direction, fr)
                fr = fr_next
            return carry

        lax.fori_loop(0, n_chunks // group, body, 0)

    def finish(hf_ref, hb_ref, o_ref, out_ref, n_chunks):
        mean_w = jnp.full((dh, dh), 1.0 / dh, BF16)

        def body(i, carry):
            rows = pl.ds(pl.multiple_of(i * chunk, chunk), chunk)
            h = hf_ref[rows, :] + hb_ref[rows, :]
            mu = jnp.dot(h.astype(BF16), mean_w, preferred_element_type=F32)
            dlt = h - mu
            var = jnp.dot((dlt * dlt).astype(BF16), mean_w, preferred_element_type=F32)
            y = dlt * lax.rsqrt(var + NORM_EPS) * nw[...]
            out_ref[rows, :] = (y * jax.nn.sigmoid(o_ref[rows, :].astype(F32))).astype(BF16)
            return carry

        lax.fori_loop(0, n_chunks, body, 0, unroll=2 if n_chunks % 2 == 0 else 1)

    scan(qc, kc, vc, xsc, rwc, hf_c, hb_c, n_ctx, ctx_out)
    scan(ql, kl, vl, xsl, rwl, hf_l, hb_l, n_lat, True)
    finish(hf_l, hb_l, ol, out_l, n_lat)
    if ctx_out:
        finish(hf_c, hb_c, oc, out_c, n_ctx)


def _mlstm(p_lat, p_ctx, stats_lat, stats_ctx, norm_w, *, n_seq, t_lat, t_ctx, heads, dh, chunk,
           lat_blk, ctx_blk, ctx_out, name):
    assert chunk % dh == 0
    n_lat, n_ctx = t_lat // chunk, t_ctx // chunk

    def tok_spec(t, blk):
        return pl.BlockSpec((t, dh), lambda b, h: (b, blk + h))

    def stat_specs(t, n_chunks):
        return [pl.BlockSpec((t, LANE), lambda b, h: (b, 0)),
                pl.BlockSpec((n_chunks, 3 * COMBOS, chunk), lambda b, h: (b, 0, 0))]

    lat_in = [tok_spec(t_lat, lat_blk[name_]) for name_ in "qkvo"]
    ctx_names = "qkvo" if ctx_out else "kv"
    ctx_in = [tok_spec(t_ctx, ctx_blk[name_]) for name_ in ctx_names]
    in_specs = (lat_in + ctx_in + stat_specs(t_lat, n_lat) + stat_specs(t_ctx, n_ctx)
                + [pl.BlockSpec((1, dh), lambda b, h: (0, h))])
    args = ([p_lat] * 4 + [p_ctx] * len(ctx_names) + list(stats_lat) + list(stats_ctx)
            + [norm_w.reshape(1, heads * dh)])

    out_specs = [pl.BlockSpec((t_lat, dh), lambda b, h: (b, h))]
    out_shape = [jax.ShapeDtypeStruct((n_seq * t_lat, heads * dh), BF16)]
    scratch = [pltpu.VMEM((t_lat, dh), F32), pltpu.VMEM((t_lat, dh), F32)]
    if ctx_out:
        out_specs.append(pl.BlockSpec((t_ctx, dh), lambda b, h: (b, h)))
        out_shape.append(jax.ShapeDtypeStruct((n_seq * t_ctx, heads * dh), BF16))
        scratch += [pltpu.VMEM((t_ctx, dh), F32), pltpu.VMEM((t_ctx, dh), F32)]
    scratch += [pltpu.VMEM((2, dh, 2 * dh), F32), pltpu.VMEM((2, 1, dh), F32)]

    kern = functools.partial(_mlstm_kernel, chunk=chunk, n_lat=n_lat, n_ctx=n_ctx, ctx_out=ctx_out,
                             heads=heads)
    return pl.pallas_call(
        kern,
        grid=(n_seq, heads),
        in_specs=in_specs,
        out_specs=out_specs,
        out_shape=out_shape,
        scratch_shapes=scratch,
        compiler_params=_params(2),
        name=name,
    )(*args)


OUT_PROJ_SUB_ROWS = 256


def _out_proj_kernel(a_ref, m_ref, c_ref, w_ref, x_ref, gpost_ref, g1_ref, gpre_ref, sh2_ref, sc2_ref,
                     x1_ref, h2_ref, *, sub_rows):
    da, dm = a_ref.shape[-1], m_ref.shape[-1]
    tm = x_ref.shape[0]
    gain1 = g1_ref[0] * gpost_ref[...]
    gain2 = gpre_ref[...] * (1.0 + sc2_ref[0])
    for r in range(tm // sub_rows):
        rows = slice(r * sub_rows, (r + 1) * sub_rows)
        y = (jnp.dot(a_ref[rows, :], w_ref[0:da, :], preferred_element_type=F32)
             + jnp.dot(m_ref[rows, :], w_ref[da:da + dm, :], preferred_element_type=F32)
             + jnp.dot(c_ref[rows, :], w_ref[da + dm:, :], preferred_element_type=F32))
        x1 = x_ref[rows, :] + _rms(y) * gain1
        x1_ref[rows, :] = x1
        h2_ref[rows, :] = (_rms(x1) * gain2 + sh2_ref[0]).astype(BF16)


def _out_proj(a, m, cc, w_out, x2d, mod3, mod_row, g_post, g_pre_ffn, *, layer, tm, name):
    n, d = x2d.shape
    da, dm, dc = a.shape[1], m.shape[1], cc.shape[1]
    row = lambda i: (i, 0)
    fixed = lambda i: (0, 0)
    mod_spec = lambda k: pl.BlockSpec((1, 1, d), lambda i: (mod_row(i), 0, k))
    return pl.pallas_call(
        functools.partial(_out_proj_kernel, sub_rows=min(tm, OUT_PROJ_SUB_ROWS)),
        grid=(n // tm,),
        in_specs=[pl.BlockSpec((tm, da), row), pl.BlockSpec((tm, dm), row), pl.BlockSpec((tm, dc), row),
                  pl.BlockSpec((None, da + dm + dc, d), lambda i: (layer, 0, 0),
                               pipeline_mode=pl.Buffered(1)),
                  pl.BlockSpec((tm, d), row),
                  pl.BlockSpec((1, d), fixed),
                  mod_spec(2),
                  pl.BlockSpec((1, d), fixed),
                  mod_spec(3), mod_spec(4)],
        out_specs=[pl.BlockSpec((tm, d), row), pl.BlockSpec((tm, d), row)],
        out_shape=[jax.ShapeDtypeStruct((n, d), F32), jax.ShapeDtypeStruct((n, d), BF16)],
        compiler_params=_params(1),
        name=name,
    )(a, m, cc, w_out, x2d, g_post.reshape(1, d), mod3, g_pre_ffn.reshape(1, d), mod3, mod3)


FFN_TF = 1024


def _ffn_kernel(h_ref, w1_ref, w2_ref, x_ref, gpost_ref, g2_ref, o_ref, acc_ref):
    j = pl.program_id(1)

    @pl.when(j == 0)
    def _():
        acc_ref[...] = jnp.zeros_like(acc_ref)

    u = jnp.maximum(jnp.dot(h_ref[...], w1_ref[...], preferred_element_type=F32), 0.0)
    acc_ref[...] += jnp.dot((u * u).astype(BF16), w2_ref[...], preferred_element_type=F32)

    @pl.when(j == pl.num_programs(1) - 1)
    def _():
        o_ref[...] = x_ref[...] + g2_ref[0] * (_rms(acc_ref[...]) * gpost_ref[...])


def _ffn(h2, w1, w2, x1, mod3, mod_row, g_post, *, layer, tm, tf, name):
    n, d = x1.shape
    dff = w1.shape[2]
    return pl.pallas_call(
        _ffn_kernel,
        grid=(n // tm, dff // tf),
        in_specs=[pl.BlockSpec((tm, d), lambda i, j: (i, 0)),
                  pl.BlockSpec((None, d, tf), lambda i, j: (layer, 0, j)),
                  pl.BlockSpec((None, tf, d), lambda i, j: (layer, j, 0)),
                  pl.BlockSpec((tm, d), lambda i, j: (i, 0)),
                  pl.BlockSpec((1, d), lambda i, j: (0, 0)),
                  pl.BlockSpec((1, 1, d), lambda i, j: (mod_row(i), 0, 5))],
        out_specs=pl.BlockSpec((tm, d), lambda i, j: (i, 0)),
        out_shape=jax.ShapeDtypeStruct((n, d), F32),
        scratch_shapes=[pltpu.VMEM((tm, d), F32)],
        compiler_params=_params(2),
        name=name,
    )(h2, w1, w2, x1, g_post.reshape(1, d), mod3)


def _tile(n, target):
    t = min(n, target)
    while n % t or (t % 8 and t != n):
        t -= 1
    return t


def kernel(x, c, ctx, c_ctx, w_ada, b_ada, g_pre_mix, g_post_mix, g_pre_ffn, g_post_ffn, w_in, b_gates,
           conv_a_w, conv_a_b, ln_a_w, ln_a_b, mlstm_norm_w, conv_c_w, w_out, w_ff1, w_ff2):
    bsz, t_lat, d = x.shape
    t_ctx = ctx.shape[1]
    depth = w_ada.shape[0]
    d_conv = conv_a_w.shape[-1]
    d_short = conv_c_w.shape[-1]
    d_mlstm = mlstm_norm_w.shape[-1]
    heads = b_gates.shape[-1] // 4
    dh = d_mlstm // heads
    n_gates = 4 * heads
    assert dh == LANE and d_conv == d_short and d_conv % LANE == 0
    assert bsz + 1 <= MOD_ROWS and t_lat % GRID_W == 0
    assert w_in.shape[-1] == 2 * d_conv + 4 * d_mlstm + n_gates + 3 * d_short

    tn = d_conv
    chunk = min(256, t_ctx)
    assert d_mlstm % tn == 0 and t_ctx % chunk == 0 and t_lat % chunk == 0
    gate_lo = 2 * d_conv + 4 * d_mlstm
    mt = d_mlstm // tn
    tile = {"a_val": 0, "a_gate": 1, "q": 2, "k": 2 + mt, "v": 2 + 2 * mt, "o": 2 + 3 * mt,
            "s_in": 2 + 4 * mt, "s_b": 3 + 4 * mt, "s_c": 4 + 4 * mt}
    assert tile["a_gate"] == CONV_IN_TILES - 1
    stored = {k_: v_ - CONV_IN_TILES for k_, v_ in tile.items() if v_ >= CONV_IN_TILES}
    lane_blk = {k_: v_ * (tn // dh) for k_, v_ in stored.items()}
    q_scale = float(dh) ** -0.5

    xl = x.reshape(bsz * t_lat, d)
    xc = ctx.reshape(bsz * t_ctx, d)

    cond = jnp.concatenate([c, c_ctx[None, :], jnp.zeros((MOD_ROWS - bsz - 1, d), F32)], axis=0)
    mod3 = _modulation(cond, w_ada, b_ada).reshape(depth * MOD_ROWS, 1, 6 * d)

    tm_in = _tile(t_lat, 512)
    tm_in_ctx = _tile(bsz * t_ctx, 512)
    tm_lat = _tile(t_lat, 512)
    tm_ctx = _tile(bsz * t_ctx, 512)
    tm_out = _tile(t_lat, 512)
    tm_out_ctx = _tile(bsz * t_ctx, 512)

    def gate_cols(g):
        i_f, f_f, i_b, f_b = jnp.split(g, 4, axis=-1)
        pad = jnp.zeros(g.shape[:-1] + (LANE - 2 * heads,), g.dtype)
        return jnp.concatenate([i_f, i_b, pad, f_f, f_b, pad], axis=-1)

    w_head = w_in[:, :, :gate_lo].astype(BF16)
    w_tail = w_in[:, :, gate_lo + n_gates:].astype(BF16)
    kv_cols = (tile["o"] - tile["k"]) * tn
    assert (tile["k"] * tn) % kv_cols == 0
    full_parts = [(w_head, gate_lo, 0), (w_tail, w_tail.shape[2], 0)]
    kv_parts = [(w_head, kv_cols, tile["k"] * tn // kv_cols)]
    w_out_b = w_out.astype(BF16)
    w1_b = w_ff1.astype(BF16)
    w2_b = w_ff2.astype(BF16)

    for layer in range(depth):
        last = layer == depth - 1
        base = layer * MOD_ROWS
        wg = gate_cols(w_in[layer][:, gate_lo:gate_lo + n_gates]).astype(BF16)
        bg = gate_cols(b_gates[layer]).reshape(1, 2 * LANE)
        conv_params = (conv_a_w[layer], conv_a_b[layer], ln_a_w[layer], ln_a_b[layer])

        def lat_row(tm):
            return lambda i: base + (i * tm) // t_lat
        ctx_row = lambda i: base + bsz

        proj = functools.partial(_in_proj, mod3=mod3, g_pre=g_pre_mix[layer], wg=wg, bg=bg, layer=layer,
                                 chunk=chunk, heads=heads, tn=tn, q_scale=q_scale)
        p_lat, xs_lat, rows_lat, a_lat = proj(
            xl, mod_row=lat_row(tm_in), w_parts=full_parts, conv=(GRID_W,) + conv_params, tm=tm_in,
            q_tile_lo=tile["q"], q_tile_hi=tile["k"], name=f"in_proj_lat_{layer}")
        if last:
            p_ctx, xs_ctx, rows_ctx = proj(
                xc, mod_row=ctx_row, w_parts=kv_parts, conv=None, tm=tm_in_ctx,
                q_tile_lo=0, q_tile_hi=0, name=f"in_proj_ctx_{layer}")
            ctx_blk = {"k": 0, "v": lane_blk["v"] - lane_blk["k"]}
        else:
            p_ctx, xs_ctx, rows_ctx, a_ctx = proj(
                xc, mod_row=ctx_row, w_parts=full_parts, conv=(t_ctx,) + conv_params, tm=tm_in_ctx,
                q_tile_lo=tile["q"], q_tile_hi=tile["k"], name=f"in_proj_ctx_{layer}")
            ctx_blk = lane_blk

        mres = _mlstm(p_lat, p_ctx, (xs_lat, rows_lat), (xs_ctx, rows_ctx), mlstm_norm_w[layer],
                      n_seq=bsz, t_lat=t_lat, t_ctx=t_ctx, heads=heads, dh=dh, chunk=chunk,
                      lat_blk=lane_blk, ctx_blk=ctx_blk, ctx_out=not last, name=f"mlstm_{layer}")

        conv_c = functools.partial(_conv_c, conv_w=conv_c_w[layer], n_seq=bsz, in_blk=stored["s_in"],
                                   b_blk=stored["s_b"], c_blk=stored["s_c"])
        c_lat = conv_c(p_lat, seq_len=t_lat, shift=GRID_W, name=f"conv_c_lat_{layer}")
        x1, h2 = _out_proj(a_lat, mres[0], c_lat, w_out_b, xl, mod3, lat_row(tm_out), g_post_mix[layer],
                           g_pre_ffn[layer], layer=layer, tm=tm_out, name=f"out_proj_lat_{layer}")
        xl = _ffn(h2, w1_b, w2_b, x1, mod3, lat_row(tm_lat), g_post_ffn[layer], layer=layer, tm=tm_lat,
                  tf=FFN_TF,
                  name=f"ffn_lat_{layer}")

        if not last:
            c_ctx_mix = conv_c(p_ctx, seq_len=t_ctx, shift=1, name=f"conv_c_ctx_{layer}")
            x1c, h2c = _out_proj(a_ctx, mres[1], c_ctx_mix, w_out_b, xc, mod3, ctx_row, g_post_mix[layer],
                                 g_pre_ffn[layer], layer=layer, tm=tm_out_ctx, name=f"out_proj_ctx_{layer}")
            xc = _ffn(h2c, w1_b, w2_b, x1c, mod3, ctx_row, g_post_ffn[layer], layer=layer, tm=tm_ctx,
                      tf=FFN_TF,
                      name=f"ffn_ctx_{layer}")

    return xl.reshape(bsz, t_lat, d)
```

```python
import functools

import jax
import jax.numpy as jnp
from jax import lax
from jax.experimental import pallas as pl
from jax.experimental.pallas import tpu as pltpu

GRID_W = 64
NORM_EPS = 1e-6
LANE = 128
SUBLANE = 8
VMEM_LIMIT_BYTES = 56 * 1024 * 1024
MOD_ROWS = 16
MASKED_LOG = -1e30
COMBOS = 16
N_SPLIT = 3

F32 = jnp.float32
BF16 = jnp.bfloat16


def _params(n_grid_axes):
    return pltpu.CompilerParams(
        dimension_semantics=("arbitrary",) * n_grid_axes,
        vmem_limit_bytes=VMEM_LIMIT_BYTES)


def _rms(y):
    return y * lax.rsqrt(jnp.mean(y * y, axis=-1, keepdims=True) + NORM_EPS)


def _split3(x):
    hi = x.astype(BF16)
    r1 = x - hi.astype(F32)
    mid = r1.astype(BF16)
    lo = (r1 - mid.astype(F32)).astype(BF16)
    return hi, mid, lo


CAST_BLOCK_BYTES = 4 * 1024 * 1024


def _cast_kernel(w_ref, o_ref):
    o_ref[...] = w_ref[...].astype(o_ref.dtype)


def _to_bf16(w, cols=None, *, name):
    depth, rows, c = w.shape
    cols = c if cols is None else cols
    tr = _tile(rows, max(SUBLANE, CAST_BLOCK_BYTES // (4 * cols)))
    spec = pl.BlockSpec((1, tr, cols), lambda l, i: (l, i, 0))
    return pl.pallas_call(
        _cast_kernel,
        grid=(depth, rows // tr),
        in_specs=[spec],
        out_specs=spec,
        out_shape=jax.ShapeDtypeStruct((depth, rows, cols), BF16),
        compiler_params=_params(2),
        name=name,
    )(w)


def _mod_kernel(s_ref, w_ref, b_ref, o_ref):
    s = s_ref[...]
    a = (s * jax.nn.sigmoid(s)).astype(BF16)
    o_ref[0] = jnp.dot(a, w_ref[0].astype(BF16), preferred_element_type=F32) + b_ref[0]


def _modulation(s, w_ada, b_ada, tn=1024):
    depth, d, n = w_ada.shape
    return pl.pallas_call(
        _mod_kernel,
        grid=(depth, n // tn),
        in_specs=[pl.BlockSpec((MOD_ROWS, d), lambda l, j: (0, 0)),
                  pl.BlockSpec((1, d, tn), lambda l, j: (l, 0, j)),
                  pl.BlockSpec((1, 1, tn), lambda l, j: (l, 0, j))],
        out_specs=pl.BlockSpec((1, MOD_ROWS, tn), lambda l, j: (l, 0, j)),
        out_shape=jax.ShapeDtypeStruct((depth, MOD_ROWS, n), F32),
        compiler_params=_params(2),
        name="adaln_modulation",
    )(s, w_ada, b_ada.reshape(depth, 1, n))


def _in_proj_kernel(*refs, n_parts, chunk, heads, tn, q_tile_lo, q_tile_hi, q_scale, conv_row_len,
                    conv_taps):
    x_ref, g_ref, sh_ref, sc_ref, wg_ref, bg_ref = refs[:6]
    n_in = 6
    conv = conv_row_len is not None
    if conv:
        cw_ref, cb_ref, lnw_ref, lnb_ref = refs[n_in:n_in + 4]
        n_in += 4
    w_refs = refs[n_in:n_in + n_parts]
    outs = refs[n_in + n_parts:]
    if conv:
        p_ref, xs_ref, rows_ref, a_ref, upad_ref, y_ref = outs
    else:
        p_ref, xs_ref, rows_ref = outs
    n_conv = CONV_IN_TILES if conv else 0

    gain = g_ref[...] * (1.0 + sc_ref[0])
    n_sub = x_ref.shape[0] // chunk
    tiles = [(w_ref, j) for w_ref in w_refs for j in range(w_ref.shape[1] // tn)]
    last = len(tiles) - 1
    if conv:
        rows_per_sub = chunk // conv_row_len
        units = [(g, sb) for g in range(rows_per_sub) for sb in range(conv_row_len // CONV_SUB)]
        unit_at = {}
        for pos, unit in zip(CONV_UNIT_TILES, units):
            unit_at.setdefault(min(pos, last), []).append(unit)
        assert len(units) <= len(CONV_UNIT_TILES)
        zeros = jnp.zeros((CONV_A_PAD, upad_ref.shape[-1]), F32)
        for row in range(n_sub * rows_per_sub):
            upad_ref[row, 0:CONV_A_PAD, :] = zeros
            upad_ref[row, CONV_A_PAD + conv_row_len:CONV_A_PAD + conv_row_len + CONV_A_PAD, :] = zeros

    def normed(r):
        return (_rms(x_ref[r * chunk:(r + 1) * chunk, :]) * gain + sh_ref[0]).astype(BF16)

    hb = normed(0)
    for r in range(n_sub):
        rows = slice(r * chunk, (r + 1) * chunk)
        gates = jnp.dot(hb, wg_ref[...], preferred_element_type=F32) + bg_ref[...]
        hb_next = None
        conv_in = []
        for t, (w_ref, j) in enumerate(tiles):
            acc = jnp.dot(hb, w_ref[:, j * tn:(j + 1) * tn], preferred_element_type=F32)
            if t < n_conv:
                conv_in.append(acc)
                if t == n_conv - 1:
                    u = conv_in[0] * jax.nn.sigmoid(conv_in[1])
                    for g in range(rows_per_sub):
                        upad_ref[r * rows_per_sub + g, CONV_A_PAD:CONV_A_PAD + conv_row_len, :] = (
                            u[g * conv_row_len:(g + 1) * conv_row_len, :])
            else:
                if q_tile_lo <= t < q_tile_hi:
                    acc = acc * q_scale
                p_ref[rows, (t - n_conv) * tn:(t - n_conv + 1) * tn] = acc.astype(BF16)
            if t == 0:
                pieces = _forget_pieces(gates)
            if t == min(2, last):
                b = _forget_cumsum(pieces, heads)
            if t == min(4, last):
                xs_ref[rows, :], rows_ref[r] = _gate_stats(gates, b, heads)
            if t == min(6, last) and r + 1 < n_sub:
                hb_next = normed(r + 1)
            if conv:
                for g, sb in unit_at.get(t, []):
                    _conv_a_unit(upad_ref, y_ref.at[(r * rows_per_sub + g) % y_ref.shape[0]],
                                 cw_ref, cb_ref, lnw_ref, lnb_ref, a_ref, r * rows_per_sub + g, sb,
                                 r * chunk + g * conv_row_len + sb * CONV_SUB, conv_taps)
        hb = hb_next


def _in_proj(x2d, mod3, mod_row, g_pre, w_parts, wg, bg, conv, *, layer, chunk, heads, tm, tn, q_tile_lo,
             q_tile_hi, q_scale, name):
    n, d = x2d.shape
    ng = wg.shape[1]
    nw = sum(width for _, width, _ in w_parts) - (CONV_IN_TILES * tn if conv else 0)
    assert tm % chunk == 0 and all(width % tn == 0 for _, width, _ in w_parts)
    fixed = lambda i: (0, 0)

    def w_spec(width, blk):
        return pl.BlockSpec((None, d, width), lambda i: (layer, 0, blk), pipeline_mode=pl.Buffered(1))

    in_specs = [pl.BlockSpec((tm, d), lambda i: (i, 0)),
                pl.BlockSpec((1, d), fixed),
                pl.BlockSpec((1, 1, d), lambda i: (mod_row(i), 0, 0)),
                pl.BlockSpec((1, 1, d), lambda i: (mod_row(i), 0, 1)),
                pl.BlockSpec((d, ng), fixed),
                pl.BlockSpec((1, ng), fixed)]
    args = [x2d, g_pre.reshape(1, d), mod3, mod3, wg, bg]
    out_specs = [pl.BlockSpec((tm, nw), lambda i: (i, 0)),
                 pl.BlockSpec((tm, LANE), lambda i: (i, 0)),
                 pl.BlockSpec((tm // chunk, 3 * COMBOS, chunk), lambda i: (i, 0, 0))]
    out_shape = [jax.ShapeDtypeStruct((n, nw), BF16),
                 jax.ShapeDtypeStruct((n, LANE), BF16),
                 jax.ShapeDtypeStruct((n // chunk, 3 * COMBOS, chunk), F32)]
    scratch = []
    row_len = taps = None
    if conv:
        row_len, conv_w, conv_b, ln_w, ln_b = conv
        taps = conv_w.shape[0]
        assert conv_w.shape[1] == tn and chunk % row_len == 0 and row_len % CONV_SUB == 0
        assert taps // 2 < CONV_A_PAD
        in_specs += [pl.BlockSpec((taps, tn), fixed)] + [pl.BlockSpec((1, tn), fixed)] * 3
        args += [conv_w, conv_b.reshape(1, tn), ln_w.reshape(1, tn), ln_b.reshape(1, tn)]
        out_specs.append(pl.BlockSpec((tm, tn), lambda i: (i, 0)))
        out_shape.append(jax.ShapeDtypeStruct((n, tn), BF16))
        conv_rows = tm // row_len
        scratch = [pltpu.VMEM((conv_rows, row_len + 2 * CONV_A_PAD, tn), F32),
                   pltpu.VMEM((min(conv_rows, 4), CONV_SUB, tn), F32)]
    in_specs += [w_spec(width, blk) for _, width, blk in w_parts]
    args += [arr for arr, _, _ in w_parts]

    kern = functools.partial(_in_proj_kernel, n_parts=len(w_parts), chunk=chunk, heads=heads, tn=tn,
                             q_tile_lo=q_tile_lo, q_tile_hi=q_tile_hi, q_scale=q_scale,
                             conv_row_len=row_len, conv_taps=taps)
    return pl.pallas_call(
        kern,
        grid=(n // tm,),
        in_specs=in_specs,
        out_specs=out_specs,
        out_shape=out_shape,
        scratch_shapes=scratch,
        compiler_params=_params(1),
        name=name,
    )(*args)


CONV_A_PAD = 16
CONV_SUB = 64
CONV_IN_TILES = 2
CONV_UNIT_TILES = (3, 5, 8, 10)


def _conv_a_unit(upad_ref, y_ref, w_ref, b_ref, lnw_ref, lnb_ref, o_ref, row, sb, out_row, taps):
    ch = o_ref.shape[-1]
    half = taps // 2
    for c in range(ch // LANE):
        cs = slice(c * LANE, (c + 1) * LANE)
        acc = None
        for res in range(SUBLANE):
            z = None
            for k in range(taps):
                if (k - half) % SUBLANE != res:
                    continue
                start = CONV_A_PAD + sb * CONV_SUB + (k - half) - res
                term = w_ref[k:k + 1, cs] * upad_ref[row, start:start + CONV_SUB + SUBLANE, cs]
                z = term if z is None else z + term
            if z is None:
                continue
            z = z[res:res + CONV_SUB, :]
            acc = z if acc is None else acc + z
        y_ref[:, cs] = acc + b_ref[:, cs]
    y = y_ref[...]
    mu = jnp.mean(y, axis=-1, keepdims=True)
    dlt = y - mu
    var = jnp.mean(dlt * dlt, axis=-1, keepdims=True)
    z = dlt * lax.rsqrt(var + NORM_EPS) * lnw_ref[...] + lnb_ref[...]
    o_ref[out_row:out_row + CONV_SUB, :] = (z * jax.nn.sigmoid(z)).astype(BF16)


CONV_C_PAD = 64
CONV_C_BLK = 64


def _conv_c_kernel(sin_ref, sb_ref, sc_ref, w_ref, o_ref, upad_ref, *, seq_len, shift):
    ch = o_ref.shape[-1]
    zeros = jnp.zeros((CONV_C_PAD, ch), F32)
    upad_ref[0:CONV_C_PAD, :] = zeros
    upad_ref[CONV_C_PAD + seq_len:CONV_C_PAD + seq_len + CONV_C_PAD, :] = zeros
    for i in range(seq_len // CONV_C_BLK):
        t = slice(i * CONV_C_BLK, (i + 1) * CONV_C_BLK)
        upad_ref[CONV_C_PAD + i * CONV_C_BLK:CONV_C_PAD + (i + 1) * CONV_C_BLK, :] = (
            sc_ref[t, :].astype(F32) * sin_ref[t, :].astype(F32))
    for i in range(seq_len // CONV_C_BLK):
        t0 = CONV_C_PAD + i * CONV_C_BLK
        y = (w_ref[0:1, :] * upad_ref[t0 - shift:t0 - shift + CONV_C_BLK, :]
             + w_ref[1:2, :] * upad_ref[t0:t0 + CONV_C_BLK, :]
             + w_ref[2:3, :] * upad_ref[t0 + shift:t0 + shift + CONV_C_BLK, :])
        t = slice(i * CONV_C_BLK, (i + 1) * CONV_C_BLK)
        o_ref[t, :] = (sb_ref[t, :].astype(F32) * y).astype(BF16)


def _conv_c(p, conv_w, *, n_seq, seq_len, shift, in_blk, b_blk, c_blk, name):
    taps, ch = conv_w.shape
    assert taps == 3 and shift <= CONV_C_PAD
    n = n_seq * seq_len
    kern = functools.partial(_conv_c_kernel, seq_len=seq_len, shift=shift)
    return pl.pallas_call(
        kern,
        grid=(n_seq,),
        in_specs=[pl.BlockSpec((seq_len, ch), lambda b: (b, in_blk)),
                  pl.BlockSpec((seq_len, ch), lambda b: (b, b_blk)),
                  pl.BlockSpec((seq_len, ch), lambda b: (b, c_blk)),
                  pl.BlockSpec((taps, ch), lambda b: (0, 0))],
        out_specs=pl.BlockSpec((seq_len, ch), lambda b: (b, 0)),
        out_shape=jax.ShapeDtypeStruct((n, ch), BF16),
        scratch_shapes=[pltpu.VMEM((seq_len + 2 * CONV_C_PAD, ch), F32)],
        compiler_params=_params(1),
        name=name,
    )(p, p, p, conv_w)


def _log_sigmoid(z):
    return jnp.minimum(z, 0.0) - jnp.log1p(jnp.exp(-jnp.abs(z)))


def _forget_pieces(gates):
    return _split3(_log_sigmoid(gates[:, LANE:2 * LANE]))


def _forget_cumsum(pieces, heads):
    chunk = pieces[0].shape[0]
    jj = lax.broadcasted_iota(jnp.int32, (chunk, chunk), 0)
    ss = lax.broadcasted_iota(jnp.int32, (chunk, chunk), 1)
    tri_p = jnp.where(ss <= jj, 1.0, 0.0).astype(BF16)
    tri_s = jnp.where(ss >= jj, 1.0, 0.0).astype(BF16)
    b_p = jnp.zeros((chunk, LANE), F32)
    b_s = jnp.zeros((chunk, LANE), F32)
    for piece in pieces:
        b_p = b_p + jnp.dot(tri_p, piece, preferred_element_type=F32)
        b_s = b_s + jnp.dot(tri_s, piece, preferred_element_type=F32)
    lane = lax.broadcasted_iota(jnp.int32, (chunk, LANE), 1)
    return jnp.where(lane < heads, b_p, b_s)


def _gate_stats(gates, b, heads):
    chunk = gates.shape[0]
    assert 2 * heads <= COMBOS and 2 * N_SPLIT * COMBOS <= LANE
    lane = lax.broadcasted_iota(jnp.int32, (chunk, LANE), 1)
    row = lax.broadcasted_iota(jnp.int32, (chunk, LANE), 0)
    fwd = lane < heads
    a = gates[:, 0:LANE] - b

    run_max = a
    sh = 1
    while sh < chunk:
        up = pltpu.roll(run_max, sh, axis=0)
        dn = pltpu.roll(run_max, chunk - sh, axis=0)
        cand = jnp.where(fwd, jnp.where(row >= sh, up, MASKED_LOG),
                         jnp.where(row < chunk - sh, dn, MASKED_LOG))
        run_max = jnp.maximum(run_max, cand)
        sh *= 2

    a_rows = a.T[0:COMBOS, :]
    b_rows = b.T[0:COMBOS, :]
    rows = jnp.concatenate(
        [a_rows,
         jnp.broadcast_to(jnp.max(a_rows, axis=1, keepdims=True), (COMBOS, chunk)),
         jnp.broadcast_to(jnp.min(b_rows, axis=1, keepdims=True), (COMBOS, chunk))], axis=0)

    packed = jnp.zeros((chunk, LANE), F32)
    pieces = _split3(run_max) + _split3(b)
    for i, piece in enumerate(pieces):
        val = jnp.where(lane < COMBOS, piece.astype(F32), 0.0)
        packed = packed + (pltpu.roll(val, COMBOS * i, axis=1) if i else val)
    return packed.astype(BF16), rows


SCAN_GROUP = 4


def _twice(r):
    return jnp.concatenate([r, r], axis=1)


def _chunk_front(q, k, v, xs, a_row, a_max, sel, *, need_h):
    chunk, dh = k.shape
    v_ext = jnp.concatenate([v, jnp.ones_like(v)], axis=1)
    k_t = k.T
    front = {"q": q, "v_ext": v_ext, "a_row": a_row, "a_max": a_max}
    if need_h:
        front["stats"] = jnp.dot(xs, sel, preferred_element_type=F32)
        front["raw"] = jnp.dot(q, k_t, preferred_element_type=F32)
    a_max_row = jnp.concatenate([a_max] * (chunk // dh), axis=1)
    kw_t = (k_t.astype(F32) * jnp.exp(a_row - a_max_row)).astype(BF16)
    front["update"] = jnp.dot(kw_t, v_ext, preferred_element_type=F32)
    return front


def _chunk_back(front, b_last, c_ext, m_st, *, reverse, need_h):
    q, v_ext, a_row, a_max = front["q"], front["v_ext"], front["a_row"], front["a_max"]
    chunk, dh = v_ext.shape[0], v_ext.shape[1] // 2
    h = None
    if need_h:
        a_run, b_col = front["stats"][:, :dh], front["stats"][:, dh:]
        raw = front["raw"]
        g_row = jnp.maximum(a_run, m_st)
        jj = lax.broadcasted_iota(jnp.int32, (dh, dh), 0)
        ss = lax.broadcasted_iota(jnp.int32, (dh, dh), 1)
        diag_seen = (ss >= jj) if reverse else (ss <= jj)
        n_blk = chunk // dh
        score_rows = []
        for rb in range(n_blk):
            r_sl = slice(rb * dh, (rb + 1) * dh)
            blocks = []
            for cb in range(n_blk):
                c_sl = slice(cb * dh, (cb + 1) * dh)
                if (cb < rb) if reverse else (cb > rb):
                    blocks.append(jnp.zeros((dh, dh), BF16))
                    continue
                arg = a_row[:, c_sl] - g_row[r_sl, :]
                if cb == rb:
                    arg = jnp.where(diag_seen, arg, MASKED_LOG)
                blocks.append((raw[r_sl, c_sl] * jnp.exp(arg)).astype(BF16))
            score_rows.append(jnp.concatenate(blocks, axis=1))
        scores = jnp.concatenate(score_rows, axis=0)
        q_inter = (q.astype(F32) * jnp.exp(m_st - g_row)).astype(BF16)
        both = jnp.dot(jnp.concatenate([scores, q_inter], axis=1),
                       jnp.concatenate([v_ext, c_ext.astype(BF16)], axis=0),
                       preferred_element_type=F32)
        floor = jnp.exp(-(b_col + g_row))
        h = both[:, :dh] / jnp.maximum(jnp.abs(both[:, dh:]), floor)

    g_last = jnp.maximum(a_max, m_st)
    c_new = (_twice(jnp.exp(m_st - g_last)) * c_ext
             + _twice(jnp.exp(a_max - g_last)) * front["update"])
    return h, c_new, b_last + g_last


def _mlstm_kernel(*refs, chunk, n_lat, n_ctx, ctx_out, heads):
    if ctx_out:
        (ql, kl, vl, ol, qc, kc, vc, oc, xsl, rwl, xsc, rwc, nw,
         out_l, out_c, hf_l, hb_l, hf_c, hb_c, c_ref, m_ref) = refs
    else:
        (ql, kl, vl, ol, kc, vc, xsl, rwl, xsc, rwc, nw,
         out_l, hf_l, hb_l, c_ref, m_ref) = refs
        qc = oc = out_c = hf_c = hb_c = None
    dh = nw.shape[-1]
    head = pl.program_id(1)

    c_ref[...] = jnp.zeros_like(c_ref)
    m_ref[...] = jnp.zeros_like(m_ref)

    kk = lax.broadcasted_iota(jnp.int32, (LANE, 2 * dh), 0)
    nn = lax.broadcasted_iota(jnp.int32, (LANE, 2 * dh), 1)
    piece = jnp.right_shift(kk, COMBOS.bit_length() - 1)
    first_piece = jnp.where(nn < dh, 0, N_SPLIT)
    target = (piece >= first_piece) & (piece < first_piece + N_SPLIT)
    pair_lane = jnp.bitwise_and(kk, COMBOS - 1)
    sels = [jnp.where(target & (pair_lane == direction * heads + head), 1.0, 0.0).astype(BF16)
            for direction in (0, 1)]

    def scan(q_ref, k_ref, v_ref, xs_ref, rw_ref, hf_ref, hb_ref, n_chunks, need_h):
        group = SCAN_GROUP if n_chunks % SCAN_GROUP == 0 else 1

        def front(i, direction):
            rows = pl.ds(pl.multiple_of(i * chunk, chunk), chunk)
            pair = direction * heads + head
            a_row = rw_ref[i, pl.ds(pair, 1), :]
            a_max = rw_ref[i, pl.ds(COMBOS + pair, 1), :][:, 0:dh]
            return _chunk_front(q_ref[rows, :] if need_h else None, k_ref[rows, :], v_ref[rows, :],
                                xs_ref[rows, :], a_row, a_max, sels[direction], need_h=need_h)

        def back(i, direction, fr):
            pair = direction * heads + head
            b_last = rw_ref[i, pl.ds(2 * COMBOS + pair, 1), :][:, 0:dh]
            h, c_new, m_new = _chunk_back(fr, b_last, c_ref[direction], m_ref[direction],
                                          reverse=direction == 1, need_h=need_h)
            c_ref[direction] = c_new
            m_ref[direction] = m_new
            if need_h:
                rows = pl.ds(pl.multiple_of(i * chunk, chunk), chunk)
                (hf_ref, hb_ref)[direction][rows, :] = h

        def body(gi, carry):
            steps = []
            for u in range(group):
                i = gi * group + u
                steps += [(i, 0), (n_chunks - 1 - i, 1)]
            fr = front(*steps[0])
            for s_idx, (i, direction) in enumerate(steps):
                fr_next = front(*steps[s_idx + 1]) if s_idx + 1 < len(steps) else None
                back(i, direction, fr)
                fr = fr_next
            return carry

        lax.fori_loop(0, n_chunks // group, body, 0)

    def finish(hf_ref, hb_ref, o_ref, out_ref, n_chunks):
        mean_w = jnp.full((dh, dh), 1.0 / dh, BF16)

        def body(i, carry):
            rows = pl.ds(pl.multiple_of(i * chunk, chunk), chunk)
            h = hf_ref[rows, :] + hb_ref[rows, :]
            mu = jnp.dot(h.astype(BF16), mean_w, preferred_element_type=F32)
            dlt = h - mu
            var = jnp.dot((dlt * dlt).astype(BF16), mean_w, preferred_element_type=F32)
            y = dlt * lax.rsqrt(var + NORM_EPS) * nw[...]
            out_ref[rows, :] = (y * jax.nn.sigmoid(o_ref[rows, :].astype(F32))).astype(BF16)
            return carry

        lax.fori_loop(0, n_chunks, body, 0, unroll=True)

    scan(qc, kc, vc, xsc, rwc, hf_c, hb_c, n_ctx, ctx_out)
    scan(ql, kl, vl, xsl, rwl, hf_l, hb_l, n_lat, True)
    finish(hf_l, hb_l, ol, out_l, n_lat)
    if ctx_out:
        finish(hf_c, hb_c, oc, out_c, n_ctx)


def _mlstm(p_lat, p_ctx, stats_lat, stats_ctx, norm_w, *, n_seq, t_lat, t_ctx, heads, dh, chunk,
           lat_blk, ctx_blk, ctx_out, name):
    assert chunk % dh == 0
    n_lat, n_ctx = t_lat // chunk, t_ctx // chunk

    def tok_spec(t, blk):
        return pl.BlockSpec((t, dh), lambda b, h: (b, blk + h))

    def stat_specs(t, n_chunks):
        return [pl.BlockSpec((t, LANE), lambda b, h: (b, 0)),
                pl.BlockSpec((n_chunks, 3 * COMBOS, chunk), lambda b, h: (b, 0, 0))]

    lat_in = [tok_spec(t_lat, lat_blk[name_]) for name_ in "qkvo"]
    ctx_names = "qkvo" if ctx_out else "kv"
    ctx_in = [tok_spec(t_ctx, ctx_blk[name_]) for name_ in ctx_names]
    in_specs = (lat_in + ctx_in + stat_specs(t_lat, n_lat) + stat_specs(t_ctx, n_ctx)
                + [pl.BlockSpec((1, dh), lambda b, h: (0, h))])
    args = ([p_lat] * 4 + [p_ctx] * len(ctx_names) + list(stats_lat) + list(stats_ctx)
            + [norm_w.reshape(1, heads * dh)])

    out_specs = [pl.BlockSpec((t_lat, dh), lambda b, h: (b, h))]
    out_shape = [jax.ShapeDtypeStruct((n_seq * t_lat, heads * dh), BF16)]
    scratch = [pltpu.VMEM((t_lat, dh), F32), pltpu.VMEM((t_lat, dh), F32)]
    if ctx_out:
        out_specs.append(pl.BlockSpec((t_ctx, dh), lambda b, h: (b, h)))
        out_shape.append(jax.ShapeDtypeStruct((n_seq * t_ctx, heads * dh), BF16))
        scratch += [pltpu.VMEM((t_ctx, dh), F32), pltpu.VMEM((t_ctx, dh), F32)]
    scratch += [pltpu.VMEM((2, dh, 2 * dh), F32), pltpu.VMEM((2, 1, dh), F32)]

    kern = functools.partial(_mlstm_kernel, chunk=chunk, n_lat=n_lat, n_ctx=n_ctx, ctx_out=ctx_out,
                             heads=heads)
    return pl.pallas_call(
        kern,
        grid=(n_seq, heads),
        in_specs=in_specs,
        out_specs=out_specs,
        out_shape=out_shape,
        scratch_shapes=scratch,
        compiler_params=_params(2),
        name=name,
    )(*args)


OUT_PROJ_SUB_ROWS = 256


def _out_proj_kernel(a_ref, m_ref, c_ref, w_ref, x_ref, gpost_ref, g1_ref, gpre_ref, sh2_ref, sc2_ref,
                     x1_ref, h2_ref, *, sub_rows):
    da, dm = a_ref.shape[-1], m_ref.shape[-1]
    tm = x_ref.shape[0]
    gain1 = g1_ref[0] * gpost_ref[...]
    gain2 = gpre_ref[...] * (1.0 + sc2_ref[0])
    for r in range(tm // sub_rows):
        rows = slice(r * sub_rows, (r + 1) * sub_rows)
        y = (jnp.dot(a_ref[rows, :], w_ref[0:da, :], preferred_element_type=F32)
             + jnp.dot(m_ref[rows, :], w_ref[da:da + dm, :], preferred_element_type=F32)
             + jnp.dot(c_ref[rows, :], w_ref[da + dm:, :], preferred_element_type=F32))
        x1 = x_ref[rows, :] + _rms(y) * gain1
        x1_ref[rows, :] = x1
        h2_ref[rows, :] = (_rms(x1) * gain2 + sh2_ref[0]).astype(BF16)


def _out_proj(a, m, cc, w_out, x2d, mod3, mod_row, g_post, g_pre_ffn, *, layer, tm, name):
    n, d = x2d.shape
    da, dm, dc = a.shape[1], m.shape[1], cc.shape[1]
    row = lambda i: (i, 0)
    fixed = lambda i: (0, 0)
    mod_spec = lambda k: pl.BlockSpec((1, 1, d), lambda i: (mod_row(i), 0, k))
    return pl.pallas_call(
        functools.partial(_out_proj_kernel, sub_rows=min(tm, OUT_PROJ_SUB_ROWS)),
        grid=(n // tm,),
        in_specs=[pl.BlockSpec((tm, da), row), pl.BlockSpec((tm, dm), row), pl.BlockSpec((tm, dc), row),
                  pl.BlockSpec((None, da + dm + dc, d), lambda i: (layer, 0, 0),
                               pipeline_mode=pl.Buffered(1)),
                  pl.BlockSpec((tm, d), row),
                  pl.BlockSpec((1, d), fixed),
                  mod_spec(2),
                  pl.BlockSpec((1, d), fixed),
                  mod_spec(3), mod_spec(4)],
        out_specs=[pl.BlockSpec((tm, d), row), pl.BlockSpec((tm, d), row)],
        out_shape=[jax.ShapeDtypeStruct((n, d), F32), jax.ShapeDtypeStruct((n, d), BF16)],
        compiler_params=_params(1),
        name=name,
    )(a, m, cc, w_out, x2d, g_post.reshape(1, d), mod3, g_pre_ffn.reshape(1, d), mod3, mod3)


FFN_TF = 1024


FFN_SUB_ROWS = 256


def _ffn_kernel(h_ref, w1_ref, w2_ref, x_ref, gpost_ref, g2_ref, o_ref, acc_ref, *, sub_rows):
    j = pl.program_id(1)
    last = pl.num_programs(1) - 1

    @pl.when(j == 0)
    def _():
        acc_ref[...] = jnp.zeros_like(acc_ref)

    def hidden_step(rows):
        u = jnp.maximum(jnp.dot(h_ref[rows, :], w1_ref[...], preferred_element_type=F32), 0.0)
        return jnp.dot((u * u).astype(BF16), w2_ref[...], preferred_element_type=F32)

    @pl.when(j < last)
    def _():
        acc_ref[...] += hidden_step(slice(None))

    @pl.when(j == last)
    def _():
        gain = g2_ref[0] * gpost_ref[...]
        for r in range(h_ref.shape[0] // sub_rows):
            rows = slice(r * sub_rows, (r + 1) * sub_rows)
            y = acc_ref[rows, :] + hidden_step(rows)
            o_ref[rows, :] = x_ref[rows, :] + _rms(y) * gain


def _ffn(h2, w1, w2, x1, mod3, mod_row, g_post, *, layer, tm, tf, name):
    n, d = x1.shape
    dff = w1.shape[2]
    return pl.pallas_call(
        functools.partial(_ffn_kernel, sub_rows=min(tm, FFN_SUB_ROWS)),
        grid=(n // tm, dff // tf),
        in_specs=[pl.BlockSpec((tm, d), lambda i, j: (i, 0)),
                  pl.BlockSpec((None, d, tf), lambda i, j: (layer, 0, j)),
                  pl.BlockSpec((None, tf, d), lambda i, j: (layer, j, 0)),
                  pl.BlockSpec((tm, d), lambda i, j: (i, 0)),
                  pl.BlockSpec((1, d), lambda i, j: (0, 0)),
                  pl.BlockSpec((1, 1, d), lambda i, j: (mod_row(i), 0, 5))],
        out_specs=pl.BlockSpec((tm, d), lambda i, j: (i, 0)),
        out_shape=jax.ShapeDtypeStruct((n, d), F32),
        scratch_shapes=[pltpu.VMEM((tm, d), F32)],
        compiler_params=_params(2),
        name=name,
    )(h2, w1, w2, x1, g_post.reshape(1, d), mod3)


def _tile(n, target):
    t = min(n, target)
    while n % t or (t % 8 and t != n):
        t -= 1
    return t


def kernel(x, c, ctx, c_ctx, w_ada, b_ada, g_pre_mix, g_post_mix, g_pre_ffn, g_post_ffn, w_in, b_gates,
           conv_a_w, conv_a_b, ln_a_w, ln_a_b, mlstm_norm_w, conv_c_w, w_out, w_ff1, w_ff2):
    bsz, t_lat, d = x.shape
    t_ctx = ctx.shape[1]
    depth = w_ada.shape[0]
    d_conv = conv_a_w.shape[-1]
    d_short = conv_c_w.shape[-1]
    d_mlstm = mlstm_norm_w.shape[-1]
    heads = b_gates.shape[-1] // 4
    dh = d_mlstm // heads
    n_gates = 4 * heads
    assert dh == LANE and d_conv == d_short and d_conv % LANE == 0
    assert bsz + 1 <= MOD_ROWS and t_lat % GRID_W == 0
    assert w_in.shape[-1] == 2 * d_conv + 4 * d_mlstm + n_gates + 3 * d_short

    tn = d_conv
    chunk = min(256, t_ctx)
    assert d_mlstm % tn == 0 and t_ctx % chunk == 0 and t_lat % chunk == 0
    gate_lo = 2 * d_conv + 4 * d_mlstm
    mt = d_mlstm // tn
    tile = {"a_val": 0, "a_gate": 1, "q": 2, "k": 2 + mt, "v": 2 + 2 * mt, "o": 2 + 3 * mt,
            "s_in": 2 + 4 * mt, "s_b": 3 + 4 * mt, "s_c": 4 + 4 * mt}
    assert tile["a_gate"] == CONV_IN_TILES - 1
    stored = {k_: v_ - CONV_IN_TILES for k_, v_ in tile.items() if v_ >= CONV_IN_TILES}
    lane_blk = {k_: v_ * (tn // dh) for k_, v_ in stored.items()}
    q_scale = float(dh) ** -0.5

    xl = x.reshape(bsz * t_lat, d)
    xc = ctx.reshape(bsz * t_ctx, d)

    cond = jnp.concatenate([c, c_ctx[None, :], jnp.zeros((MOD_ROWS - bsz - 1, d), F32)], axis=0)
    mod3 = _modulation(cond, w_ada, b_ada).reshape(depth * MOD_ROWS, 1, 6 * d)

    tm_in = _tile(t_lat, 512)
    tm_in_ctx = _tile(bsz * t_ctx, 512)
    tm_lat = _tile(t_lat, 512)
    tm_ctx = _tile(bsz * t_ctx, 512)
    tm_out = _tile(t_lat, 512)
    tm_out_ctx = _tile(bsz * t_ctx, 512)

    def gate_cols(g):
        i_f, f_f, i_b, f_b = jnp.split(g, 4, axis=-1)
        pad = jnp.zeros(g.shape[:-1] + (LANE - 2 * heads,), g.dtype)
        return jnp.concatenate([i_f, i_b, pad, f_f, f_b, pad], axis=-1)

    w_head = _to_bf16(w_in, gate_lo, name="cast_w_in_head")
    w_tail = _to_bf16(w_in[:, :, gate_lo + n_gates:], name="cast_w_in_tail")
    kv_cols = (tile["o"] - tile["k"]) * tn
    assert (tile["k"] * tn) % kv_cols == 0
    full_parts = [(w_head, gate_lo, 0), (w_tail, w_tail.shape[2], 0)]
    kv_parts = [(w_head, kv_cols, tile["k"] * tn // kv_cols)]
    w_out_b = _to_bf16(w_out, name="cast_w_out")
    w1_b = _to_bf16(w_ff1, name="cast_w_ff1")
    w2_b = _to_bf16(w_ff2, name="cast_w_ff2")

    for layer in range(depth):
        last = layer == depth - 1
        base = layer * MOD_ROWS
        wg = gate_cols(w_in[layer][:, gate_lo:gate_lo + n_gates]).astype(BF16)
        bg = gate_cols(b_gates[layer]).reshape(1, 2 * LANE)
        conv_params = (conv_a_w[layer], conv_a_b[layer], ln_a_w[layer], ln_a_b[layer])

        def lat_row(tm):
            return lambda i: base + (i * tm) // t_lat
        ctx_row = lambda i: base + bsz

        proj = functools.partial(_in_proj, mod3=mod3, g_pre=g_pre_mix[layer], wg=wg, bg=bg, layer=layer,
                                 chunk=chunk, heads=heads, tn=tn, q_scale=q_scale)
        p_lat, xs_lat, rows_lat, a_lat = proj(
            xl, mod_row=lat_row(tm_in), w_parts=full_parts, conv=(GRID_W,) + conv_params, tm=tm_in,
            q_tile_lo=tile["q"], q_tile_hi=tile["k"], name=f"in_proj_lat_{layer}")
        if last:
            p_ctx, xs_ctx, rows_ctx = proj(
                xc, mod_row=ctx_row, w_parts=kv_parts, conv=None, tm=tm_in_ctx,
                q_tile_lo=0, q_tile_hi=0, name=f"in_proj_ctx_{layer}")
            ctx_blk = {"k": 0, "v": lane_blk["v"] - lane_blk["k"]}
        else:
            p_ctx, xs_ctx, rows_ctx, a_ctx = proj(
                xc, mod_row=ctx_row, w_parts=full_parts, conv=(t_ctx,) + conv_params, tm=tm_in_ctx,
                q_tile_lo=tile["q"], q_tile_hi=tile["k"], name=f"in_proj_ctx_{layer}")
            ctx_blk = lane_blk

        mres = _mlstm(p_lat, p_ctx, (xs_lat, rows_lat), (xs_ctx, rows_ctx), mlstm_norm_w[layer],
                      n_seq=bsz, t_lat=t_lat, t_ctx=t_ctx, heads=heads, dh=dh, chunk=chunk,
                      lat_blk=lane_blk, ctx_blk=ctx_blk, ctx_out=not last, name=f"mlstm_{layer}")

        conv_c = functools.partial(_conv_c, conv_w=conv_c_w[layer], n_seq=bsz, in_blk=stored["s_in"],
                                   b_blk=stored["s_b"], c_blk=stored["s_c"])
        c_lat = conv_c(p_lat, seq_len=t_lat, shift=GRID_W, name=f"conv_c_lat_{layer}")
        x1, h2 = _out_proj(a_lat, mres[0], c_lat, w_out_b, xl, mod3, lat_row(tm_out), g_post_mix[layer],
                           g_pre_ffn[layer], layer=layer, tm=tm_out, name=f"out_proj_lat_{layer}")
        xl = _ffn(h2, w1_b, w2_b, x1, mod3, lat_row(tm_lat), g_post_ffn[layer], layer=layer, tm=tm_lat,
                  tf=FFN_TF,
                  name=f"ffn_lat_{layer}")

        if not last:
            c_ctx_mix = conv_c(p_ctx, seq_len=t_ctx, shift=1, name=f"conv_c_ctx_{layer}")
            x1c, h2c = _out_proj(a_ctx, mres[1], c_ctx_mix, w_out_b, xc, mod3, ctx_row, g_post_mix[layer],
                                 g_pre_ffn[layer], layer=layer, tm=tm_out_ctx, name=f"out_proj_ctx_{layer}")
            xc = _ffn(h2c, w1_b, w2_b, x1c, mod3, ctx_row, g_post_ffn[layer], layer=layer, tm=tm_ctx,
                      tf=FFN_TF,
                      name=f"ffn_ctx_{layer}")

    return xl.reshape(bsz, t_lat, d)
```

```python
import functools

import jax
import jax.numpy as jnp
from jax import lax
from jax.experimental import pallas as pl
from jax.experimental.pallas import tpu as pltpu

GRID_W = 64
NORM_EPS = 1e-6
LANE = 128
SUBLANE = 8
VMEM_LIMIT_BYTES = 56 * 1024 * 1024
MOD_ROWS = 16
MASKED_LOG = -1e30
COMBOS = 16
N_SPLIT = 3

F32 = jnp.float32
BF16 = jnp.bfloat16


def _params(n_grid_axes):
    return pltpu.CompilerParams(
        dimension_semantics=("arbitrary",) * n_grid_axes,
        vmem_limit_bytes=VMEM_LIMIT_BYTES)


def _rms(y):
    return y * lax.rsqrt(jnp.mean(y * y, axis=-1, keepdims=True) + NORM_EPS)


def _split3(x):
    hi = x.astype(BF16)
    r1 = x - hi.astype(F32)
    mid = r1.astype(BF16)
    lo = (r1 - mid.astype(F32)).astype(BF16)
    return hi, mid, lo


CAST_BLOCK_BYTES = 4 * 1024 * 1024


def _cast_kernel(w_ref, o_ref):
    o_ref[...] = w_ref[...].astype(o_ref.dtype)


def _to_bf16(w, cols=None, *, name):
    depth, rows, c = w.shape
    cols = c if cols is None else cols
    tr = _tile(rows, max(SUBLANE, CAST_BLOCK_BYTES // (4 * cols)))
    spec = pl.BlockSpec((1, tr, cols), lambda l, i: (l, i, 0))
    return pl.pallas_call(
        _cast_kernel,
        grid=(depth, rows // tr),
        in_specs=[spec],
        out_specs=spec,
        out_shape=jax.ShapeDtypeStruct((depth, rows, cols), BF16),
        compiler_params=_params(2),
        name=name,
    )(w)


def _split_w_in_kernel(w_ref, place_ref, head_ref, gate_ref, tail_ref, *, gate_lo, n_gates):
    x = w_ref[0]
    head_ref[0] = x[:, :gate_lo].astype(BF16)
    gates = x[:, gate_lo:gate_lo + n_gates].astype(BF16)
    gate_ref[0] = jnp.dot(gates, place_ref[...], preferred_element_type=F32).astype(BF16)
    tail_ref[0] = x[:, gate_lo + n_gates:].astype(BF16)


def _split_w_in(w_in, gate_lo, heads):
    depth, rows, c = w_in.shape
    n_gates = 4 * heads
    n_tail = c - gate_lo - n_gates
    src = jnp.arange(n_gates)
    kind, head = src // heads, src % heads
    dst = jnp.where(kind % 2 == 0, 0, LANE) + (kind // 2) * heads + head
    place = (dst[:, None] == jnp.arange(2 * LANE)[None, :]).astype(BF16)
    tr = _tile(rows, max(SUBLANE, CAST_BLOCK_BYTES // (4 * c)))
    blk = lambda cols: pl.BlockSpec((1, tr, cols), lambda l, i: (l, i, 0))
    return pl.pallas_call(
        functools.partial(_split_w_in_kernel, gate_lo=gate_lo, n_gates=n_gates),
        grid=(depth, rows // tr),
        in_specs=[blk(c), pl.BlockSpec((n_gates, 2 * LANE), lambda l, i: (0, 0))],
        out_specs=[blk(gate_lo), blk(2 * LANE), blk(n_tail)],
        out_shape=[jax.ShapeDtypeStruct((depth, rows, gate_lo), BF16),
                   jax.ShapeDtypeStruct((depth, rows, 2 * LANE), BF16),
                   jax.ShapeDtypeStruct((depth, rows, n_tail), BF16)],
        compiler_params=_params(2),
        name="cast_w_in",
    )(w_in, place)


def _mod_kernel(s_ref, w_ref, b_ref, o_ref):
    s = s_ref[...]
    a = (s * jax.nn.sigmoid(s)).astype(BF16)
    o_ref[0] = jnp.dot(a, w_ref[0].astype(BF16), preferred_element_type=F32) + b_ref[0]


def _modulation(s, w_ada, b_ada, tn=1024):
    depth, d, n = w_ada.shape
    return pl.pallas_call(
        _mod_kernel,
        grid=(depth, n // tn),
        in_specs=[pl.BlockSpec((MOD_ROWS, d), lambda l, j: (0, 0)),
                  pl.BlockSpec((1, d, tn), lambda l, j: (l, 0, j)),
                  pl.BlockSpec((1, 1, tn), lambda l, j: (l, 0, j))],
        out_specs=pl.BlockSpec((1, MOD_ROWS, tn), lambda l, j: (l, 0, j)),
        out_shape=jax.ShapeDtypeStruct((depth, MOD_ROWS, n), F32),
        compiler_params=_params(2),
        name="adaln_modulation",
    )(s, w_ada, b_ada.reshape(depth, 1, n))


def _in_proj_kernel(*refs, n_parts, chunk, heads, tn, q_tile_lo, q_tile_hi, q_scale, conv_row_len,
                    conv_taps):
    x_ref, g_ref, sh_ref, sc_ref, wg_ref, bg_ref = refs[:6]
    n_in = 6
    conv = conv_row_len is not None
    if conv:
        cw_ref, cb_ref, lnw_ref, lnb_ref = refs[n_in:n_in + 4]
        n_in += 4
    w_refs = refs[n_in:n_in + n_parts]
    outs = refs[n_in + n_parts:]
    if conv:
        p_ref, xs_ref, rows_ref, a_ref, upad_ref, y_ref = outs
    else:
        p_ref, xs_ref, rows_ref = outs
    n_conv = CONV_IN_TILES if conv else 0

    gain = g_ref[...] * (1.0 + sc_ref[0])
    n_sub = x_ref.shape[0] // chunk
    tiles = [(w_ref, j) for w_ref in w_refs for j in range(w_ref.shape[1] // tn)]
    last = len(tiles) - 1
    if conv:
        rows_per_sub = chunk // conv_row_len
        items = []
        for g in range(rows_per_sub):
            for sb in range(conv_row_len // CONV_SUB):
                items += [(g, sb, c) for c in range(upad_ref.shape[-1] // LANE)] + [(g, sb, None)]
        first = min(n_conv, last)
        per_tile = -(-len(items) // (last - first + 1))
        items_at = {first + i: items[i * per_tile:(i + 1) * per_tile] for i in range(last - first + 1)}
        zeros = jnp.zeros((CONV_A_PAD, upad_ref.shape[-1]), F32)
        for row in range(n_sub * rows_per_sub):
            upad_ref[row, 0:CONV_A_PAD, :] = zeros
            upad_ref[row, CONV_A_PAD + conv_row_len:CONV_A_PAD + conv_row_len + CONV_A_PAD, :] = zeros

    def normed(r):
        return (_rms(x_ref[r * chunk:(r + 1) * chunk, :]) * gain + sh_ref[0]).astype(BF16)

    hb = normed(0)
    for r in range(n_sub):
        rows = slice(r * chunk, (r + 1) * chunk)
        gates = jnp.dot(hb, wg_ref[...], preferred_element_type=F32) + bg_ref[...]
        hb_next = None
        conv_in = []
        for t, (w_ref, j) in enumerate(tiles):
            acc = jnp.dot(hb, w_ref[:, j * tn:(j + 1) * tn], preferred_element_type=F32)
            if t < n_conv:
                conv_in.append(acc)
                if t == n_conv - 1:
                    u = conv_in[0] * jax.nn.sigmoid(conv_in[1])
                    for g in range(rows_per_sub):
                        upad_ref[r * rows_per_sub + g, CONV_A_PAD:CONV_A_PAD + conv_row_len, :] = (
                            u[g * conv_row_len:(g + 1) * conv_row_len, :])
            else:
                if q_tile_lo <= t < q_tile_hi:
                    acc = acc * q_scale
                p_ref[rows, (t - n_conv) * tn:(t - n_conv + 1) * tn] = acc.astype(BF16)
            if t == 0:
                pieces = _forget_pieces(gates)
            if t == min(2, last):
                b = _forget_cumsum(pieces, heads)
            if t == min(4, last):
                xs_ref[rows, :], rows_ref[r] = _gate_stats(gates, b, heads)
            if t == min(6, last) and r + 1 < n_sub:
                hb_next = normed(r + 1)
            if conv:
                for g, sb, c in items_at.get(t, []):
                    row = r * rows_per_sub + g
                    y_blk = y_ref.at[row % y_ref.shape[0]]
                    if c is None:
                        _conv_a_finish(y_blk, lnw_ref, lnb_ref, a_ref,
                                       r * chunk + g * conv_row_len + sb * CONV_SUB)
                    else:
                        _conv_a_piece(upad_ref, y_blk, cw_ref, cb_ref, row, sb, c, conv_taps)
        hb = hb_next


def _in_proj(x2d, mod3, mod_row, g_pre, w_parts, wg, bg, conv, *, layer, chunk, heads, tm, tn, q_tile_lo,
             q_tile_hi, q_scale, name):
    n, d = x2d.shape
    ng = wg.shape[2]
    nw = sum(width for _, width, _ in w_parts) - (CONV_IN_TILES * tn if conv else 0)
    assert tm % chunk == 0 and all(width % tn == 0 for _, width, _ in w_parts)
    fixed = lambda i: (0, 0)

    def w_spec(width, blk):
        return pl.BlockSpec((None, d, width), lambda i: (layer, 0, blk), pipeline_mode=pl.Buffered(1))

    in_specs = [pl.BlockSpec((tm, d), lambda i: (i, 0)),
                pl.BlockSpec((1, d), fixed),
                pl.BlockSpec((1, 1, d), lambda i: (mod_row(i), 0, 0)),
                pl.BlockSpec((1, 1, d), lambda i: (mod_row(i), 0, 1)),
                pl.BlockSpec((None, d, ng), lambda i: (layer, 0, 0)),
                pl.BlockSpec((1, ng), fixed)]
    args = [x2d, g_pre.reshape(1, d), mod3, mod3, wg, bg]
    out_specs = [pl.BlockSpec((tm, nw), lambda i: (i, 0)),
                 pl.BlockSpec((tm, LANE), lambda i: (i, 0)),
                 pl.BlockSpec((tm // chunk, 3 * COMBOS, chunk), lambda i: (i, 0, 0))]
    out_shape = [jax.ShapeDtypeStruct((n, nw), BF16),
                 jax.ShapeDtypeStruct((n, LANE), BF16),
                 jax.ShapeDtypeStruct((n // chunk, 3 * COMBOS, chunk), F32)]
    scratch = []
    row_len = taps = None
    if conv:
        row_len, conv_w, conv_b, ln_w, ln_b = conv
        taps = conv_w.shape[0]
        assert conv_w.shape[1] == tn and chunk % row_len == 0 and row_len % CONV_SUB == 0
        assert taps // 2 < CONV_A_PAD
        in_specs += [pl.BlockSpec((taps, tn), fixed)] + [pl.BlockSpec((1, tn), fixed)] * 3
        args += [conv_w, conv_b.reshape(1, tn), ln_w.reshape(1, tn), ln_b.reshape(1, tn)]
        out_specs.append(pl.BlockSpec((tm, tn), lambda i: (i, 0)))
        out_shape.append(jax.ShapeDtypeStruct((n, tn), BF16))
        conv_rows = tm // row_len
        scratch = [pltpu.VMEM((conv_rows, row_len + 2 * CONV_A_PAD, tn), F32),
                   pltpu.VMEM((min(conv_rows, 4), CONV_SUB, tn), F32)]
    in_specs += [w_spec(width, blk) for _, width, blk in w_parts]
    args += [arr for arr, _, _ in w_parts]

    kern = functools.partial(_in_proj_kernel, n_parts=len(w_parts), chunk=chunk, heads=heads, tn=tn,
                             q_tile_lo=q_tile_lo, q_tile_hi=q_tile_hi, q_scale=q_scale,
                             conv_row_len=row_len, conv_taps=taps)
    return pl.pallas_call(
        kern,
        grid=(n // tm,),
        in_specs=in_specs,
        out_specs=out_specs,
        out_shape=out_shape,
        scratch_shapes=scratch,
        compiler_params=_params(1),
        name=name,
    )(*args)


CONV_A_PAD = 16
CONV_SUB = 64
CONV_IN_TILES = 2


def _conv_a_piece(upad_ref, y_ref, w_ref, b_ref, row, sb, c, taps):
    half = taps // 2
    cs = slice(c * LANE, (c + 1) * LANE)
    acc = None
    for res in range(SUBLANE):
        z = None
        for k in range(taps):
            if (k - half) % SUBLANE != res:
                continue
            start = CONV_A_PAD + sb * CONV_SUB + (k - half) - res
            term = w_ref[k:k + 1, cs] * upad_ref[row, start:start + CONV_SUB + SUBLANE, cs]
            z = term if z is None else z + term
        if z is None:
            continue
        z = z[res:res + CONV_SUB, :]
        acc = z if acc is None else acc + z
    y_ref[:, cs] = acc + b_ref[:, cs]


def _conv_a_finish(y_ref, lnw_ref, lnb_ref, o_ref, out_row):
    y = y_ref[...]
    mu = jnp.mean(y, axis=-1, keepdims=True)
    dlt = y - mu
    var = jnp.mean(dlt * dlt, axis=-1, keepdims=True)
    z = dlt * lax.rsqrt(var + NORM_EPS) * lnw_ref[...] + lnb_ref[...]
    o_ref[out_row:out_row + CONV_SUB, :] = (z * jax.nn.sigmoid(z)).astype(BF16)


CONV_C_PAD = 64
CONV_C_BLK = 64


def _conv_c_kernel(sin_ref, sb_ref, sc_ref, w_ref, o_ref, upad_ref, *, seq_len, shift):
    ch = o_ref.shape[-1]
    zeros = jnp.zeros((CONV_C_PAD, ch), F32)
    upad_ref[0:CONV_C_PAD, :] = zeros
    upad_ref[CONV_C_PAD + seq_len:CONV_C_PAD + seq_len + CONV_C_PAD, :] = zeros
    for i in range(seq_len // CONV_C_BLK):
        t = slice(i * CONV_C_BLK, (i + 1) * CONV_C_BLK)
        upad_ref[CONV_C_PAD + i * CONV_C_BLK:CONV_C_PAD + (i + 1) * CONV_C_BLK, :] = (
            sc_ref[t, :].astype(F32) * sin_ref[t, :].astype(F32))
    for i in range(seq_len // CONV_C_BLK):
        t0 = CONV_C_PAD + i * CONV_C_BLK
        y = (w_ref[0:1, :] * upad_ref[t0 - shift:t0 - shift + CONV_C_BLK, :]
             + w_ref[1:2, :] * upad_ref[t0:t0 + CONV_C_BLK, :]
             + w_ref[2:3, :] * upad_ref[t0 + shift:t0 + shift + CONV_C_BLK, :])
        t = slice(i * CONV_C_BLK, (i + 1) * CONV_C_BLK)
        o_ref[t, :] = (sb_ref[t, :].astype(F32) * y).astype(BF16)


def _conv_c(p, conv_w, *, n_seq, seq_len, shift, in_blk, b_blk, c_blk, name):
    taps, ch = conv_w.shape
    assert taps == 3 and shift <= CONV_C_PAD
    n = n_seq * seq_len
    kern = functools.partial(_conv_c_kernel, seq_len=seq_len, shift=shift)
    return pl.pallas_call(
        kern,
        grid=(n_seq,),
        in_specs=[pl.BlockSpec((seq_len, ch), lambda b: (b, in_blk)),
                  pl.BlockSpec((seq_len, ch), lambda b: (b, b_blk)),
                  pl.BlockSpec((seq_len, ch), lambda b: (b, c_blk)),
                  pl.BlockSpec((taps, ch), lambda b: (0, 0))],
        out_specs=pl.BlockSpec((seq_len, ch), lambda b: (b, 0)),
        out_shape=jax.ShapeDtypeStruct((n, ch), BF16),
        scratch_shapes=[pltpu.VMEM((seq_len + 2 * CONV_C_PAD, ch), F32)],
        compiler_params=_params(1),
        name=name,
    )(p, p, p, conv_w)


def _log_sigmoid(z):
    return jnp.minimum(z, 0.0) - jnp.log1p(jnp.exp(-jnp.abs(z)))


def _forget_pieces(gates):
    return _split3(_log_sigmoid(gates[:, LANE:2 * LANE]))


def _forget_cumsum(pieces, heads):
    chunk = pieces[0].shape[0]
    jj = lax.broadcasted_iota(jnp.int32, (chunk, chunk), 0)
    ss = lax.broadcasted_iota(jnp.int32, (chunk, chunk), 1)
    tri_p = jnp.where(ss <= jj, 1.0, 0.0).astype(BF16)
    tri_s = jnp.where(ss >= jj, 1.0, 0.0).astype(BF16)
    b_p = jnp.zeros((chunk, LANE), F32)
    b_s = jnp.zeros((chunk, LANE), F32)
    for piece in pieces:
        b_p = b_p + jnp.dot(tri_p, piece, preferred_element_type=F32)
        b_s = b_s + jnp.dot(tri_s, piece, preferred_element_type=F32)
    lane = lax.broadcasted_iota(jnp.int32, (chunk, LANE), 1)
    return jnp.where(lane < heads, b_p, b_s)


def _gate_stats(gates, b, heads):
    chunk = gates.shape[0]
    assert 2 * heads <= COMBOS and 2 * N_SPLIT * COMBOS <= LANE
    lane = lax.broadcasted_iota(jnp.int32, (chunk, LANE), 1)
    row = lax.broadcasted_iota(jnp.int32, (chunk, LANE), 0)
    fwd = lane < heads
    a = gates[:, 0:LANE] - b

    run_max = a
    sh = 1
    while sh < chunk:
        up = pltpu.roll(run_max, sh, axis=0)
        dn = pltpu.roll(run_max, chunk - sh, axis=0)
        cand = jnp.where(fwd, jnp.where(row >= sh, up, MASKED_LOG),
                         jnp.where(row < chunk - sh, dn, MASKED_LOG))
        run_max = jnp.maximum(run_max, cand)
        sh *= 2

    a_rows = a.T[0:COMBOS, :]
    b_rows = b.T[0:COMBOS, :]
    rows = jnp.concatenate(
        [a_rows,
         jnp.broadcast_to(jnp.max(a_rows, axis=1, keepdims=True), (COMBOS, chunk)),
         jnp.broadcast_to(jnp.min(b_rows, axis=1, keepdims=True), (COMBOS, chunk))], axis=0)

    packed = jnp.zeros((chunk, LANE), F32)
    pieces = _split3(run_max) + _split3(b)
    for i, piece in enumerate(pieces):
        val = jnp.where(lane < COMBOS, piece.astype(F32), 0.0)
        packed = packed + (pltpu.roll(val, COMBOS * i, axis=1) if i else val)
    return packed.astype(BF16), rows


SCAN_GROUP = 4


def _twice(r):
    return jnp.concatenate([r, r], axis=1)


def _chunk_front(q, k, v, xs, a_row, a_max, sel, *, need_h):
    chunk, dh = k.shape
    v_ext = jnp.concatenate([v, jnp.ones_like(v)], axis=1)
    k_t = k.T
    front = {"q": q, "v_ext": v_ext, "a_row": a_row, "a_max": a_max}
    if need_h:
        front["stats"] = jnp.dot(xs, sel, preferred_element_type=F32)
        front["raw"] = jnp.dot(q, k_t, preferred_element_type=F32)
    a_max_row = jnp.concatenate([a_max] * (chunk // dh), axis=1)
    kw_t = (k_t.astype(F32) * jnp.exp(a_row - a_max_row)).astype(BF16)
    front["update"] = jnp.dot(kw_t, v_ext, preferred_element_type=F32)
    return front


def _chunk_back(front, b_last, c_ext, m_st, *, reverse, need_h):
    q, v_ext, a_row, a_max = front["q"], front["v_ext"], front["a_row"], front["a_max"]
    chunk, dh = v_ext.shape[0], v_ext.shape[1] // 2
    h = None
    if need_h:
        a_run, b_col = front["stats"][:, :dh], front["stats"][:, dh:]
        raw = front["raw"]
        g_row = jnp.maximum(a_run, m_st)
        jj = lax.broadcasted_iota(jnp.int32, (dh, dh), 0)
        ss = lax.broadcasted_iota(jnp.int32, (dh, dh), 1)
        diag_seen = (ss >= jj) if reverse else (ss <= jj)
        n_blk = chunk // dh
        score_rows = []
        for rb in range(n_blk):
            r_sl = slice(rb * dh, (rb + 1) * dh)
            blocks = []
            for cb in range(n_blk):
                c_sl = slice(cb * dh, (cb + 1) * dh)
                if (cb < rb) if reverse else (cb > rb):
                    blocks.append(jnp.zeros((dh, dh), BF16))
                    continue
                arg = a_row[:, c_sl] - g_row[r_sl, :]
                if cb == rb:
                    arg = jnp.where(diag_seen, arg, MASKED_LOG)
                blocks.append((raw[r_sl, c_sl] * jnp.exp(arg)).astype(BF16))
            score_rows.append(jnp.concatenate(blocks, axis=1))
        scores = jnp.concatenate(score_rows, axis=0)
        q_inter = (q.astype(F32) * jnp.exp(m_st - g_row)).astype(BF16)
        both = jnp.dot(jnp.concatenate([scores, q_inter], axis=1),
                       jnp.concatenate([v_ext, c_ext.astype(BF16)], axis=0),
                       preferred_element_type=F32)
        floor = jnp.exp(-(b_col + g_row))
        h = both[:, :dh] / jnp.maximum(jnp.abs(both[:, dh:]), floor)

    g_last = jnp.maximum(a_max, m_st)
    c_new = (_twice(jnp.exp(m_st - g_last)) * c_ext
             + _twice(jnp.exp(a_max - g_last)) * front["update"])
    return h, c_new, b_last + g_last


def _mlstm_kernel(*refs, chunk, n_lat, n_ctx, ctx_out, heads):
    if ctx_out:
        (ql, kl, vl, ol, qc, kc, vc, oc, xsl, rwl, xsc, rwc, nw,
         out_l, out_c, hf_l, hb_l, hf_c, hb_c, c_ref, m_ref) = refs
    else:
        (ql, kl, vl, ol, kc, vc, xsl, rwl, xsc, rwc, nw,
         out_l, hf_l, hb_l, c_ref, m_ref) = refs
        qc = oc = out_c = hf_c = hb_c = None
    dh = nw.shape[-1]
    head = pl.program_id(1)

    c_ref[...] = jnp.zeros_like(c_ref)
    m_ref[...] = jnp.zeros_like(m_ref)

    kk = lax.broadcasted_iota(jnp.int32, (LANE, 2 * dh), 0)
    nn = lax.broadcasted_iota(jnp.int32, (LANE, 2 * dh), 1)
    piece = jnp.right_shift(kk, COMBOS.bit_length() - 1)
    first_piece = jnp.where(nn < dh, 0, N_SPLIT)
    target = (piece >= first_piece) & (piece < first_piece + N_SPLIT)
    pair_lane = jnp.bitwise_and(kk, COMBOS - 1)
    sels = [jnp.where(target & (pair_lane == direction * heads + head), 1.0, 0.0).astype(BF16)
            for direction in (0, 1)]

    def scan(q_ref, k_ref, v_ref, xs_ref, rw_ref, hf_ref, hb_ref, n_chunks, need_h):
        group = SCAN_GROUP if n_chunks % SCAN_GROUP == 0 else 1

        def front(i, direction):
            rows = pl.ds(pl.multiple_of(i * chunk, chunk), chunk)
            pair = direction * heads + head
            a_row = rw_ref[i, pl.ds(pair, 1), :]
            a_max = rw_ref[i, pl.ds(COMBOS + pair, 1), :][:, 0:dh]
            return _chunk_front(q_ref[rows, :] if need_h else None, k_ref[rows, :], v_ref[rows, :],
                                xs_ref[rows, :], a_row, a_max, sels[direction], need_h=need_h)

        def back(i, direction, fr):
            pair = direction * heads + head
            b_last = rw_ref[i, pl.ds(2 * COMBOS + pair, 1), :][:, 0:dh]
            h, c_new, m_new = _chunk_back(fr, b_last, c_ref[direction], m_ref[direction],
                                          reverse=direction == 1, need_h=need_h)
            c_ref[direction] = c_new
            m_ref[direction] = m_new
            if need_h:
                rows = pl.ds(pl.multiple_of(i * chunk, chunk), chunk)
                (hf_ref, hb_ref)[direction][rows, :] = h

        def body(gi, carry):
            steps = []
            for u in range(group):
                i = gi * group + u
                steps += [(i, 0), (n_chunks - 1 - i, 1)]
            fr = front(*steps[0])
            for s_idx, (i, direction) in enumerate(steps):
                fr_next = front(*steps[s_idx + 1]) if s_idx + 1 < len(steps) else None
                back(i, direction, fr)
                fr = fr_next
            return carry

        lax.fori_loop(0, n_chunks // group, body, 0)

    def finish(hf_ref, hb_ref, o_ref, out_ref, n_chunks):
        mean_w = jnp.full((dh, dh), 1.0 / dh, BF16)

        def body(i, carry):
            rows = pl.ds(pl.multiple_of(i * chunk, chunk), chunk)
            h = hf_ref[rows, :] + hb_ref[rows, :]
            mu = jnp.dot(h.astype(BF16), mean_w, preferred_element_type=F32)
            dlt = h - mu
            var = jnp.dot((dlt * dlt).astype(BF16), mean_w, preferred_element_type=F32)
            y = dlt * lax.rsqrt(var + NORM_EPS) * nw[...]
            out_ref[rows, :] = (y * jax.nn.sigmoid(o_ref[rows, :].astype(F32))).astype(BF16)
            return carry

        lax.fori_loop(0, n_chunks, body, 0, unroll=True)

    scan(qc, kc, vc, xsc, rwc, hf_c, hb_c, n_ctx, ctx_out)
    scan(ql, kl, vl, xsl, rwl, hf_l, hb_l, n_lat, True)
    finish(hf_l, hb_l, ol, out_l, n_lat)
    if ctx_out:
        finish(hf_c, hb_c, oc, out_c, n_ctx)


def _mlstm(p_lat, p_ctx, stats_lat, stats_ctx, norm_w, *, n_seq, t_lat, t_ctx, heads, dh, chunk,
           lat_blk, ctx_blk, ctx_out, name):
    assert chunk % dh == 0
    n_lat, n_ctx = t_lat // chunk, t_ctx // chunk

    def tok_spec(t, blk):
        return pl.BlockSpec((t, dh), lambda b, h: (b, blk + h))

    def stat_specs(t, n_chunks):
        return [pl.BlockSpec((t, LANE), lambda b, h: (b, 0)),
                pl.BlockSpec((n_chunks, 3 * COMBOS, chunk), lambda b, h: (b, 0, 0))]

    lat_in = [tok_spec(t_lat, lat_blk[name_]) for name_ in "qkvo"]
    ctx_names = "qkvo" if ctx_out else "kv"
    ctx_in = [tok_spec(t_ctx, ctx_blk[name_]) for name_ in ctx_names]
    in_specs = (lat_in + ctx_in + stat_specs(t_lat, n_lat) + stat_specs(t_ctx, n_ctx)
                + [pl.BlockSpec((1, dh), lambda b, h: (0, h))])
    args = ([p_lat] * 4 + [p_ctx] * len(ctx_names) + list(stats_lat) + list(stats_ctx)
            + [norm_w.reshape(1, heads * dh)])

    out_specs = [pl.BlockSpec((t_lat, dh), lambda b, h: (b, h))]
    out_shape = [jax.ShapeDtypeStruct((n_seq * t_lat, heads * dh), BF16)]
    scratch = [pltpu.VMEM((t_lat, dh), F32), pltpu.VMEM((t_lat, dh), F32)]
    if ctx_out:
        out_specs.append(pl.BlockSpec((t_ctx, dh), lambda b, h: (b, h)))
        out_shape.append(jax.ShapeDtypeStruct((n_seq * t_ctx, heads * dh), BF16))
        scratch += [pltpu.VMEM((t_ctx, dh), F32), pltpu.VMEM((t_ctx, dh), F32)]
    scratch += [pltpu.VMEM((2, dh, 2 * dh), F32), pltpu.VMEM((2, 1, dh), F32)]

    kern = functools.partial(_mlstm_kernel, chunk=chunk, n_lat=n_lat, n_ctx=n_ctx, ctx_out=ctx_out,
                             heads=heads)
    return pl.pallas_call(
        kern,
        grid=(n_seq, heads),
        in_specs=in_specs,
        out_specs=out_specs,
        out_shape=out_shape,
        scratch_shapes=scratch,
        compiler_params=_params(2),
        name=name,
    )(*args)


OUT_PROJ_SUB_ROWS = 256


def _out_proj_kernel(a_ref, m_ref, c_ref, w_ref, x_ref, gpost_ref, g1_ref, gpre_ref, sh2_ref, sc2_ref,
                     x1_ref, h2_ref, *, sub_rows):
    da, dm = a_ref.shape[-1], m_ref.shape[-1]
    tm = x_ref.shape[0]
    gain1 = g1_ref[0] * gpost_ref[...]
    gain2 = gpre_ref[...] * (1.0 + sc2_ref[0])
    for r in range(tm // sub_rows):
        rows = slice(r * sub_rows, (r + 1) * sub_rows)
        y = (jnp.dot(a_ref[rows, :], w_ref[0:da, :], preferred_element_type=F32)
             + jnp.dot(m_ref[rows, :], w_ref[da:da + dm, :], preferred_element_type=F32)
             + jnp.dot(c_ref[rows, :], w_ref[da + dm:, :], preferred_element_type=F32))
        x1 = x_ref[rows, :] + _rms(y) * gain1
        x1_ref[rows, :] = x1
        h2_ref[rows, :] = (_rms(x1) * gain2 + sh2_ref[0]).astype(BF16)


def _out_proj(a, m, cc, w_out, x2d, mod3, mod_row, g_post, g_pre_ffn, *, layer, tm, name):
    n, d = x2d.shape
    da, dm, dc = a.shape[1], m.shape[1], cc.shape[1]
    row = lambda i: (i, 0)
    fixed = lambda i: (0, 0)
    mod_spec = lambda k: pl.BlockSpec((1, 1, d), lambda i: (mod_row(i), 0, k))
    return pl.pallas_call(
        functools.partial(_out_proj_kernel, sub_rows=min(tm, OUT_PROJ_SUB_ROWS)),
        grid=(n // tm,),
        in_specs=[pl.BlockSpec((tm, da), row), pl.BlockSpec((tm, dm), row), pl.BlockSpec((tm, dc), row),
                  pl.BlockSpec((None, da + dm + dc, d), lambda i: (layer, 0, 0),
                               pipeline_mode=pl.Buffered(1)),
                  pl.BlockSpec((tm, d), row),
                  pl.BlockSpec((1, d), fixed),
                  mod_spec(2),
                  pl.BlockSpec((1, d), fixed),
                  mod_spec(3), mod_spec(4)],
        out_specs=[pl.BlockSpec((tm, d), row), pl.BlockSpec((tm, d), row)],
        out_shape=[jax.ShapeDtypeStruct((n, d), F32), jax.ShapeDtypeStruct((n, d), BF16)],
        compiler_params=_params(1),
        name=name,
    )(a, m, cc, w_out, x2d, g_post.reshape(1, d), mod3, g_pre_ffn.reshape(1, d), mod3, mod3)


FFN_TF = 1024


FFN_SUB_ROWS = 256


def _ffn_kernel(h_ref, w1_ref, w2_ref, x_ref, gpost_ref, g2_ref, o_ref, acc_ref, *, sub_rows):
    j = pl.program_id(1)
    last = pl.num_programs(1) - 1

    @pl.when(j == 0)
    def _():
        acc_ref[...] = jnp.zeros_like(acc_ref)

    def hidden_step(rows):
        u = jnp.maximum(jnp.dot(h_ref[rows, :], w1_ref[...], preferred_element_type=F32), 0.0)
        return jnp.dot((u * u).astype(BF16), w2_ref[...], preferred_element_type=F32)

    @pl.when(j < last)
    def _():
        acc_ref[...] += hidden_step(slice(None))

    @pl.when(j == last)
    def _():
        gain = g2_ref[0] * gpost_ref[...]
        for r in range(h_ref.shape[0] // sub_rows):
            rows = slice(r * sub_rows, (r + 1) * sub_rows)
            y = acc_ref[rows, :] + hidden_step(rows)
            o_ref[rows, :] = x_ref[rows, :] + _rms(y) * gain


def _ffn(h2, w1, w2, x1, mod3, mod_row, g_post, *, layer, tm, tf, name):
    n, d = x1.shape
    dff = w1.shape[2]
    return pl.pallas_call(
        functools.partial(_ffn_kernel, sub_rows=min(tm, FFN_SUB_ROWS)),
        grid=(n // tm, dff // tf),
        in_specs=[pl.BlockSpec((tm, d), lambda i, j: (i, 0)),
                  pl.BlockSpec((None, d, tf), lambda i, j: (layer, 0, j)),
                  pl.BlockSpec((None, tf, d), lambda i, j: (layer, j, 0)),
                  pl.BlockSpec((tm, d), lambda i, j: (i, 0)),
                  pl.BlockSpec((1, d), lambda i, j: (0, 0)),
                  pl.BlockSpec((1, 1, d), lambda i, j: (mod_row(i), 0, 5))],
        out_specs=pl.BlockSpec((tm, d), lambda i, j: (i, 0)),
        out_shape=jax.ShapeDtypeStruct((n, d), F32),
        scratch_shapes=[pltpu.VMEM((tm, d), F32)],
        compiler_params=_params(2),
        name=name,
    )(h2, w1, w2, x1, g_post.reshape(1, d), mod3)


def _tile(n, target):
    t = min(n, target)
    while n % t or (t % 8 and t != n):
        t -= 1
    return t


def kernel(x, c, ctx, c_ctx, w_ada, b_ada, g_pre_mix, g_post_mix, g_pre_ffn, g_post_ffn, w_in, b_gates,
           conv_a_w, conv_a_b, ln_a_w, ln_a_b, mlstm_norm_w, conv_c_w, w_out, w_ff1, w_ff2):
    bsz, t_lat, d = x.shape
    t_ctx = ctx.shape[1]
    depth = w_ada.shape[0]
    d_conv = conv_a_w.shape[-1]
    d_short = conv_c_w.shape[-1]
    d_mlstm = mlstm_norm_w.shape[-1]
    heads = b_gates.shape[-1] // 4
    dh = d_mlstm // heads
    n_gates = 4 * heads
    assert dh == LANE and d_conv == d_short and d_conv % LANE == 0
    assert bsz + 1 <= MOD_ROWS and t_lat % GRID_W == 0
    assert w_in.shape[-1] == 2 * d_conv + 4 * d_mlstm + n_gates + 3 * d_short

    tn = d_conv
    chunk = min(256, t_ctx)
    assert d_mlstm % tn == 0 and t_ctx % chunk == 0 and t_lat % chunk == 0
    gate_lo = 2 * d_conv + 4 * d_mlstm
    mt = d_mlstm // tn
    tile = {"a_val": 0, "a_gate": 1, "q": 2, "k": 2 + mt, "v": 2 + 2 * mt, "o": 2 + 3 * mt,
            "s_in": 2 + 4 * mt, "s_b": 3 + 4 * mt, "s_c": 4 + 4 * mt}
    assert tile["a_gate"] == CONV_IN_TILES - 1
    stored = {k_: v_ - CONV_IN_TILES for k_, v_ in tile.items() if v_ >= CONV_IN_TILES}
    lane_blk = {k_: v_ * (tn // dh) for k_, v_ in stored.items()}
    q_scale = float(dh) ** -0.5

    xl = x.reshape(bsz * t_lat, d)
    xc = ctx.reshape(bsz * t_ctx, d)

    cond = jnp.concatenate([c, c_ctx[None, :], jnp.zeros((MOD_ROWS - bsz - 1, d), F32)], axis=0)
    mod3 = _modulation(cond, w_ada, b_ada).reshape(depth * MOD_ROWS, 1, 6 * d)

    tm_in = _tile(t_lat, 512)
    tm_in_ctx = _tile(bsz * t_ctx, 512)
    tm_lat = _tile(t_lat, 512)
    tm_ctx = _tile(bsz * t_ctx, 512)
    tm_out = _tile(t_lat, 512)
    tm_out_ctx = _tile(bsz * t_ctx, 512)

    def gate_cols(g):
        i_f, f_f, i_b, f_b = jnp.split(g, 4, axis=-1)
        pad = jnp.zeros(g.shape[:-1] + (LANE - 2 * heads,), g.dtype)
        return jnp.concatenate([i_f, i_b, pad, f_f, f_b, pad], axis=-1)

    w_head, wg, w_tail = _split_w_in(w_in, gate_lo, heads)
    kv_cols = (tile["o"] - tile["k"]) * tn
    assert (tile["k"] * tn) % kv_cols == 0
    full_parts = [(w_head, gate_lo, 0), (w_tail, w_tail.shape[2], 0)]
    kv_parts = [(w_head, kv_cols, tile["k"] * tn // kv_cols)]
    w_out_b = _to_bf16(w_out, name="cast_w_out")
    w1_b = _to_bf16(w_ff1, name="cast_w_ff1")
    w2_b = _to_bf16(w_ff2, name="cast_w_ff2")

    for layer in range(depth):
        last = layer == depth - 1
        base = layer * MOD_ROWS
        bg = gate_cols(b_gates[layer]).reshape(1, 2 * LANE)
        conv_params = (conv_a_w[layer], conv_a_b[layer], ln_a_w[layer], ln_a_b[layer])

        def lat_row(tm):
            return lambda i: base + (i * tm) // t_lat
        ctx_row = lambda i: base + bsz

        proj = functools.partial(_in_proj, mod3=mod3, g_pre=g_pre_mix[layer], wg=wg, bg=bg, layer=layer,
                                 chunk=chunk, heads=heads, tn=tn, q_scale=q_scale)
        p_lat, xs_lat, rows_lat, a_lat = proj(
            xl, mod_row=lat_row(tm_in), w_parts=full_parts, conv=(GRID_W,) + conv_params, tm=tm_in,
            q_tile_lo=tile["q"], q_tile_hi=tile["k"], name=f"in_proj_lat_{layer}")
        if last:
            p_ctx, xs_ctx, rows_ctx = proj(
                xc, mod_row=ctx_row, w_parts=kv_parts, conv=None, tm=tm_in_ctx,
                q_tile_lo=0, q_tile_hi=0, name=f"in_proj_ctx_{layer}")
            ctx_blk = {"k": 0, "v": lane_blk["v"] - lane_blk["k"]}
        else:
            p_ctx, xs_ctx, rows_ctx, a_ctx = proj(
                xc, mod_row=ctx_row, w_parts=full_parts, conv=(t_ctx,) + conv_params, tm=tm_in_ctx,
                q_tile_lo=tile["q"], q_tile_hi=tile["k"], name=f"in_proj_ctx_{layer}")
            ctx_blk = lane_blk

        mres = _mlstm(p_lat, p_ctx, (xs_lat, rows_lat), (xs_ctx, rows_ctx), mlstm_norm_w[layer],
                      n_seq=bsz, t_lat=t_lat, t_ctx=t_ctx, heads=heads, dh=dh, chunk=chunk,
                      lat_blk=lane_blk, ctx_blk=ctx_blk, ctx_out=not last, name=f"mlstm_{layer}")

        conv_c = functools.partial(_conv_c, conv_w=conv_c_w[layer], n_seq=bsz, in_blk=stored["s_in"],
                                   b_blk=stored["s_b"], c_blk=stored["s_c"])
        c_lat = conv_c(p_lat, seq_len=t_lat, shift=GRID_W, name=f"conv_c_lat_{layer}")
        x1, h2 = _out_proj(a_lat, mres[0], c_lat, w_out_b, xl, mod3, lat_row(tm_out), g_post_mix[layer],
                           g_pre_ffn[layer], layer=layer, tm=tm_out, name=f"out_proj_lat_{layer}")
        xl = _ffn(h2, w1_b, w2_b, x1, mod3, lat_row(tm_lat), g_post_ffn[layer], layer=layer, tm=tm_lat,
                  tf=FFN_TF,
                  name=f"ffn_lat_{layer}")

        if not last:
            c_ctx_mix = conv_c(p_ctx, seq_len=t_ctx, shift=1, name=f"conv_c_ctx_{layer}")
            x1c, h2c = _out_proj(a_ctx, mres[1], c_ctx_mix, w_out_b, xc, mod3, ctx_row, g_post_mix[layer],
                                 g_pre_ffn[layer], layer=layer, tm=tm_out_ctx, name=f"out_proj_ctx_{layer}")
            xc = _ffn(h2c, w1_b, w2_b, x1c, mod3, ctx_row, g_post_ffn[layer], layer=layer, tm=tm_ctx,
                      tf=FFN_TF,
                      name=f"ffn_ctx_{layer}")

    return xl.reshape(bsz, t_lat, d)
```

```python
import functools

import jax
import jax.numpy as jnp
from jax import lax
from jax.experimental import pallas as pl
from jax.experimental.pallas import tpu as pltpu

GRID_W = 64
NORM_EPS = 1e-6
LANE = 128
SUBLANE = 8
VMEM_LIMIT_BYTES = 56 * 1024 * 1024
MOD_ROWS = 16
MASKED_LOG = -1e30
COMBOS = 16
N_SPLIT = 3

F32 = jnp.float32
BF16 = jnp.bfloat16


def _params(n_grid_axes):
    return pltpu.CompilerParams(
        dimension_semantics=("arbitrary",) * n_grid_axes,
        vmem_limit_bytes=VMEM_LIMIT_BYTES)


def _rms(y):
    return y * lax.rsqrt(jnp.mean(y * y, axis=-1, keepdims=True) + NORM_EPS)


def _split3(x):
    hi = x.astype(BF16)
    r1 = x - hi.astype(F32)
    mid = r1.astype(BF16)
    lo = (r1 - mid.astype(F32)).astype(BF16)
    return hi, mid, lo


CAST_BLOCK_BYTES = 4 * 1024 * 1024


def _cast_kernel(w_ref, o_ref):
    o_ref[...] = w_ref[...].astype(o_ref.dtype)


def _to_bf16(w, cols=None, *, name):
    depth, rows, c = w.shape
    cols = c if cols is None else cols
    tr = _tile(rows, max(SUBLANE, CAST_BLOCK_BYTES // (4 * cols)))
    spec = pl.BlockSpec((1, tr, cols), lambda l, i: (l, i, 0))
    return pl.pallas_call(
        _cast_kernel,
        grid=(depth, rows // tr),
        in_specs=[spec],
        out_specs=spec,
        out_shape=jax.ShapeDtypeStruct((depth, rows, cols), BF16),
        compiler_params=_params(2),
        name=name,
    )(w)


def _split_w_in_kernel(wt_ref, place_ref, head_ref, gate_ref, tail_ref, *, gate_lo, n_gates):
    x = wt_ref[0]
    head_ref[0] = x[:gate_lo, :].astype(BF16).T
    gate_ref[0] = lax.dot_general(x[gate_lo:gate_lo + n_gates, :].astype(BF16), place_ref[...],
                                  (((0,), (0,)), ((), ())), preferred_element_type=F32).astype(BF16)
    tail_ref[0] = x[gate_lo + n_gates:, :].astype(BF16).T


def _split_w_in(w_in, gate_lo, heads):
    depth, rows, c = w_in.shape
    n_gates = 4 * heads
    n_tail = c - gate_lo - n_gates
    assert gate_lo % SUBLANE == 0 and n_gates % SUBLANE == 0
    src = jnp.arange(n_gates)
    kind, head = src // heads, src % heads
    dst = jnp.where(kind % 2 == 0, 0, LANE) + (kind // 2) * heads + head
    place = (dst[:, None] == jnp.arange(2 * LANE)[None, :]).astype(BF16)
    tr = _tile(rows, max(LANE, CAST_BLOCK_BYTES // (4 * c) // LANE * LANE))
    blk = lambda cols: pl.BlockSpec((1, tr, cols), lambda l, i: (l, i, 0))
    return pl.pallas_call(
        functools.partial(_split_w_in_kernel, gate_lo=gate_lo, n_gates=n_gates),
        grid=(depth, rows // tr),
        in_specs=[pl.BlockSpec((1, c, tr), lambda l, i: (l, 0, i)),
                  pl.BlockSpec((n_gates, 2 * LANE), lambda l, i: (0, 0))],
        out_specs=[blk(gate_lo), blk(2 * LANE), blk(n_tail)],
        out_shape=[jax.ShapeDtypeStruct((depth, rows, gate_lo), BF16),
                   jax.ShapeDtypeStruct((depth, rows, 2 * LANE), BF16),
                   jax.ShapeDtypeStruct((depth, rows, n_tail), BF16)],
        compiler_params=_params(2),
        name="cast_w_in",
    )(jnp.swapaxes(w_in, 1, 2), place)


def _mod_kernel(s_ref, w_ref, b_ref, o_ref):
    s = s_ref[...]
    a = (s * jax.nn.sigmoid(s)).astype(BF16)
    o_ref[0] = jnp.dot(a, w_ref[0].astype(BF16), preferred_element_type=F32) + b_ref[0]


def _modulation(s, w_ada, b_ada, tn=1024):
    depth, d, n = w_ada.shape
    return pl.pallas_call(
        _mod_kernel,
        grid=(depth, n // tn),
        in_specs=[pl.BlockSpec((MOD_ROWS, d), lambda l, j: (0, 0)),
                  pl.BlockSpec((1, d, tn), lambda l, j: (l, 0, j)),
                  pl.BlockSpec((1, 1, tn), lambda l, j: (l, 0, j))],
        out_specs=pl.BlockSpec((1, MOD_ROWS, tn), lambda l, j: (l, 0, j)),
        out_shape=jax.ShapeDtypeStruct((depth, MOD_ROWS, n), F32),
        compiler_params=_params(2),
        name="adaln_modulation",
    )(s, w_ada, b_ada.reshape(depth, 1, n))


def _in_proj_kernel(*refs, n_parts, chunk, heads, tn, q_tile_lo, q_tile_hi, q_scale, conv_row_len,
                    conv_taps):
    x_ref, g_ref, sh_ref, sc_ref, wg_ref, bg_ref = refs[:6]
    n_in = 6
    conv = conv_row_len is not None
    if conv:
        cw_ref, cb_ref, lnw_ref, lnb_ref = refs[n_in:n_in + 4]
        n_in += 4
    w_refs = refs[n_in:n_in + n_parts]
    outs = refs[n_in + n_parts:]
    if conv:
        p_ref, xs_ref, rows_ref, a_ref, upad_ref, y_ref = outs
    else:
        p_ref, xs_ref, rows_ref = outs
    n_conv = CONV_IN_TILES if conv else 0

    gain = g_ref[...] * (1.0 + sc_ref[0])
    n_sub = x_ref.shape[0] // chunk
    tiles = [(w_ref, j) for w_ref in w_refs for j in range(w_ref.shape[1] // tn)]
    last = len(tiles) - 1
    if conv:
        rows_per_sub = chunk // conv_row_len
        items = []
        for g in range(rows_per_sub):
            for sb in range(conv_row_len // CONV_SUB):
                items += [(g, sb, c) for c in range(upad_ref.shape[-1] // LANE)] + [(g, sb, None)]
        first = min(n_conv, last)
        per_tile = -(-len(items) // (last - first + 1))
        items_at = {first + i: items[i * per_tile:(i + 1) * per_tile] for i in range(last - first + 1)}
        zeros = jnp.zeros((CONV_A_PAD, upad_ref.shape[-1]), F32)
        for row in range(n_sub * rows_per_sub):
            upad_ref[row, 0:CONV_A_PAD, :] = zeros
            upad_ref[row, CONV_A_PAD + conv_row_len:CONV_A_PAD + conv_row_len + CONV_A_PAD, :] = zeros

    def normed(r):
        return (_rms(x_ref[r * chunk:(r + 1) * chunk, :]) * gain + sh_ref[0]).astype(BF16)

    hb = normed(0)
    for r in range(n_sub):
        rows = slice(r * chunk, (r + 1) * chunk)
        gates = jnp.dot(hb, wg_ref[...], preferred_element_type=F32) + bg_ref[...]
        hb_next = None
        conv_in = []
        for t, (w_ref, j) in enumerate(tiles):
            acc = jnp.dot(hb, w_ref[:, j * tn:(j + 1) * tn], preferred_element_type=F32)
            if t < n_conv:
                conv_in.append(acc)
                if t == n_conv - 1:
                    u = conv_in[0] * jax.nn.sigmoid(conv_in[1])
                    for g in range(rows_per_sub):
                        upad_ref[r * rows_per_sub + g, CONV_A_PAD:CONV_A_PAD + conv_row_len, :] = (
                            u[g * conv_row_len:(g + 1) * conv_row_len, :])
            else:
                if q_tile_lo <= t < q_tile_hi:
                    acc = acc * q_scale
                p_ref[rows, (t - n_conv) * tn:(t - n_conv + 1) * tn] = acc.astype(BF16)
            if t == 0:
                pieces = _forget_pieces(gates)
            if t == min(2, last):
                b = _forget_cumsum(pieces, heads)
            if t == min(4, last):
                xs_ref[rows, :], rows_ref[r] = _gate_stats(gates, b, heads)
            if t == min(6, last) and r + 1 < n_sub:
                hb_next = normed(r + 1)
            if conv:
                for g, sb, c in items_at.get(t, []):
                    row = r * rows_per_sub + g
                    y_blk = y_ref.at[row % y_ref.shape[0]]
                    if c is None:
                        _conv_a_finish(y_blk, lnw_ref, lnb_ref, a_ref,
                                       r * chunk + g * conv_row_len + sb * CONV_SUB)
                    else:
                        _conv_a_piece(upad_ref, y_blk, cw_ref, cb_ref, row, sb, c, conv_taps)
        hb = hb_next


def _in_proj(x2d, mod3, mod_row, g_pre, w_parts, wg, bg, conv, *, layer, chunk, heads, tm, tn, q_tile_lo,
             q_tile_hi, q_scale, name):
    n, d = x2d.shape
    ng = wg.shape[2]
    nw = sum(width for _, width, _ in w_parts) - (CONV_IN_TILES * tn if conv else 0)
    assert tm % chunk == 0 and all(width % tn == 0 for _, width, _ in w_parts)
    fixed = lambda i: (0, 0)

    def w_spec(width, blk):
        return pl.BlockSpec((None, d, width), lambda i: (layer, 0, blk), pipeline_mode=pl.Buffered(1))

    in_specs = [pl.BlockSpec((tm, d), lambda i: (i, 0)),
                pl.BlockSpec((1, d), fixed),
                pl.BlockSpec((1, 1, d), lambda i: (mod_row(i), 0, 0)),
                pl.BlockSpec((1, 1, d), lambda i: (mod_row(i), 0, 1)),
                pl.BlockSpec((None, d, ng), lambda i: (layer, 0, 0)),
                pl.BlockSpec((1, ng), fixed)]
    args = [x2d, g_pre.reshape(1, d), mod3, mod3, wg, bg]
    out_specs = [pl.BlockSpec((tm, nw), lambda i: (i, 0)),
                 pl.BlockSpec((tm, LANE), lambda i: (i, 0)),
                 pl.BlockSpec((tm // chunk, 3 * COMBOS, chunk), lambda i: (i, 0, 0))]
    out_shape = [jax.ShapeDtypeStruct((n, nw), BF16),
                 jax.ShapeDtypeStruct((n, LANE), BF16),
                 jax.ShapeDtypeStruct((n // chunk, 3 * COMBOS, chunk), F32)]
    scratch = []
    row_len = taps = None
    if conv:
        row_len, conv_w, conv_b, ln_w, ln_b = conv
        taps = conv_w.shape[0]
        assert conv_w.shape[1] == tn and chunk % row_len == 0 and row_len % CONV_SUB == 0
        assert taps // 2 < CONV_A_PAD
        in_specs += [pl.BlockSpec((taps, tn), fixed)] + [pl.BlockSpec((1, tn), fixed)] * 3
        args += [conv_w, conv_b.reshape(1, tn), ln_w.reshape(1, tn), ln_b.reshape(1, tn)]
        out_specs.append(pl.BlockSpec((tm, tn), lambda i: (i, 0)))
        out_shape.append(jax.ShapeDtypeStruct((n, tn), BF16))
        conv_rows = tm // row_len
        scratch = [pltpu.VMEM((conv_rows, row_len + 2 * CONV_A_PAD, tn), F32),
                   pltpu.VMEM((min(conv_rows, 4), CONV_SUB, tn), F32)]
    in_specs += [w_spec(width, blk) for _, width, blk in w_parts]
    args += [arr for arr, _, _ in w_parts]

    kern = functools.partial(_in_proj_kernel, n_parts=len(w_parts), chunk=chunk, heads=heads, tn=tn,
                             q_tile_lo=q_tile_lo, q_tile_hi=q_tile_hi, q_scale=q_scale,
                             conv_row_len=row_len, conv_taps=taps)
    return pl.pallas_call(
        kern,
        grid=(n // tm,),
        in_specs=in_specs,
        out_specs=out_specs,
        out_shape=out_shape,
        scratch_shapes=scratch,
        compiler_params=_params(1),
        name=name,
    )(*args)


CONV_A_PAD = 16
CONV_SUB = 64
CONV_IN_TILES = 2


def _conv_a_piece(upad_ref, y_ref, w_ref, b_ref, row, sb, c, taps):
    half = taps // 2
    cs = slice(c * LANE, (c + 1) * LANE)
    acc = None
    for res in range(SUBLANE):
        z = None
        for k in range(taps):
            if (k - half) % SUBLANE != res:
                continue
            start = CONV_A_PAD + sb * CONV_SUB + (k - half) - res
            term = w_ref[k:k + 1, cs] * upad_ref[row, start:start + CONV_SUB + SUBLANE, cs]
            z = term if z is None else z + term
        if z is None:
            continue
        z = z[res:res + CONV_SUB, :]
        acc = z if acc is None else acc + z
    y_ref[:, cs] = acc + b_ref[:, cs]


def _conv_a_finish(y_ref, lnw_ref, lnb_ref, o_ref, out_row):
    y = y_ref[...]
    mu = jnp.mean(y, axis=-1, keepdims=True)
    dlt = y - mu
    var = jnp.mean(dlt * dlt, axis=-1, keepdims=True)
    z = dlt * lax.rsqrt(var + NORM_EPS) * lnw_ref[...] + lnb_ref[...]
    o_ref[out_row:out_row + CONV_SUB, :] = (z * jax.nn.sigmoid(z)).astype(BF16)


CONV_C_PAD = 64
CONV_C_BLK = 64


def _conv_c_kernel(sin_ref, sb_ref, sc_ref, w_ref, o_ref, upad_ref, *, seq_len, shift):
    ch = o_ref.shape[-1]
    zeros = jnp.zeros((CONV_C_PAD, ch), F32)
    upad_ref[0:CONV_C_PAD, :] = zeros
    upad_ref[CONV_C_PAD + seq_len:CONV_C_PAD + seq_len + CONV_C_PAD, :] = zeros
    for i in range(seq_len // CONV_C_BLK):
        t = slice(i * CONV_C_BLK, (i + 1) * CONV_C_BLK)
        upad_ref[CONV_C_PAD + i * CONV_C_BLK:CONV_C_PAD + (i + 1) * CONV_C_BLK, :] = (
            sc_ref[t, :].astype(F32) * sin_ref[t, :].astype(F32))
    for i in range(seq_len // CONV_C_BLK):
        t0 = CONV_C_PAD + i * CONV_C_BLK
        y = (w_ref[0:1, :] * upad_ref[t0 - shift:t0 - shift + CONV_C_BLK, :]
             + w_ref[1:2, :] * upad_ref[t0:t0 + CONV_C_BLK, :]
             + w_ref[2:3, :] * upad_ref[t0 + shift:t0 + shift + CONV_C_BLK, :])
        t = slice(i * CONV_C_BLK, (i + 1) * CONV_C_BLK)
        o_ref[t, :] = (sb_ref[t, :].astype(F32) * y).astype(BF16)


def _conv_c(p, conv_w, *, n_seq, seq_len, shift, in_blk, b_blk, c_blk, name):
    taps, ch = conv_w.shape
    assert taps == 3 and shift <= CONV_C_PAD
    n = n_seq * seq_len
    kern = functools.partial(_conv_c_kernel, seq_len=seq_len, shift=shift)
    return pl.pallas_call(
        kern,
        grid=(n_seq,),
        in_specs=[pl.BlockSpec((seq_len, ch), lambda b: (b, in_blk)),
                  pl.BlockSpec((seq_len, ch), lambda b: (b, b_blk)),
                  pl.BlockSpec((seq_len, ch), lambda b: (b, c_blk)),
                  pl.BlockSpec((taps, ch), lambda b: (0, 0))],
        out_specs=pl.BlockSpec((seq_len, ch), lambda b: (b, 0)),
        out_shape=jax.ShapeDtypeStruct((n, ch), BF16),
        scratch_shapes=[pltpu.VMEM((seq_len + 2 * CONV_C_PAD, ch), F32)],
        compiler_params=_params(1),
        name=name,
    )(p, p, p, conv_w)


def _log_sigmoid(z):
    return jnp.minimum(z, 0.0) - jnp.log1p(jnp.exp(-jnp.abs(z)))


def _forget_pieces(gates):
    return _split3(_log_sigmoid(gates[:, LANE:2 * LANE]))


def _forget_cumsum(pieces, heads):
    chunk = pieces[0].shape[0]
    jj = lax.broadcasted_iota(jnp.int32, (chunk, chunk), 0)
    ss = lax.broadcasted_iota(jnp.int32, (chunk, chunk), 1)
    tri_p = jnp.where(ss <= jj, 1.0, 0.0).astype(BF16)
    tri_s = jnp.where(ss >= jj, 1.0, 0.0).astype(BF16)
    b_p = jnp.zeros((chunk, LANE), F32)
    b_s = jnp.zeros((chunk, LANE), F32)
    for piece in pieces:
        b_p = b_p + jnp.dot(tri_p, piece, preferred_element_type=F32)
        b_s = b_s + jnp.dot(tri_s, piece, preferred_element_type=F32)
    lane = lax.broadcasted_iota(jnp.int32, (chunk, LANE), 1)
    return jnp.where(lane < heads, b_p, b_s)


def _gate_stats(gates, b, heads):
    chunk = gates.shape[0]
    assert 2 * heads <= COMBOS and 2 * N_SPLIT * COMBOS <= LANE
    lane = lax.broadcasted_iota(jnp.int32, (chunk, LANE), 1)
    row = lax.broadcasted_iota(jnp.int32, (chunk, LANE), 0)
    fwd = lane < heads
    a = gates[:, 0:LANE] - b

    run_max = a
    sh = 1
    while sh < chunk:
        up = pltpu.roll(run_max, sh, axis=0)
        dn = pltpu.roll(run_max, chunk - sh, axis=0)
        cand = jnp.where(fwd, jnp.where(row >= sh, up, MASKED_LOG),
                         jnp.where(row < chunk - sh, dn, MASKED_LOG))
        run_max = jnp.maximum(run_max, cand)
        sh *= 2

    a_rows = a.T[0:COMBOS, :]
    b_rows = b.T[0:COMBOS, :]
    rows = jnp.concatenate(
        [a_rows,
         jnp.broadcast_to(jnp.max(a_rows, axis=1, keepdims=True), (COMBOS, chunk)),
         jnp.broadcast_to(jnp.min(b_rows, axis=1, keepdims=True), (COMBOS, chunk))], axis=0)

    packed = jnp.zeros((chunk, LANE), F32)
    pieces = _split3(run_max) + _split3(b)
    for i, piece in enumerate(pieces):
        val = jnp.where(lane < COMBOS, piece.astype(F32), 0.0)
        packed = packed + (pltpu.roll(val, COMBOS * i, axis=1) if i else val)
    return packed.astype(BF16), rows


SCAN_GROUP = 4


def _twice(r):
    return jnp.concatenate([r, r], axis=1)


def _chunk_front(q, k, v, xs, a_row, a_max, sel, *, need_h):
    chunk, dh = k.shape
    v_ext = jnp.concatenate([v, jnp.ones_like(v)], axis=1)
    k_t = k.T
    front = {"q": q, "v_ext": v_ext, "a_row": a_row, "a_max": a_max}
    if need_h:
        front["stats"] = jnp.dot(xs, sel, preferred_element_type=F32)
        front["raw"] = jnp.dot(q, k_t, preferred_element_type=F32)
    a_max_row = jnp.concatenate([a_max] * (chunk // dh), axis=1)
    kw_t = (k_t.astype(F32) * jnp.exp(a_row - a_max_row)).astype(BF16)
    front["update"] = jnp.dot(kw_t, v_ext, preferred_element_type=F32)
    return front


def _chunk_back(front, b_last, c_ext, m_st, *, reverse, need_h):
    q, v_ext, a_row, a_max = front["q"], front["v_ext"], front["a_row"], front["a_max"]
    chunk, dh = v_ext.shape[0], v_ext.shape[1] // 2
    h = None
    if need_h:
        a_run, b_col = front["stats"][:, :dh], front["stats"][:, dh:]
        raw = front["raw"]
        g_row = jnp.maximum(a_run, m_st)
        jj = lax.broadcasted_iota(jnp.int32, (dh, dh), 0)
        ss = lax.broadcasted_iota(jnp.int32, (dh, dh), 1)
        diag_seen = (ss >= jj) if reverse else (ss <= jj)
        n_blk = chunk // dh
        score_rows = []
        for rb in range(n_blk):
            r_sl = slice(rb * dh, (rb + 1) * dh)
            blocks = []
            for cb in range(n_blk):
                c_sl = slice(cb * dh, (cb + 1) * dh)
                if (cb < rb) if reverse else (cb > rb):
                    blocks.append(jnp.zeros((dh, dh), BF16))
                    continue
                arg = a_row[:, c_sl] - g_row[r_sl, :]
                if cb == rb:
                    arg = jnp.where(diag_seen, arg, MASKED_LOG)
                blocks.append((raw[r_sl, c_sl] * jnp.exp(arg)).astype(BF16))
            score_rows.append(jnp.concatenate(blocks, axis=1))
        scores = jnp.concatenate(score_rows, axis=0)
        q_inter = (q.astype(F32) * jnp.exp(m_st - g_row)).astype(BF16)
        both = jnp.dot(jnp.concatenate([scores, q_inter], axis=1),
                       jnp.concatenate([v_ext, c_ext.astype(BF16)], axis=0),
                       preferred_element_type=F32)
        floor = jnp.exp(-(b_col + g_row))
        h = both[:, :dh] / jnp.maximum(jnp.abs(both[:, dh:]), floor)

    g_last = jnp.maximum(a_max, m_st)
    c_new = (_twice(jnp.exp(m_st - g_last)) * c_ext
             + _twice(jnp.exp(a_max - g_last)) * front["update"])
    return h, c_new, b_last + g_last


def _mlstm_kernel(*refs, chunk, n_lat, n_ctx, ctx_out, heads):
    if ctx_out:
        (ql, kl, vl, ol, qc, kc, vc, oc, xsl, rwl, xsc, rwc, nw,
         out_l, out_c, hf_l, hb_l, hf_c, hb_c, c_ref, m_ref) = refs
    else:
        (ql, kl, vl, ol, kc, vc, xsl, rwl, xsc, rwc, nw,
         out_l, hf_l, hb_l, c_ref, m_ref) = refs
        qc = oc = out_c = hf_c = hb_c = None
    dh = nw.shape[-1]
    head = pl.program_id(1)

    c_ref[...] = jnp.zeros_like(c_ref)
    m_ref[...] = jnp.zeros_like(m_ref)

    kk = lax.broadcasted_iota(jnp.int32, (LANE, 2 * dh), 0)
    nn = lax.broadcasted_iota(jnp.int32, (LANE, 2 * dh), 1)
    piece = jnp.right_shift(kk, COMBOS.bit_length() - 1)
    first_piece = jnp.where(nn < dh, 0, N_SPLIT)
    target = (piece >= first_piece) & (piece < first_piece + N_SPLIT)
    pair_lane = jnp.bitwise_and(kk, COMBOS - 1)
    sels = [jnp.where(target & (pair_lane == direction * heads + head), 1.0, 0.0).astype(BF16)
            for direction in (0, 1)]

    def scan(q_ref, k_ref, v_ref, xs_ref, rw_ref, hf_ref, hb_ref, n_chunks, need_h):
        group = SCAN_GROUP if n_chunks % SCAN_GROUP == 0 else 1

        def front(i, direction):
            rows = pl.ds(pl.multiple_of(i * chunk, chunk), chunk)
            pair = direction * heads + head
            a_row = rw_ref[i, pl.ds(pair, 1), :]
            a_max = rw_ref[i, pl.ds(COMBOS + pair, 1), :][:, 0:dh]
            return _chunk_front(q_ref[rows, :] if need_h else None, k_ref[rows, :], v_ref[rows, :],
                                xs_ref[rows, :], a_row, a_max, sels[direction], need_h=need_h)

        def back(i, direction, fr):
            pair = direction * heads + head
            b_last = rw_ref[i, pl.ds(2 * COMBOS + pair, 1), :][:, 0:dh]
            h, c_new, m_new = _chunk_back(fr, b_last, c_ref[direction], m_ref[direction],
                                          reverse=direction == 1, need_h=need_h)
            c_ref[direction] = c_new
            m_ref[direction] = m_new
            if need_h:
                rows = pl.ds(pl.multiple_of(i * chunk, chunk), chunk)
                (hf_ref, hb_ref)[direction][rows, :] = h

        def body(gi, carry):
            steps = []
            for u in range(group):
                i = gi * group + u
                steps += [(i, 0), (n_chunks - 1 - i, 1)]
            fr = front(*steps[0])
            for s_idx, (i, direction) in enumerate(steps):
                fr_next = front(*steps[s_idx + 1]) if s_idx + 1 < len(steps) else None
                back(i, direction, fr)
                fr = fr_next
            return carry

        lax.fori_loop(0, n_chunks // group, body, 0)

    def finish(hf_ref, hb_ref, o_ref, out_ref, n_chunks):
        mean_w = jnp.full((dh, dh), 1.0 / dh, BF16)

        def body(i, carry):
            rows = pl.ds(pl.multiple_of(i * chunk, chunk), chunk)
            h = hf_ref[rows, :] + hb_ref[rows, :]
            mu = jnp.dot(h.astype(BF16), mean_w, preferred_element_type=F32)
            dlt = h - mu
            var = jnp.dot((dlt * dlt).astype(BF16), mean_w, preferred_element_type=F32)
            y = dlt * lax.rsqrt(var + NORM_EPS) * nw[...]
            out_ref[rows, :] = (y * jax.nn.sigmoid(o_ref[rows, :].astype(F32))).astype(BF16)
            return carry

        lax.fori_loop(0, n_chunks, body, 0, unroll=True)

    scan(qc, kc, vc, xsc, rwc, hf_c, hb_c, n_ctx, ctx_out)
    scan(ql, kl, vl, xsl, rwl, hf_l, hb_l, n_lat, True)
    finish(hf_l, hb_l, ol, out_l, n_lat)
    if ctx_out:
        finish(hf_c, hb_c, oc, out_c, n_ctx)


def _mlstm(p_lat, p_ctx, stats_lat, stats_ctx, norm_w, *, n_seq, t_lat, t_ctx, heads, dh, chunk,
           lat_blk, ctx_blk, ctx_out, name):
    assert chunk % dh == 0
    n_lat, n_ctx = t_lat // chunk, t_ctx // chunk

    def tok_spec(t, blk):
        return pl.BlockSpec((t, dh), lambda b, h: (b, blk + h))

    def stat_specs(t, n_chunks):
        return [pl.BlockSpec((t, LANE), lambda b, h: (b, 0)),
                pl.BlockSpec((n_chunks, 3 * COMBOS, chunk), lambda b, h: (b, 0, 0))]

    lat_in = [tok_spec(t_lat, lat_blk[name_]) for name_ in "qkvo"]
    ctx_names = "qkvo" if ctx_out else "kv"
    ctx_in = [tok_spec(t_ctx, ctx_blk[name_]) for name_ in ctx_names]
    in_specs = (lat_in + ctx_in + stat_specs(t_lat, n_lat) + stat_specs(t_ctx, n_ctx)
                + [pl.BlockSpec((1, dh), lambda b, h: (0, h))])
    args = ([p_lat] * 4 + [p_ctx] * len(ctx_names) + list(stats_lat) + list(stats_ctx)
            + [norm_w.reshape(1, heads * dh)])

    out_specs = [pl.BlockSpec((t_lat, dh), lambda b, h: (b, h))]
    out_shape = [jax.ShapeDtypeStruct((n_seq * t_lat, heads * dh), BF16)]
    scratch = [pltpu.VMEM((t_lat, dh), F32), pltpu.VMEM((t_lat, dh), F32)]
    if ctx_out:
        out_specs.append(pl.BlockSpec((t_ctx, dh), lambda b, h: (b, h)))
        out_shape.append(jax.ShapeDtypeStruct((n_seq * t_ctx, heads * dh), BF16))
        scratch += [pltpu.VMEM((t_ctx, dh), F32), pltpu.VMEM((t_ctx, dh), F32)]
    scratch += [pltpu.VMEM((2, dh, 2 * dh), F32), pltpu.VMEM((2, 1, dh), F32)]

    kern = functools.partial(_mlstm_kernel, chunk=chunk, n_lat=n_lat, n_ctx=n_ctx, ctx_out=ctx_out,
                             heads=heads)
    return pl.pallas_call(
        kern,
        grid=(n_seq, heads),
        in_specs=in_specs,
        out_specs=out_specs,
        out_shape=out_shape,
        scratch_shapes=scratch,
        compiler_params=_params(2),
        name=name,
    )(*args)


OUT_PROJ_SUB_ROWS = 256


def _out_proj_kernel(a_ref, m_ref, c_ref, w_ref, x_ref, gpost_ref, g1_ref, gpre_ref, sh2_ref, sc2_ref,
                     x1_ref, h2_ref, *, sub_rows):
    da, dm = a_ref.shape[-1], m_ref.shape[-1]
    tm = x_ref.shape[0]
    gain1 = g1_ref[0] * gpost_ref[...]
    gain2 = gpre_ref[...] * (1.0 + sc2_ref[0])
    for r in range(tm // sub_rows):
        rows = slice(r * sub_rows, (r + 1) * sub_rows)
        y = (jnp.dot(a_ref[rows, :], w_ref[0:da, :], preferred_element_type=F32)
             + jnp.dot(m_ref[rows, :], w_ref[da:da + dm, :], preferred_element_type=F32)
             + jnp.dot(c_ref[rows, :], w_ref[da + dm:, :], preferred_element_type=F32))
        x1 = x_ref[rows, :] + _rms(y) * gain1
        x1_ref[rows, :] = x1
        h2_ref[rows, :] = (_rms(x1) * gain2 + sh2_ref[0]).astype(BF16)


def _out_proj(a, m, cc, w_out, x2d, mod3, mod_row, g_post, g_pre_ffn, *, layer, tm, name):
    n, d = x2d.shape
    da, dm, dc = a.shape[1], m.shape[1], cc.shape[1]
    row = lambda i: (i, 0)
    fixed = lambda i: (0, 0)
    mod_spec = lambda k: pl.BlockSpec((1, 1, d), lambda i: (mod_row(i), 0, k))
    return pl.pallas_call(
        functools.partial(_out_proj_kernel, sub_rows=min(tm, OUT_PROJ_SUB_ROWS)),
        grid=(n // tm,),
        in_specs=[pl.BlockSpec((tm, da), row), pl.BlockSpec((tm, dm), row), pl.BlockSpec((tm, dc), row),
                  pl.BlockSpec((None, da + dm + dc, d), lambda i: (layer, 0, 0),
                               pipeline_mode=pl.Buffered(1)),
                  pl.BlockSpec((tm, d), row),
                  pl.BlockSpec((1, d), fixed),
                  mod_spec(2),
                  pl.BlockSpec((1, d), fixed),
                  mod_spec(3), mod_spec(4)],
        out_specs=[pl.BlockSpec((tm, d), row), pl.BlockSpec((tm, d), row)],
        out_shape=[jax.ShapeDtypeStruct((n, d), F32), jax.ShapeDtypeStruct((n, d), BF16)],
        compiler_params=_params(1),
        name=name,
    )(a, m, cc, w_out, x2d, g_post.reshape(1, d), mod3, g_pre_ffn.reshape(1, d), mod3, mod3)


FFN_TF = 1024


FFN_SUB_ROWS = 256


def _ffn_kernel(h_ref, w1_ref, w2_ref, x_ref, gpost_ref, g2_ref, o_ref, acc_ref, *, sub_rows):
    j = pl.program_id(1)
    last = pl.num_programs(1) - 1

    @pl.when(j == 0)
    def _():
        acc_ref[...] = jnp.zeros_like(acc_ref)

    def hidden_step(rows):
        u = jnp.maximum(jnp.dot(h_ref[rows, :], w1_ref[...], preferred_element_type=F32), 0.0)
        return jnp.dot((u * u).astype(BF16), w2_ref[...], preferred_element_type=F32)

    @pl.when(j < last)
    def _():
        acc_ref[...] += hidden_step(slice(None))

    @pl.when(j == last)
    def _():
        gain = g2_ref[0] * gpost_ref[...]
        for r in range(h_ref.shape[0] // sub_rows):
            rows = slice(r * sub_rows, (r + 1) * sub_rows)
            y = acc_ref[rows, :] + hidden_step(rows)
            o_ref[rows, :] = x_ref[rows, :] + _rms(y) * gain


def _ffn(h2, w1, w2, x1, mod3, mod_row, g_post, *, layer, tm, tf, name):
    n, d = x1.shape
    dff = w1.shape[2]
    n_hidden = dff // tf

    def hidden(i, j):
        return jnp.where(i % 2 == 0, j, n_hidden - 1 - j)

    return pl.pallas_call(
        functools.partial(_ffn_kernel, sub_rows=min(tm, FFN_SUB_ROWS)),
        grid=(n // tm, n_hidden),
        in_specs=[pl.BlockSpec((tm, d), lambda i, j: (i, 0)),
                  pl.BlockSpec((None, d, tf), lambda i, j: (layer, 0, hidden(i, j))),
                  pl.BlockSpec((None, tf, d), lambda i, j: (layer, hidden(i, j), 0)),
                  pl.BlockSpec((tm, d), lambda i, j: (i, 0)),
                  pl.BlockSpec((1, d), lambda i, j: (0, 0)),
                  pl.BlockSpec((1, 1, d), lambda i, j: (mod_row(i), 0, 5))],
        out_specs=pl.BlockSpec((tm, d), lambda i, j: (i, 0)),
        out_shape=jax.ShapeDtypeStruct((n, d), F32),
        scratch_shapes=[pltpu.VMEM((tm, d), F32)],
        compiler_params=_params(2),
        name=name,
    )(h2, w1, w2, x1, g_post.reshape(1, d), mod3)


def _tile(n, target):
    t = min(n, target)
    while n % t or (t % 8 and t != n):
        t -= 1
    return t


def kernel(x, c, ctx, c_ctx, w_ada, b_ada, g_pre_mix, g_post_mix, g_pre_ffn, g_post_ffn, w_in, b_gates,
           conv_a_w, conv_a_b, ln_a_w, ln_a_b, mlstm_norm_w, conv_c_w, w_out, w_ff1, w_ff2):
    bsz, t_lat, d = x.shape
    t_ctx = ctx.shape[1]
    depth = w_ada.shape[0]
    d_conv = conv_a_w.shape[-1]
    d_short = conv_c_w.shape[-1]
    d_mlstm = mlstm_norm_w.shape[-1]
    heads = b_gates.shape[-1] // 4
    dh = d_mlstm // heads
    n_gates = 4 * heads
    assert dh == LANE and d_conv == d_short and d_conv % LANE == 0
    assert bsz + 1 <= MOD_ROWS and t_lat % GRID_W == 0
    assert w_in.shape[-1] == 2 * d_conv + 4 * d_mlstm + n_gates + 3 * d_short

    tn = d_conv
    chunk = min(256, t_ctx)
    assert d_mlstm % tn == 0 and t_ctx % chunk == 0 and t_lat % chunk == 0
    gate_lo = 2 * d_conv + 4 * d_mlstm
    mt = d_mlstm // tn
    tile = {"a_val": 0, "a_gate": 1, "q": 2, "k": 2 + mt, "v": 2 + 2 * mt, "o": 2 + 3 * mt,
            "s_in": 2 + 4 * mt, "s_b": 3 + 4 * mt, "s_c": 4 + 4 * mt}
    assert tile["a_gate"] == CONV_IN_TILES - 1
    stored = {k_: v_ - CONV_IN_TILES for k_, v_ in tile.items() if v_ >= CONV_IN_TILES}
    lane_blk = {k_: v_ * (tn // dh) for k_, v_ in stored.items()}
    q_scale = float(dh) ** -0.5

    xl = x.reshape(bsz * t_lat, d)
    xc = ctx.reshape(bsz * t_ctx, d)

    cond = jnp.concatenate([c, c_ctx[None, :], jnp.zeros((MOD_ROWS - bsz - 1, d), F32)], axis=0)
    mod3 = _modulation(cond, w_ada, b_ada).reshape(depth * MOD_ROWS, 1, 6 * d)

    tm_in = _tile(t_lat, 512)
    tm_in_ctx = _tile(bsz * t_ctx, 512)
    tm_lat = _tile(t_lat, 512)
    tm_ctx = _tile(bsz * t_ctx, 512)
    tm_out = _tile(t_lat, 512)
    tm_out_ctx = _tile(bsz * t_ctx, 512)

    def gate_cols(g):
        i_f, f_f, i_b, f_b = jnp.split(g, 4, axis=-1)
        pad = jnp.zeros(g.shape[:-1] + (LANE - 2 * heads,), g.dtype)
        return jnp.concatenate([i_f, i_b, pad, f_f, f_b, pad], axis=-1)

    w_head, wg, w_tail = _split_w_in(w_in, gate_lo, heads)
    kv_cols = (tile["o"] - tile["k"]) * tn
    assert (tile["k"] * tn) % kv_cols == 0
    full_parts = [(w_head, gate_lo, 0), (w_tail, w_tail.shape[2], 0)]
    kv_parts = [(w_head, kv_cols, tile["k"] * tn // kv_cols)]
    w_out_b = _to_bf16(w_out, name="cast_w_out")
    w1_b = _to_bf16(w_ff1, name="cast_w_ff1")
    w2_b = _to_bf16(w_ff2, name="cast_w_ff2")

    for layer in range(depth):
        last = layer == depth - 1
        base = layer * MOD_ROWS
        bg = gate_cols(b_gates[layer]).reshape(1, 2 * LANE)
        conv_params = (conv_a_w[layer], conv_a_b[layer], ln_a_w[layer], ln_a_b[layer])

        def lat_row(tm):
            return lambda i: base + (i * tm) // t_lat
        ctx_row = lambda i: base + bsz

        proj = functools.partial(_in_proj, mod3=mod3, g_pre=g_pre_mix[layer], wg=wg, bg=bg, layer=layer,
                                 chunk=chunk, heads=heads, tn=tn, q_scale=q_scale)
        p_lat, xs_lat, rows_lat, a_lat = proj(
            xl, mod_row=lat_row(tm_in), w_parts=full_parts, conv=(GRID_W,) + conv_params, tm=tm_in,
            q_tile_lo=tile["q"], q_tile_hi=tile["k"], name=f"in_proj_lat_{layer}")
        if last:
            p_ctx, xs_ctx, rows_ctx = proj(
                xc, mod_row=ctx_row, w_parts=kv_parts, conv=None, tm=tm_in_ctx,
                q_tile_lo=0, q_tile_hi=0, name=f"in_proj_ctx_{layer}")
            ctx_blk = {"k": 0, "v": lane_blk["v"] - lane_blk["k"]}
        else:
            p_ctx, xs_ctx, rows_ctx, a_ctx = proj(
                xc, mod_row=ctx_row, w_parts=full_parts, conv=(t_ctx,) + conv_params, tm=tm_in_ctx,
                q_tile_lo=tile["q"], q_tile_hi=tile["k"], name=f"in_proj_ctx_{layer}")
            ctx_blk = lane_blk

        mres = _mlstm(p_lat, p_ctx, (xs_lat, rows_lat), (xs_ctx, rows_ctx), mlstm_norm_w[layer],
                      n_seq=bsz, t_lat=t_lat, t_ctx=t_ctx, heads=heads, dh=dh, chunk=chunk,
                      lat_blk=lane_blk, ctx_blk=ctx_blk, ctx_out=not last, name=f"mlstm_{layer}")

        conv_c = functools.partial(_conv_c, conv_w=conv_c_w[layer], n_seq=bsz, in_blk=stored["s_in"],
                                   b_blk=stored["s_b"], c_blk=stored["s_c"])
        c_lat = conv_c(p_lat, seq_len=t_lat, shift=GRID_W, name=f"conv_c_lat_{layer}")
        x1, h2 = _out_proj(a_lat, mres[0], c_lat, w_out_b, xl, mod3, lat_row(tm_out), g_post_mix[layer],
                           g_pre_ffn[layer], layer=layer, tm=tm_out, name=f"out_proj_lat_{layer}")
        xl = _ffn(h2, w1_b, w2_b, x1, mod3, lat_row(tm_lat), g_post_ffn[layer], layer=layer, tm=tm_lat,
                  tf=FFN_TF,
                  name=f"ffn_lat_{layer}")

        if not last:
            c_ctx_mix = conv_c(p_ctx, seq_len=t_ctx, shift=1, name=f"conv_c_ctx_{layer}")
            x1c, h2c = _out_proj(a_ctx, mres[1], c_ctx_mix, w_out_b, xc, mod3, ctx_row, g_post_mix[layer],
                                 g_pre_ffn[layer], layer=layer, tm=tm_out_ctx, name=f"out_proj_ctx_{layer}")
            xc = _ffn(h2c, w1_b, w2_b, x1c, mod3, ctx_row, g_post_ffn[layer], layer=layer, tm=tm_ctx,
                      tf=FFN_TF,
                      name=f"ffn_ctx_{layer}")

    return xl.reshape(bsz, t_lat, d)
```

```python
import functools

import jax
import jax.numpy as jnp
from jax import lax
from jax.experimental import pallas as pl
from jax.experimental.pallas import tpu as pltpu

GRID_W = 64
NORM_EPS = 1e-6
LANE = 128
SUBLANE = 8
VMEM_LIMIT_BYTES = 56 * 1024 * 1024
MOD_ROWS = 16
MASKED_LOG = -1e30
COMBOS = 16
N_SPLIT = 3

F32 = jnp.float32
BF16 = jnp.bfloat16


def _params(n_grid_axes):
    return pltpu.CompilerParams(
        dimension_semantics=("arbitrary",) * n_grid_axes,
        vmem_limit_bytes=VMEM_LIMIT_BYTES)


def _rms(y):
    return y * lax.rsqrt(jnp.mean(y * y, axis=-1, keepdims=True) + NORM_EPS)


def _split3(x):
    hi = x.astype(BF16)
    r1 = x - hi.astype(F32)
    mid = r1.astype(BF16)
    lo = (r1 - mid.astype(F32)).astype(BF16)
    return hi, mid, lo


CAST_BLOCK_BYTES = 4 * 1024 * 1024


def _split_w_in_kernel(wt_ref, place_ref, head_ref, gate_ref, tail_ref, *, gate_lo, n_gates):
    x = wt_ref[0]
    head_ref[0] = x[:gate_lo, :].astype(BF16).T
    gate_ref[0] = lax.dot_general(x[gate_lo:gate_lo + n_gates, :].astype(BF16), place_ref[...],
                                  (((0,), (0,)), ((), ())), preferred_element_type=F32).astype(BF16)
    tail_ref[0] = x[gate_lo + n_gates:, :].astype(BF16).T


def _split_w_in(w_in, gate_lo, heads):
    depth, rows, c = w_in.shape
    n_gates = 4 * heads
    n_tail = c - gate_lo - n_gates
    assert gate_lo % SUBLANE == 0 and n_gates % SUBLANE == 0
    src = jnp.arange(n_gates)
    kind, head = src // heads, src % heads
    dst = jnp.where(kind % 2 == 0, 0, LANE) + (kind // 2) * heads + head
    place = (dst[:, None] == jnp.arange(2 * LANE)[None, :]).astype(BF16)
    tr = _tile(rows, max(LANE, CAST_BLOCK_BYTES // (4 * c) // LANE * LANE))
    blk = lambda cols: pl.BlockSpec((1, tr, cols), lambda l, i: (l, i, 0))
    return pl.pallas_call(
        functools.partial(_split_w_in_kernel, gate_lo=gate_lo, n_gates=n_gates),
        grid=(depth, rows // tr),
        in_specs=[pl.BlockSpec((1, c, tr), lambda l, i: (l, 0, i)),
                  pl.BlockSpec((n_gates, 2 * LANE), lambda l, i: (0, 0))],
        out_specs=[blk(gate_lo), blk(2 * LANE), blk(n_tail)],
        out_shape=[jax.ShapeDtypeStruct((depth, rows, gate_lo), BF16),
                   jax.ShapeDtypeStruct((depth, rows, 2 * LANE), BF16),
                   jax.ShapeDtypeStruct((depth, rows, n_tail), BF16)],
        compiler_params=_params(2),
        name="cast_w_in",
    )(jnp.swapaxes(w_in, 1, 2), place)


def _mod_kernel(s_ref, w_ref, b_ref, o_ref):
    s = s_ref[...]
    a = (s * jax.nn.sigmoid(s)).astype(BF16)
    o_ref[0] = jnp.dot(a, w_ref[0].astype(BF16), preferred_element_type=F32) + b_ref[0]


def _modulation(s, w_ada, b_ada, tn=1024):
    depth, d, n = w_ada.shape
    return pl.pallas_call(
        _mod_kernel,
        grid=(depth, n // tn),
        in_specs=[pl.BlockSpec((MOD_ROWS, d), lambda l, j: (0, 0)),
                  pl.BlockSpec((1, d, tn), lambda l, j: (l, 0, j)),
                  pl.BlockSpec((1, 1, tn), lambda l, j: (l, 0, j))],
        out_specs=pl.BlockSpec((1, MOD_ROWS, tn), lambda l, j: (l, 0, j)),
        out_shape=jax.ShapeDtypeStruct((depth, MOD_ROWS, n), F32),
        compiler_params=_params(2),
        name="adaln_modulation",
    )(s, w_ada, b_ada.reshape(depth, 1, n))


def _in_proj_kernel(*refs, n_parts, chunk, heads, tn, q_tile_lo, q_tile_hi, q_scale, glu):
    x_ref, g_ref, sh_ref, sc_ref, wg_ref, bg_ref = refs[:6]
    w_refs = refs[6:6 + n_parts]
    outs = refs[6 + n_parts:]
    if glu:
        p_ref, xs_ref, rows_ref, u_ref = outs
    else:
        p_ref, xs_ref, rows_ref = outs
    n_glu = CONV_IN_TILES if glu else 0

    gain = g_ref[...] * (1.0 + sc_ref[0])
    n_sub = x_ref.shape[0] // chunk
    tiles = [(w_ref, j) for w_ref in w_refs for j in range(w_ref.shape[1] // tn)]
    last = len(tiles) - 1

    def normed(r):
        return (_rms(x_ref[r * chunk:(r + 1) * chunk, :]) * gain + sh_ref[0]).astype(BF16)

    hb = normed(0)
    for r in range(n_sub):
        rows = slice(r * chunk, (r + 1) * chunk)
        gates = jnp.dot(hb, wg_ref[...], preferred_element_type=F32) + bg_ref[...]
        hb_next = None
        value = None
        for t, (w_ref, j) in enumerate(tiles):
            acc = jnp.dot(hb, w_ref[:, j * tn:(j + 1) * tn], preferred_element_type=F32)
            if t < n_glu:
                if t == 0:
                    value = acc
                else:
                    u_ref[rows, :] = (value * jax.nn.sigmoid(acc)).astype(BF16)
            else:
                if q_tile_lo <= t < q_tile_hi:
                    acc = acc * q_scale
                p_ref[rows, (t - n_glu) * tn:(t - n_glu + 1) * tn] = acc.astype(BF16)
            if t == 0:
                pieces = _forget_pieces(gates)
            if t == min(2, last):
                b = _forget_cumsum(pieces, heads)
            if t == min(4, last):
                xs_ref[rows, :], rows_ref[r] = _gate_stats(gates, b, heads)
            if t == min(6, last) and r + 1 < n_sub:
                hb_next = normed(r + 1)
        hb = hb_next


def _in_proj(x2d, mod3, mod_row, g_pre, w_parts, wg, bg, *, glu, layer, chunk, heads, tm, tn, q_tile_lo,
             q_tile_hi, q_scale, name):
    n, d = x2d.shape
    ng = wg.shape[2]
    nw = sum(width for _, width, _ in w_parts) - (CONV_IN_TILES * tn if glu else 0)
    assert tm % chunk == 0 and all(width % tn == 0 for _, width, _ in w_parts)
    fixed = lambda i: (0, 0)

    def w_spec(width, blk):
        return pl.BlockSpec((None, d, width), lambda i: (layer, 0, blk), pipeline_mode=pl.Buffered(1))

    out_specs = [pl.BlockSpec((tm, nw), lambda i: (i, 0)),
                 pl.BlockSpec((tm, LANE), lambda i: (i, 0)),
                 pl.BlockSpec((tm // chunk, 3 * COMBOS, chunk), lambda i: (i, 0, 0))]
    out_shape = [jax.ShapeDtypeStruct((n, nw), BF16),
                 jax.ShapeDtypeStruct((n, LANE), BF16),
                 jax.ShapeDtypeStruct((n // chunk, 3 * COMBOS, chunk), F32)]
    if glu:
        out_specs.append(pl.BlockSpec((tm, tn), lambda i: (i, 0)))
        out_shape.append(jax.ShapeDtypeStruct((n, tn), BF16))
    kern = functools.partial(_in_proj_kernel, n_parts=len(w_parts), chunk=chunk, heads=heads, tn=tn,
                             q_tile_lo=q_tile_lo, q_tile_hi=q_tile_hi, q_scale=q_scale, glu=glu)
    return pl.pallas_call(
        kern,
        grid=(n // tm,),
        in_specs=[pl.BlockSpec((tm, d), lambda i: (i, 0)),
                  pl.BlockSpec((1, d), fixed),
                  pl.BlockSpec((1, 1, d), lambda i: (mod_row(i), 0, 0)),
                  pl.BlockSpec((1, 1, d), lambda i: (mod_row(i), 0, 1)),
                  pl.BlockSpec((None, d, ng), lambda i: (layer, 0, 0)),
                  pl.BlockSpec((1, ng), fixed)]
                 + [w_spec(width, blk) for _, width, blk in w_parts],
        out_specs=out_specs,
        out_shape=out_shape,
        compiler_params=_params(1),
        name=name,
    )(x2d, g_pre.reshape(1, d), mod3, mod3, wg, bg, *[arr for arr, _, _ in w_parts])


CONV_A_PAD = 16
CONV_SUB = 64
CONV_IN_TILES = 2


def _conv_a_piece(upad_ref, y_ref, w_ref, b_ref, row, sb, c, taps):
    half = taps // 2
    cs = slice(c * LANE, (c + 1) * LANE)
    acc = None
    for res in range(SUBLANE):
        z = None
        for k in range(taps):
            if (k - half) % SUBLANE != res:
                continue
            start = CONV_A_PAD + sb * CONV_SUB + (k - half) - res
            term = w_ref[k:k + 1, cs] * upad_ref[row, start:start + CONV_SUB + SUBLANE, cs]
            z = term if z is None else z + term
        if z is None:
            continue
        z = z[res:res + CONV_SUB, :]
        acc = z if acc is None else acc + z
    y_ref[:, cs] = acc + b_ref[:, cs]


def _conv_a_finish(y_ref, lnw_ref, lnb_ref, o_ref, out_row):
    y = y_ref[...]
    mu = jnp.mean(y, axis=-1, keepdims=True)
    dlt = y - mu
    var = jnp.mean(dlt * dlt, axis=-1, keepdims=True)
    z = dlt * lax.rsqrt(var + NORM_EPS) * lnw_ref[...] + lnb_ref[...]
    o_ref[out_row:out_row + CONV_SUB, :] = (z * jax.nn.sigmoid(z)).astype(BF16)


CONV_C_PAD = 64
CONV_C_BLK = 64


def _conv_c_kernel(sin_ref, sb_ref, sc_ref, w_ref, o_ref, upad_ref, *, seq_len, shift):
    ch = o_ref.shape[-1]
    zeros = jnp.zeros((CONV_C_PAD, ch), F32)
    upad_ref[0:CONV_C_PAD, :] = zeros
    upad_ref[CONV_C_PAD + seq_len:CONV_C_PAD + seq_len + CONV_C_PAD, :] = zeros
    for i in range(seq_len // CONV_C_BLK):
        t = slice(i * CONV_C_BLK, (i + 1) * CONV_C_BLK)
        upad_ref[CONV_C_PAD + i * CONV_C_BLK:CONV_C_PAD + (i + 1) * CONV_C_BLK, :] = (
            sc_ref[t, :].astype(F32) * sin_ref[t, :].astype(F32))
    for i in range(seq_len // CONV_C_BLK):
        t0 = CONV_C_PAD + i * CONV_C_BLK
        y = (w_ref[0:1, :] * upad_ref[t0 - shift:t0 - shift + CONV_C_BLK, :]
             + w_ref[1:2, :] * upad_ref[t0:t0 + CONV_C_BLK, :]
             + w_ref[2:3, :] * upad_ref[t0 + shift:t0 + shift + CONV_C_BLK, :])
        t = slice(i * CONV_C_BLK, (i + 1) * CONV_C_BLK)
        o_ref[t, :] = (sb_ref[t, :].astype(F32) * y).astype(BF16)


def _conv_c(p, conv_w, *, n_seq, seq_len, shift, in_blk, b_blk, c_blk, name):
    taps, ch = conv_w.shape
    assert taps == 3 and shift <= CONV_C_PAD
    n = n_seq * seq_len
    kern = functools.partial(_conv_c_kernel, seq_len=seq_len, shift=shift)
    return pl.pallas_call(
        kern,
        grid=(n_seq,),
        in_specs=[pl.BlockSpec((seq_len, ch), lambda b: (b, in_blk)),
                  pl.BlockSpec((seq_len, ch), lambda b: (b, b_blk)),
                  pl.BlockSpec((seq_len, ch), lambda b: (b, c_blk)),
                  pl.BlockSpec((taps, ch), lambda b: (0, 0))],
        out_specs=pl.BlockSpec((seq_len, ch), lambda b: (b, 0)),
        out_shape=jax.ShapeDtypeStruct((n, ch), BF16),
        scratch_shapes=[pltpu.VMEM((seq_len + 2 * CONV_C_PAD, ch), F32)],
        compiler_params=_params(1),
        name=name,
    )(p, p, p, conv_w)


def _log_sigmoid(z):
    return jnp.minimum(z, 0.0) - jnp.log1p(jnp.exp(-jnp.abs(z)))


def _forget_pieces(gates):
    return _split3(_log_sigmoid(gates[:, LANE:2 * LANE]))


def _forget_cumsum(pieces, heads):
    chunk = pieces[0].shape[0]
    jj = lax.broadcasted_iota(jnp.int32, (chunk, chunk), 0)
    ss = lax.broadcasted_iota(jnp.int32, (chunk, chunk), 1)
    tri_p = jnp.where(ss <= jj, 1.0, 0.0).astype(BF16)
    tri_s = jnp.where(ss >= jj, 1.0, 0.0).astype(BF16)
    b_p = jnp.zeros((chunk, LANE), F32)
    b_s = jnp.zeros((chunk, LANE), F32)
    for piece in pieces:
        b_p = b_p + jnp.dot(tri_p, piece, preferred_element_type=F32)
        b_s = b_s + jnp.dot(tri_s, piece, preferred_element_type=F32)
    lane = lax.broadcasted_iota(jnp.int32, (chunk, LANE), 1)
    return jnp.where(lane < heads, b_p, b_s)


def _gate_stats(gates, b, heads):
    chunk = gates.shape[0]
    assert 2 * heads <= COMBOS and 2 * N_SPLIT * COMBOS <= LANE
    lane = lax.broadcasted_iota(jnp.int32, (chunk, LANE), 1)
    row = lax.broadcasted_iota(jnp.int32, (chunk, LANE), 0)
    fwd = lane < heads
    a = gates[:, 0:LANE] - b

    run_max = a
    sh = 1
    while sh < chunk:
        up = pltpu.roll(run_max, sh, axis=0)
        dn = pltpu.roll(run_max, chunk - sh, axis=0)
        cand = jnp.where(fwd, jnp.where(row >= sh, up, MASKED_LOG),
                         jnp.where(row < chunk - sh, dn, MASKED_LOG))
        run_max = jnp.maximum(run_max, cand)
        sh *= 2

    a_rows = a.T[0:COMBOS, :]
    b_rows = b.T[0:COMBOS, :]
    rows = jnp.concatenate(
        [a_rows,
         jnp.broadcast_to(jnp.max(a_rows, axis=1, keepdims=True), (COMBOS, chunk)),
         jnp.broadcast_to(jnp.min(b_rows, axis=1, keepdims=True), (COMBOS, chunk))], axis=0)

    packed = jnp.zeros((chunk, LANE), F32)
    pieces = _split3(run_max) + _split3(b)
    for i, piece in enumerate(pieces):
        val = jnp.where(lane < COMBOS, piece.astype(F32), 0.0)
        packed = packed + (pltpu.roll(val, COMBOS * i, axis=1) if i else val)
    return packed.astype(BF16), rows


SCAN_GROUP = 4


def _twice(r):
    return jnp.concatenate([r, r], axis=1)


def _chunk_front(q, k, v, xs, a_row, a_max, sel, *, need_h):
    chunk, dh = k.shape
    v_ext = jnp.concatenate([v, jnp.ones_like(v)], axis=1)
    k_t = k.T
    front = {"q": q, "v_ext": v_ext, "a_row": a_row, "a_max": a_max}
    if need_h:
        front["stats"] = jnp.dot(xs, sel, preferred_element_type=F32)
        front["raw"] = jnp.dot(q, k_t, preferred_element_type=F32)
    a_max_row = jnp.concatenate([a_max] * (chunk // dh), axis=1)
    kw_t = (k_t.astype(F32) * jnp.exp(a_row - a_max_row)).astype(BF16)
    front["update"] = jnp.dot(kw_t, v_ext, preferred_element_type=F32)
    return front


def _chunk_back(front, b_last, c_ext, m_st, *, reverse, need_h):
    q, v_ext, a_row, a_max = front["q"], front["v_ext"], front["a_row"], front["a_max"]
    chunk, dh = v_ext.shape[0], v_ext.shape[1] // 2
    h = None
    if need_h:
        a_run, b_col = front["stats"][:, :dh], front["stats"][:, dh:]
        raw = front["raw"]
        g_row = jnp.maximum(a_run, m_st)
        jj = lax.broadcasted_iota(jnp.int32, (dh, dh), 0)
        ss = lax.broadcasted_iota(jnp.int32, (dh, dh), 1)
        diag_seen = (ss >= jj) if reverse else (ss <= jj)
        n_blk = chunk // dh
        score_rows = []
        for rb in range(n_blk):
            r_sl = slice(rb * dh, (rb + 1) * dh)
            blocks = []
            for cb in range(n_blk):
                c_sl = slice(cb * dh, (cb + 1) * dh)
                if (cb < rb) if reverse else (cb > rb):
                    blocks.append(jnp.zeros((dh, dh), BF16))
                    continue
                arg = a_row[:, c_sl] - g_row[r_sl, :]
                if cb == rb:
                    arg = jnp.where(diag_seen, arg, MASKED_LOG)
                blocks.append((raw[r_sl, c_sl] * jnp.exp(arg)).astype(BF16))
            score_rows.append(jnp.concatenate(blocks, axis=1))
        scores = jnp.concatenate(score_rows, axis=0)
        q_inter = (q.astype(F32) * jnp.exp(m_st - g_row)).astype(BF16)
        both = jnp.dot(jnp.concatenate([scores, q_inter], axis=1),
                       jnp.concatenate([v_ext, c_ext.astype(BF16)], axis=0),
                       preferred_element_type=F32)
        floor = jnp.exp(-(b_col + g_row))
        h = both[:, :dh] / jnp.maximum(jnp.abs(both[:, dh:]), floor)

    g_last = jnp.maximum(a_max, m_st)
    c_new = (_twice(jnp.exp(m_st - g_last)) * c_ext
             + _twice(jnp.exp(a_max - g_last)) * front["update"])
    return h, c_new, b_last + g_last


def _mlstm_kernel(*refs, chunk, n_lat, n_ctx, ctx_out, heads, n_cast):
    n_in = 13 if ctx_out else 11
    cast_in, refs = refs[n_in:n_in + n_cast], refs[:n_in] + refs[n_in + n_cast:]
    n_io = n_in + (2 if ctx_out else 1)
    cast_out, refs = refs[n_io:n_io + n_cast], refs[:n_io] + refs[n_io + n_cast:]
    if ctx_out:
        (ql, kl, vl, ol, qc, kc, vc, oc, xsl, rwl, xsc, rwc, nw,
         out_l, out_c, hf_l, hb_l, hf_c, hb_c, c_ref, m_ref) = refs
    else:
        (ql, kl, vl, ol, kc, vc, xsl, rwl, xsc, rwc, nw,
         out_l, hf_l, hb_l, c_ref, m_ref) = refs
        qc = oc = out_c = hf_c = hb_c = None
    for w_ref, wb_ref in zip(cast_in, cast_out):
        wb_ref[...] = w_ref[...].astype(BF16)
    dh = nw.shape[-1]
    head = pl.program_id(1)

    c_ref[...] = jnp.zeros_like(c_ref)
    m_ref[...] = jnp.zeros_like(m_ref)

    kk = lax.broadcasted_iota(jnp.int32, (LANE, 2 * dh), 0)
    nn = lax.broadcasted_iota(jnp.int32, (LANE, 2 * dh), 1)
    piece = jnp.right_shift(kk, COMBOS.bit_length() - 1)
    first_piece = jnp.where(nn < dh, 0, N_SPLIT)
    target = (piece >= first_piece) & (piece < first_piece + N_SPLIT)
    pair_lane = jnp.bitwise_and(kk, COMBOS - 1)
    sels = [jnp.where(target & (pair_lane == direction * heads + head), 1.0, 0.0).astype(BF16)
            for direction in (0, 1)]

    def scan(q_ref, k_ref, v_ref, xs_ref, rw_ref, hf_ref, hb_ref, n_chunks, need_h):
        group = SCAN_GROUP if n_chunks % SCAN_GROUP == 0 else 1

        def front(i, direction):
            rows = pl.ds(pl.multiple_of(i * chunk, chunk), chunk)
            pair = direction * heads + head
            a_row = rw_ref[i, pl.ds(pair, 1), :]
            a_max = rw_ref[i, pl.ds(COMBOS + pair, 1), :][:, 0:dh]
            return _chunk_front(q_ref[rows, :] if need_h else None, k_ref[rows, :], v_ref[rows, :],
                                xs_ref[rows, :], a_row, a_max, sels[direction], need_h=need_h)

        def back(i, direction, fr):
            pair = direction * heads + head
            b_last = rw_ref[i, pl.ds(2 * COMBOS + pair, 1), :][:, 0:dh]
            h, c_new, m_new = _chunk_back(fr, b_last, c_ref[direction], m_ref[direction],
                                          reverse=direction == 1, need_h=need_h)
            c_ref[direction] = c_new
            m_ref[direction] = m_new
            if need_h:
                rows = pl.ds(pl.multiple_of(i * chunk, chunk), chunk)
                (hf_ref, hb_ref)[direction][rows, :] = h

        def body(gi, carry):
            steps = []
            for u in range(group):
                i = gi * group + u
                steps += [(i, 0), (n_chunks - 1 - i, 1)]
            fr = front(*steps[0])
            for s_idx, (i, direction) in enumerate(steps):
                fr_next = front(*steps[s_idx + 1]) if s_idx + 1 < len(steps) else None
                back(i, direction, fr)
                fr = fr_next
            return carry

        lax.fori_loop(0, n_chunks // group, body, 0)

    def finish(hf_ref, hb_ref, o_ref, out_ref, n_chunks):
        mean_w = jnp.full((dh, dh), 1.0 / dh, BF16)

        def body(i, carry):
            rows = pl.ds(pl.multiple_of(i * chunk, chunk), chunk)
            h = hf_ref[rows, :] + hb_ref[rows, :]
            mu = jnp.dot(h.astype(BF16), mean_w, preferred_element_type=F32)
            dlt = h - mu
            var = jnp.dot((dlt * dlt).astype(BF16), mean_w, preferred_element_type=F32)
            y = dlt * lax.rsqrt(var + NORM_EPS) * nw[...]
            out_ref[rows, :] = (y * jax.nn.sigmoid(o_ref[rows, :].astype(F32))).astype(BF16)
            return carry

        lax.fori_loop(0, n_chunks, body, 0, unroll=True)

    scan(qc, kc, vc, xsc, rwc, hf_c, hb_c, n_ctx, ctx_out)
    scan(ql, kl, vl, xsl, rwl, hf_l, hb_l, n_lat, True)
    finish(hf_l, hb_l, ol, out_l, n_lat)
    if ctx_out:
        finish(hf_c, hb_c, oc, out_c, n_ctx)


def _mlstm(p_lat, p_ctx, stats_lat, stats_ctx, norm_w, casts, *, layer, n_seq, t_lat, t_ctx, heads, dh,
           chunk, lat_blk, ctx_blk, ctx_out, name):
    assert chunk % dh == 0
    n_steps = n_seq * heads
    n_lat, n_ctx = t_lat // chunk, t_ctx // chunk

    def tok_spec(t, blk):
        return pl.BlockSpec((t, dh), lambda b, h: (b, blk + h))

    def stat_specs(t, n_chunks):
        return [pl.BlockSpec((t, LANE), lambda b, h: (b, 0)),
                pl.BlockSpec((n_chunks, 3 * COMBOS, chunk), lambda b, h: (b, 0, 0))]

    lat_in = [tok_spec(t_lat, lat_blk[name_]) for name_ in "qkvo"]
    ctx_names = "qkvo" if ctx_out else "kv"
    ctx_in = [tok_spec(t_ctx, ctx_blk[name_]) for name_ in ctx_names]
    in_specs = (lat_in + ctx_in + stat_specs(t_lat, n_lat) + stat_specs(t_ctx, n_ctx)
                + [pl.BlockSpec((1, dh), lambda b, h: (0, h))])
    args = ([p_lat] * 4 + [p_ctx] * len(ctx_names) + list(stats_lat) + list(stats_ctx)
            + [norm_w.reshape(1, heads * dh)] + list(casts))
    for w in casts:
        slab = w.shape[1] // n_steps
        assert w.shape[1] % n_steps == 0 and slab % (2 * SUBLANE) == 0
        in_specs.append(pl.BlockSpec((None, slab, w.shape[2]), lambda b, h: (layer, b * heads + h, 0)))

    out_specs = [pl.BlockSpec((t_lat, dh), lambda b, h: (b, h))]
    out_shape = [jax.ShapeDtypeStruct((n_seq * t_lat, heads * dh), BF16)]
    scratch = [pltpu.VMEM((t_lat, dh), F32), pltpu.VMEM((t_lat, dh), F32)]
    if ctx_out:
        out_specs.append(pl.BlockSpec((t_ctx, dh), lambda b, h: (b, h)))
        out_shape.append(jax.ShapeDtypeStruct((n_seq * t_ctx, heads * dh), BF16))
        scratch += [pltpu.VMEM((t_ctx, dh), F32), pltpu.VMEM((t_ctx, dh), F32)]
    scratch += [pltpu.VMEM((2, dh, 2 * dh), F32), pltpu.VMEM((2, 1, dh), F32)]
    for w in casts:
        out_specs.append(pl.BlockSpec((w.shape[1] // n_steps, w.shape[2]), lambda b, h: (b * heads + h, 0)))
        out_shape.append(jax.ShapeDtypeStruct(w.shape[1:], BF16))

    kern = functools.partial(_mlstm_kernel, chunk=chunk, n_lat=n_lat, n_ctx=n_ctx, ctx_out=ctx_out,
                             heads=heads, n_cast=len(casts))
    return pl.pallas_call(
        kern,
        grid=(n_seq, heads),
        in_specs=in_specs,
        out_specs=out_specs,
        out_shape=out_shape,
        scratch_shapes=scratch,
        compiler_params=_params(2),
        name=name,
    )(*args)


OUT_PROJ_SUB_ROWS = 256


def _out_proj_kernel(u_ref, m_ref, c_ref, w_ref, x_ref, gpost_ref, g1_ref, gpre_ref, sh2_ref, sc2_ref,
                     cw_ref, cb_ref, lnw_ref, lnb_ref, x1_ref, h2_ref, upad_ref, y_ref, a_ref,
                     *, sub_rows, conv_row_len, conv_taps):
    da, dm = u_ref.shape[-1], m_ref.shape[-1]
    tm = x_ref.shape[0]
    n_sub = tm // sub_rows
    gain1 = g1_ref[0] * gpost_ref[...]
    gain2 = gpre_ref[...] * (1.0 + sc2_ref[0])

    zeros = jnp.zeros((CONV_A_PAD, da), F32)
    for row in range(tm // conv_row_len):
        upad_ref[row, 0:CONV_A_PAD, :] = zeros
        upad_ref[row, CONV_A_PAD:CONV_A_PAD + conv_row_len, :] = (
            u_ref[row * conv_row_len:(row + 1) * conv_row_len, :].astype(F32))
        upad_ref[row, CONV_A_PAD + conv_row_len:CONV_A_PAD + conv_row_len + CONV_A_PAD, :] = zeros

    items = []
    for row in range(tm // conv_row_len):
        for sb in range(conv_row_len // CONV_SUB):
            items += [(row, sb, c) for c in range(da // LANE)] + [(row, sb, None)]
    d_out = x_ref.shape[1]
    col_tiles = [slice(c0, c0 + da) for c0 in range(0, d_out, da)]
    slots = [(r, cols) for r in range(n_sub) for cols in col_tiles]
    per_slot = -(-len(items) // len(slots))
    for s_idx, (r, cols) in enumerate(slots):
        rows = slice(r * sub_rows, (r + 1) * sub_rows)
        x1_ref[rows, cols] = (jnp.dot(m_ref[rows, :], w_ref[da:da + dm, cols], preferred_element_type=F32)
                              + jnp.dot(c_ref[rows, :], w_ref[da + dm:, cols],
                                        preferred_element_type=F32))
        for row, sb, c in items[s_idx * per_slot:(s_idx + 1) * per_slot]:
            y_blk = y_ref.at[row % y_ref.shape[0]]
            if c is None:
                _conv_a_finish(y_blk, lnw_ref, lnb_ref, a_ref, row * conv_row_len + sb * CONV_SUB)
            else:
                _conv_a_piece(upad_ref, y_blk, cw_ref, cb_ref, row, sb, c, conv_taps)
    for r in range(n_sub):
        rows = slice(r * sub_rows, (r + 1) * sub_rows)
        y = x1_ref[rows, :] + jnp.dot(a_ref[rows, :], w_ref[0:da, :], preferred_element_type=F32)
        x1 = x_ref[rows, :] + _rms(y) * gain1
        x1_ref[rows, :] = x1
        h2_ref[rows, :] = (_rms(x1) * gain2 + sh2_ref[0]).astype(BF16)


def _out_proj(u, m, cc, w_out, x2d, mod3, mod_row, g_post, g_pre_ffn, conv, *, tm, name):
    n, d = x2d.shape
    da, dm, dc = u.shape[1], m.shape[1], cc.shape[1]
    row_len, conv_w, conv_b, ln_w, ln_b = conv
    taps = conv_w.shape[0]
    assert tm % row_len == 0 and row_len % CONV_SUB == 0 and taps // 2 < CONV_A_PAD
    row = lambda i: (i, 0)
    fixed = lambda i: (0, 0)
    mod_spec = lambda k: pl.BlockSpec((1, 1, d), lambda i: (mod_row(i), 0, k))
    conv_rows = tm // row_len
    return pl.pallas_call(
        functools.partial(_out_proj_kernel, sub_rows=min(tm, OUT_PROJ_SUB_ROWS), conv_row_len=row_len,
                          conv_taps=taps),
        grid=(n // tm,),
        in_specs=[pl.BlockSpec((tm, da), row), pl.BlockSpec((tm, dm), row), pl.BlockSpec((tm, dc), row),
                  pl.BlockSpec((da + dm + dc, d), fixed, pipeline_mode=pl.Buffered(1)),
                  pl.BlockSpec((tm, d), row),
                  pl.BlockSpec((1, d), fixed),
                  mod_spec(2),
                  pl.BlockSpec((1, d), fixed),
                  mod_spec(3), mod_spec(4),
                  pl.BlockSpec((taps, da), fixed)] + [pl.BlockSpec((1, da), fixed)] * 3,
        out_specs=[pl.BlockSpec((tm, d), row), pl.BlockSpec((tm, d), row)],
        out_shape=[jax.ShapeDtypeStruct((n, d), F32), jax.ShapeDtypeStruct((n, d), BF16)],
        scratch_shapes=[pltpu.VMEM((conv_rows, row_len + 2 * CONV_A_PAD, da), F32),
                        pltpu.VMEM((min(conv_rows, 4), CONV_SUB, da), F32),
                        pltpu.VMEM((tm, da), BF16)],
        compiler_params=_params(1),
        name=name,
    )(u, m, cc, w_out, x2d, g_post.reshape(1, d), mod3, g_pre_ffn.reshape(1, d), mod3, mod3,
      conv_w, conv_b.reshape(1, da), ln_w.reshape(1, da), ln_b.reshape(1, da))


FFN_TF = 1024


FFN_SUB_ROWS = 256


def _ffn_kernel(h_ref, w1_ref, w2_ref, x_ref, gpost_ref, g2_ref, o_ref, acc_ref, *, sub_rows):
    j = pl.program_id(1)
    last = pl.num_programs(1) - 1

    @pl.when(j == 0)
    def _():
        acc_ref[...] = jnp.zeros_like(acc_ref)

    def hidden_step(rows):
        u = jnp.maximum(jnp.dot(h_ref[rows, :], w1_ref[...], preferred_element_type=F32), 0.0)
        return jnp.dot((u * u).astype(BF16), w2_ref[...], preferred_element_type=F32)

    @pl.when(j < last)
    def _():
        acc_ref[...] += hidden_step(slice(None))

    @pl.when(j == last)
    def _():
        gain = g2_ref[0] * gpost_ref[...]
        for r in range(h_ref.shape[0] // sub_rows):
            rows = slice(r * sub_rows, (r + 1) * sub_rows)
            y = acc_ref[rows, :] + hidden_step(rows)
            o_ref[rows, :] = x_ref[rows, :] + _rms(y) * gain


def _ffn(h2, w1, w2, x1, mod3, mod_row, g_post, *, tm, tf, name):
    n, d = x1.shape
    dff = w1.shape[1]
    n_hidden = dff // tf

    def hidden(i, j):
        return jnp.where(i % 2 == 0, j, n_hidden - 1 - j)

    return pl.pallas_call(
        functools.partial(_ffn_kernel, sub_rows=min(tm, FFN_SUB_ROWS)),
        grid=(n // tm, n_hidden),
        in_specs=[pl.BlockSpec((tm, d), lambda i, j: (i, 0)),
                  pl.BlockSpec((d, tf), lambda i, j: (0, hidden(i, j))),
                  pl.BlockSpec((tf, d), lambda i, j: (hidden(i, j), 0)),
                  pl.BlockSpec((tm, d), lambda i, j: (i, 0)),
                  pl.BlockSpec((1, d), lambda i, j: (0, 0)),
                  pl.BlockSpec((1, 1, d), lambda i, j: (mod_row(i), 0, 5))],
        out_specs=pl.BlockSpec((tm, d), lambda i, j: (i, 0)),
        out_shape=jax.ShapeDtypeStruct((n, d), F32),
        scratch_shapes=[pltpu.VMEM((tm, d), F32)],
        compiler_params=_params(2),
        name=name,
    )(h2, w1, w2, x1, g_post.reshape(1, d), mod3)


def _tile(n, target):
    t = min(n, target)
    while n % t or (t % 8 and t != n):
        t -= 1
    return t


def kernel(x, c, ctx, c_ctx, w_ada, b_ada, g_pre_mix, g_post_mix, g_pre_ffn, g_post_ffn, w_in, b_gates,
           conv_a_w, conv_a_b, ln_a_w, ln_a_b, mlstm_norm_w, conv_c_w, w_out, w_ff1, w_ff2):
    bsz, t_lat, d = x.shape
    t_ctx = ctx.shape[1]
    depth = w_ada.shape[0]
    d_conv = conv_a_w.shape[-1]
    d_short = conv_c_w.shape[-1]
    d_mlstm = mlstm_norm_w.shape[-1]
    heads = b_gates.shape[-1] // 4
    dh = d_mlstm // heads
    n_gates = 4 * heads
    assert dh == LANE and d_conv == d_short and d_conv % LANE == 0
    assert bsz + 1 <= MOD_ROWS and t_lat % GRID_W == 0
    assert w_in.shape[-1] == 2 * d_conv + 4 * d_mlstm + n_gates + 3 * d_short

    tn = d_conv
    chunk = min(256, t_ctx)
    assert d_mlstm % tn == 0 and t_ctx % chunk == 0 and t_lat % chunk == 0
    gate_lo = 2 * d_conv + 4 * d_mlstm
    mt = d_mlstm // tn
    tile = {"a_val": 0, "a_gate": 1, "q": 2, "k": 2 + mt, "v": 2 + 2 * mt, "o": 2 + 3 * mt,
            "s_in": 2 + 4 * mt, "s_b": 3 + 4 * mt, "s_c": 4 + 4 * mt}
    assert tile["a_gate"] == CONV_IN_TILES - 1
    stored = {k_: v_ - CONV_IN_TILES for k_, v_ in tile.items() if v_ >= CONV_IN_TILES}
    lane_blk = {k_: v_ * (tn // dh) for k_, v_ in stored.items()}
    q_scale = float(dh) ** -0.5

    xl = x.reshape(bsz * t_lat, d)
    xc = ctx.reshape(bsz * t_ctx, d)

    cond = jnp.concatenate([c, c_ctx[None, :], jnp.zeros((MOD_ROWS - bsz - 1, d), F32)], axis=0)
    mod3 = _modulation(cond, w_ada, b_ada).reshape(depth * MOD_ROWS, 1, 6 * d)

    tm_in = _tile(t_lat, 512)
    tm_in_ctx = _tile(bsz * t_ctx, 512)
    tm_lat = _tile(t_lat, 512)
    tm_ctx = _tile(bsz * t_ctx, 512)
    tm_out = _tile(t_lat, 512)
    tm_out_ctx = _tile(bsz * t_ctx, 512)

    def gate_cols(g):
        i_f, f_f, i_b, f_b = jnp.split(g, 4, axis=-1)
        pad = jnp.zeros(g.shape[:-1] + (LANE - 2 * heads,), g.dtype)
        return jnp.concatenate([i_f, i_b, pad, f_f, f_b, pad], axis=-1)

    w_head, wg, w_tail = _split_w_in(w_in, gate_lo, heads)
    kv_cols = (tile["o"] - tile["k"]) * tn
    assert (tile["k"] * tn) % kv_cols == 0
    full_parts = [(w_head, gate_lo, 0), (w_tail, w_tail.shape[2], 0)]
    kv_parts = [(w_head, kv_cols, tile["k"] * tn // kv_cols)]

    for layer in range(depth):
        last = layer == depth - 1
        base = layer * MOD_ROWS
        bg = gate_cols(b_gates[layer]).reshape(1, 2 * LANE)
        conv_params = (conv_a_w[layer], conv_a_b[layer], ln_a_w[layer], ln_a_b[layer])

        def lat_row(tm):
            return lambda i: base + (i * tm) // t_lat
        ctx_row = lambda i: base + bsz

        proj = functools.partial(_in_proj, mod3=mod3, g_pre=g_pre_mix[layer], wg=wg, bg=bg, layer=layer,
                                 chunk=chunk, heads=heads, tn=tn, q_scale=q_scale)
        p_lat, xs_lat, rows_lat, u_lat = proj(
            xl, mod_row=lat_row(tm_in), w_parts=full_parts, glu=True, tm=tm_in,
            q_tile_lo=tile["q"], q_tile_hi=tile["k"], name=f"in_proj_lat_{layer}")
        if last:
            p_ctx, xs_ctx, rows_ctx = proj(
                xc, mod_row=ctx_row, w_parts=kv_parts, glu=False, tm=tm_in_ctx,
                q_tile_lo=0, q_tile_hi=0, name=f"in_proj_ctx_{layer}")
            ctx_blk = {"k": 0, "v": lane_blk["v"] - lane_blk["k"]}
        else:
            p_ctx, xs_ctx, rows_ctx, u_ctx = proj(
                xc, mod_row=ctx_row, w_parts=full_parts, glu=True, tm=tm_in_ctx,
                q_tile_lo=tile["q"], q_tile_hi=tile["k"], name=f"in_proj_ctx_{layer}")
            ctx_blk = lane_blk

        *mres, w_out_b, w1_b, w2_b = _mlstm(
                      p_lat, p_ctx, (xs_lat, rows_lat), (xs_ctx, rows_ctx), mlstm_norm_w[layer],
                      (w_out, w_ff1, w_ff2), layer=layer, n_seq=bsz, t_lat=t_lat, t_ctx=t_ctx, heads=heads, dh=dh, chunk=chunk,
                      lat_blk=lane_blk, ctx_blk=ctx_blk, ctx_out=not last, name=f"mlstm_{layer}")

        conv_c = functools.partial(_conv_c, conv_w=conv_c_w[layer], n_seq=bsz, in_blk=stored["s_in"],
                                   b_blk=stored["s_b"], c_blk=stored["s_c"])
        c_lat = conv_c(p_lat, seq_len=t_lat, shift=GRID_W, name=f"conv_c_lat_{layer}")
        x1, h2 = _out_proj(u_lat, mres[0], c_lat, w_out_b, xl, mod3, lat_row(tm_out), g_post_mix[layer],
                           g_pre_ffn[layer], (GRID_W,) + conv_params, tm=tm_out,
                           name=f"out_proj_lat_{layer}")
        xl = _ffn(h2, w1_b, w2_b, x1, mod3, lat_row(tm_lat), g_post_ffn[layer], tm=tm_lat,
                  tf=FFN_TF,
                  name=f"ffn_lat_{layer}")

        if not last:
            c_ctx_mix = conv_c(p_ctx, seq_len=t_ctx, shift=1, name=f"conv_c_ctx_{layer}")
            x1c, h2c = _out_proj(u_ctx, mres[1], c_ctx_mix, w_out_b, xc, mod3, ctx_row, g_post_mix[layer],
                                 g_pre_ffn[layer], (t_ctx,) + conv_params, tm=tm_out_ctx,
                                 name=f"out_proj_ctx_{layer}")
            xc = _ffn(h2c, w1_b, w2_b, x1c, mod3, ctx_row, g_post_ffn[layer], tm=tm_ctx,
                      tf=FFN_TF,
                      name=f"ffn_ctx_{layer}")

    return xl.reshape(bsz, t_lat, d)
```

```python
import functools

import jax
import jax.numpy as jnp
from jax import lax
from jax.experimental import pallas as pl
from jax.experimental.pallas import tpu as pltpu

GRID_W = 64
NORM_EPS = 1e-6
LANE = 128
SUBLANE = 8
VMEM_LIMIT_BYTES = 56 * 1024 * 1024
MOD_ROWS = 16
MASKED_LOG = -1e30
COMBOS = 16
N_SPLIT = 3

F32 = jnp.float32
BF16 = jnp.bfloat16


def _params(n_grid_axes):
    return pltpu.CompilerParams(
        dimension_semantics=("arbitrary",) * n_grid_axes,
        vmem_limit_bytes=VMEM_LIMIT_BYTES)


def _rms(y):
    return y * lax.rsqrt(jnp.mean(y * y, axis=-1, keepdims=True) + NORM_EPS)


def _split3(x):
    hi = x.astype(BF16)
    r1 = x - hi.astype(F32)
    mid = r1.astype(BF16)
    lo = (r1 - mid.astype(F32)).astype(BF16)
    return hi, mid, lo


CAST_BLOCK_BYTES = 4 * 1024 * 1024


def _split_w_in_kernel(wt_ref, place_ref, head_ref, gate_ref, tail_ref, *, gate_lo, n_gates):
    x = wt_ref[0]
    head_ref[0] = x[:gate_lo, :].astype(BF16).T
    gate_ref[0] = lax.dot_general(x[gate_lo:gate_lo + n_gates, :].astype(BF16), place_ref[...],
                                  (((0,), (0,)), ((), ())), preferred_element_type=F32).astype(BF16)
    tail_ref[0] = x[gate_lo + n_gates:, :].astype(BF16).T


def _split_w_in(w_in, gate_lo, heads):
    depth, rows, c = w_in.shape
    n_gates = 4 * heads
    n_tail = c - gate_lo - n_gates
    assert gate_lo % SUBLANE == 0 and n_gates % SUBLANE == 0
    src = jnp.arange(n_gates)
    kind, head = src // heads, src % heads
    dst = jnp.where(kind % 2 == 0, 0, LANE) + (kind // 2) * heads + head
    place = (dst[:, None] == jnp.arange(2 * LANE)[None, :]).astype(BF16)
    tr = _tile(rows, max(LANE, CAST_BLOCK_BYTES // (4 * c) // LANE * LANE))
    blk = lambda cols: pl.BlockSpec((1, tr, cols), lambda l, i: (l, i, 0))
    return pl.pallas_call(
        functools.partial(_split_w_in_kernel, gate_lo=gate_lo, n_gates=n_gates),
        grid=(depth, rows // tr),
        in_specs=[pl.BlockSpec((1, c, tr), lambda l, i: (l, 0, i)),
                  pl.BlockSpec((n_gates, 2 * LANE), lambda l, i: (0, 0))],
        out_specs=[blk(gate_lo), blk(2 * LANE), blk(n_tail)],
        out_shape=[jax.ShapeDtypeStruct((depth, rows, gate_lo), BF16),
                   jax.ShapeDtypeStruct((depth, rows, 2 * LANE), BF16),
                   jax.ShapeDtypeStruct((depth, rows, n_tail), BF16)],
        compiler_params=_params(2),
        name="cast_w_in",
    )(jnp.swapaxes(w_in, 1, 2), place)


def _mod_kernel(s_ref, w_ref, b_ref, o_ref):
    s = s_ref[...]
    a = (s * jax.nn.sigmoid(s)).astype(BF16)
    o_ref[0] = jnp.dot(a, w_ref[0].astype(BF16), preferred_element_type=F32) + b_ref[0]


def _modulation(s, w_ada, b_ada, tn=1024):
    depth, d, n = w_ada.shape
    return pl.pallas_call(
        _mod_kernel,
        grid=(depth, n // tn),
        in_specs=[pl.BlockSpec((MOD_ROWS, d), lambda l, j: (0, 0)),
                  pl.BlockSpec((1, d, tn), lambda l, j: (l, 0, j)),
                  pl.BlockSpec((1, 1, tn), lambda l, j: (l, 0, j))],
        out_specs=pl.BlockSpec((1, MOD_ROWS, tn), lambda l, j: (l, 0, j)),
        out_shape=jax.ShapeDtypeStruct((depth, MOD_ROWS, n), F32),
        compiler_params=_params(2),
        name="adaln_modulation",
    )(s, w_ada, b_ada.reshape(depth, 1, n))


def _in_proj_kernel(*refs, n_parts, chunk, heads, tn, q_tile_lo, q_tile_hi, q_scale, glu):
    x_ref, g_ref, sh_ref, sc_ref, wg_ref, bg_ref = refs[:6]
    w_refs = refs[6:6 + n_parts]
    outs = refs[6 + n_parts:]
    if glu:
        p_ref, xs_ref, rows_ref, u_ref = outs
    else:
        p_ref, xs_ref, rows_ref = outs
    n_glu = CONV_IN_TILES if glu else 0

    gain = g_ref[...] * (1.0 + sc_ref[0])
    n_sub = x_ref.shape[0] // chunk
    tiles = [(w_ref, j) for w_ref in w_refs for j in range(w_ref.shape[1] // tn)]
    last = len(tiles) - 1

    def normed(r):
        return (_rms(x_ref[r * chunk:(r + 1) * chunk, :]) * gain + sh_ref[0]).astype(BF16)

    hb = normed(0)
    for r in range(n_sub):
        rows = slice(r * chunk, (r + 1) * chunk)
        gates = jnp.dot(hb, wg_ref[...], preferred_element_type=F32) + bg_ref[...]
        hb_next = None
        value = None
        for t, (w_ref, j) in enumerate(tiles):
            acc = jnp.dot(hb, w_ref[:, j * tn:(j + 1) * tn], preferred_element_type=F32)
            if t < n_glu:
                if t == 0:
                    value = acc
                else:
                    u_ref[rows, :] = (value * jax.nn.sigmoid(acc)).astype(BF16)
            else:
                if q_tile_lo <= t < q_tile_hi:
                    acc = acc * q_scale
                p_ref[rows, (t - n_glu) * tn:(t - n_glu + 1) * tn] = acc.astype(BF16)
            if t == 0:
                pieces = _forget_pieces(gates)
            if t == min(2, last):
                b = _forget_cumsum(pieces, heads)
            if t == min(4, last):
                xs_ref[rows, :], rows_ref[r] = _gate_stats(gates, b, heads)
            if t == min(6, last) and r + 1 < n_sub:
                hb_next = normed(r + 1)
        hb = hb_next


def _in_proj(x2d, mod3, mod_row, g_pre, w_parts, wg, bg, *, glu, layer, chunk, heads, tm, tn, q_tile_lo,
             q_tile_hi, q_scale, name):
    n, d = x2d.shape
    ng = wg.shape[2]
    nw = sum(width for _, width, _ in w_parts) - (CONV_IN_TILES * tn if glu else 0)
    assert tm % chunk == 0 and all(width % tn == 0 for _, width, _ in w_parts)
    fixed = lambda i: (0, 0)

    def w_spec(width, blk):
        return pl.BlockSpec((None, d, width), lambda i: (layer, 0, blk), pipeline_mode=pl.Buffered(1))

    out_specs = [pl.BlockSpec((tm, nw), lambda i: (i, 0)),
                 pl.BlockSpec((tm, LANE), lambda i: (i, 0)),
                 pl.BlockSpec((tm // chunk, 3 * COMBOS, chunk), lambda i: (i, 0, 0))]
    out_shape = [jax.ShapeDtypeStruct((n, nw), BF16),
                 jax.ShapeDtypeStruct((n, LANE), BF16),
                 jax.ShapeDtypeStruct((n // chunk, 3 * COMBOS, chunk), F32)]
    if glu:
        out_specs.append(pl.BlockSpec((tm, tn), lambda i: (i, 0)))
        out_shape.append(jax.ShapeDtypeStruct((n, tn), BF16))
    kern = functools.partial(_in_proj_kernel, n_parts=len(w_parts), chunk=chunk, heads=heads, tn=tn,
                             q_tile_lo=q_tile_lo, q_tile_hi=q_tile_hi, q_scale=q_scale, glu=glu)
    return pl.pallas_call(
        kern,
        grid=(n // tm,),
        in_specs=[pl.BlockSpec((tm, d), lambda i: (i, 0)),
                  pl.BlockSpec((1, d), fixed),
                  pl.BlockSpec((1, 1, d), lambda i: (mod_row(i), 0, 0)),
                  pl.BlockSpec((1, 1, d), lambda i: (mod_row(i), 0, 1)),
                  pl.BlockSpec((None, d, ng), lambda i: (layer, 0, 0)),
                  pl.BlockSpec((1, ng), fixed)]
                 + [w_spec(width, blk) for _, width, blk in w_parts],
        out_specs=out_specs,
        out_shape=out_shape,
        compiler_params=_params(1),
        name=name,
    )(x2d, g_pre.reshape(1, d), mod3, mod3, wg, bg, *[arr for arr, _, _ in w_parts])


CONV_A_PAD = 16
CONV_SUB = 64
CONV_IN_TILES = 2


def _conv_a_piece(upad_ref, y_ref, w_ref, b_ref, row, sb, c, taps):
    half = taps // 2
    cs = slice(c * LANE, (c + 1) * LANE)
    acc = None
    for res in range(SUBLANE):
        z = None
        for k in range(taps):
            if (k - half) % SUBLANE != res:
                continue
            start = CONV_A_PAD + sb * CONV_SUB + (k - half) - res
            term = w_ref[k:k + 1, cs] * upad_ref[row, start:start + CONV_SUB + SUBLANE, cs]
            z = term if z is None else z + term
        if z is None:
            continue
        z = z[res:res + CONV_SUB, :]
        acc = z if acc is None else acc + z
    y_ref[:, cs] = acc + b_ref[:, cs]


def _conv_a_finish(y_ref, lnw_ref, lnb_ref, o_ref, out_row):
    y = y_ref[...]
    mu = jnp.mean(y, axis=-1, keepdims=True)
    dlt = y - mu
    var = jnp.mean(dlt * dlt, axis=-1, keepdims=True)
    z = dlt * lax.rsqrt(var + NORM_EPS) * lnw_ref[...] + lnb_ref[...]
    o_ref[out_row:out_row + CONV_SUB, :] = (z * jax.nn.sigmoid(z)).astype(BF16)


CONV_C_PAD = 64
CONV_C_BLK = 64


def _conv_c_kernel(sin_ref, sb_ref, sc_ref, w_ref, o_ref, upad_ref, *, seq_len, shift):
    ch = o_ref.shape[-1]
    zeros = jnp.zeros((CONV_C_PAD, ch), F32)
    upad_ref[0:CONV_C_PAD, :] = zeros
    upad_ref[CONV_C_PAD + seq_len:CONV_C_PAD + seq_len + CONV_C_PAD, :] = zeros
    for i in range(seq_len // CONV_C_BLK):
        t = slice(i * CONV_C_BLK, (i + 1) * CONV_C_BLK)
        upad_ref[CONV_C_PAD + i * CONV_C_BLK:CONV_C_PAD + (i + 1) * CONV_C_BLK, :] = (
            sc_ref[t, :].astype(F32) * sin_ref[t, :].astype(F32))
    for i in range(seq_len // CONV_C_BLK):
        t0 = CONV_C_PAD + i * CONV_C_BLK
        y = (w_ref[0:1, :] * upad_ref[t0 - shift:t0 - shift + CONV_C_BLK, :]
             + w_ref[1:2, :] * upad_ref[t0:t0 + CONV_C_BLK, :]
             + w_ref[2:3, :] * upad_ref[t0 + shift:t0 + shift + CONV_C_BLK, :])
        t = slice(i * CONV_C_BLK, (i + 1) * CONV_C_BLK)
        o_ref[t, :] = (sb_ref[t, :].astype(F32) * y).astype(BF16)


def _conv_c(p, conv_w, *, n_seq, seq_len, shift, in_blk, b_blk, c_blk, name):
    taps, ch = conv_w.shape
    assert taps == 3 and shift <= CONV_C_PAD
    n = n_seq * seq_len
    kern = functools.partial(_conv_c_kernel, seq_len=seq_len, shift=shift)
    return pl.pallas_call(
        kern,
        grid=(n_seq,),
        in_specs=[pl.BlockSpec((seq_len, ch), lambda b: (b, in_blk)),
                  pl.BlockSpec((seq_len, ch), lambda b: (b, b_blk)),
                  pl.BlockSpec((seq_len, ch), lambda b: (b, c_blk)),
                  pl.BlockSpec((taps, ch), lambda b: (0, 0))],
        out_specs=pl.BlockSpec((seq_len, ch), lambda b: (b, 0)),
        out_shape=jax.ShapeDtypeStruct((n, ch), BF16),
        scratch_shapes=[pltpu.VMEM((seq_len + 2 * CONV_C_PAD, ch), F32)],
        compiler_params=_params(1),
        name=name,
    )(p, p, p, conv_w)


def _log_sigmoid(z):
    return jnp.minimum(z, 0.0) - jnp.log1p(jnp.exp(-jnp.abs(z)))


def _forget_pieces(gates):
    return _split3(_log_sigmoid(gates[:, LANE:2 * LANE]))


def _forget_cumsum(pieces, heads):
    chunk = pieces[0].shape[0]
    jj = lax.broadcasted_iota(jnp.int32, (chunk, chunk), 0)
    ss = lax.broadcasted_iota(jnp.int32, (chunk, chunk), 1)
    tri_p = jnp.where(ss <= jj, 1.0, 0.0).astype(BF16)
    tri_s = jnp.where(ss >= jj, 1.0, 0.0).astype(BF16)
    b_p = jnp.zeros((chunk, LANE), F32)
    b_s = jnp.zeros((chunk, LANE), F32)
    for piece in pieces:
        b_p = b_p + jnp.dot(tri_p, piece, preferred_element_type=F32)
        b_s = b_s + jnp.dot(tri_s, piece, preferred_element_type=F32)
    lane = lax.broadcasted_iota(jnp.int32, (chunk, LANE), 1)
    return jnp.where(lane < heads, b_p, b_s)


def _gate_stats(gates, b, heads):
    chunk = gates.shape[0]
    assert 2 * heads <= COMBOS and 2 * N_SPLIT * COMBOS <= LANE
    lane = lax.broadcasted_iota(jnp.int32, (chunk, LANE), 1)
    row = lax.broadcasted_iota(jnp.int32, (chunk, LANE), 0)
    fwd = lane < heads
    a = gates[:, 0:LANE] - b

    run_max = a
    sh = 1
    while sh < chunk:
        up = pltpu.roll(run_max, sh, axis=0)
        dn = pltpu.roll(run_max, chunk - sh, axis=0)
        cand = jnp.where(fwd, jnp.where(row >= sh, up, MASKED_LOG),
                         jnp.where(row < chunk - sh, dn, MASKED_LOG))
        run_max = jnp.maximum(run_max, cand)
        sh *= 2

    a_rows = a.T[0:COMBOS, :]
    b_rows = b.T[0:COMBOS, :]
    rows = jnp.concatenate(
        [a_rows,
         jnp.broadcast_to(jnp.max(a_rows, axis=1, keepdims=True), (COMBOS, chunk)),
         jnp.broadcast_to(jnp.min(b_rows, axis=1, keepdims=True), (COMBOS, chunk))], axis=0)

    packed = jnp.zeros((chunk, LANE), F32)
    pieces = _split3(run_max) + _split3(b)
    for i, piece in enumerate(pieces):
        val = jnp.where(lane < COMBOS, piece.astype(F32), 0.0)
        packed = packed + (pltpu.roll(val, COMBOS * i, axis=1) if i else val)
    return packed.astype(BF16), rows


SCAN_GROUP = 8


def _twice(r):
    return jnp.concatenate([r, r], axis=1)


def _chunk_front(q, k, v, xs, a_row, a_max, sel, *, need_h):
    chunk, dh = k.shape
    v_ext = jnp.concatenate([v, jnp.ones_like(v)], axis=1)
    k_t = k.T
    front = {"q": q, "v_ext": v_ext, "a_row": a_row, "a_max": a_max}
    if need_h:
        front["stats"] = jnp.dot(xs, sel, preferred_element_type=F32)
        front["raw"] = jnp.dot(q, k_t, preferred_element_type=F32)
    a_max_row = jnp.concatenate([a_max] * (chunk // dh), axis=1)
    kw_t = (k_t.astype(F32) * jnp.exp(a_row - a_max_row)).astype(BF16)
    front["update"] = jnp.dot(kw_t, v_ext, preferred_element_type=F32)
    return front


def _chunk_back(front, b_last, c_ext, m_st, *, reverse, need_h):
    q, v_ext, a_row, a_max = front["q"], front["v_ext"], front["a_row"], front["a_max"]
    chunk, dh = v_ext.shape[0], v_ext.shape[1] // 2
    h = None
    if need_h:
        a_run, b_col = front["stats"][:, :dh], front["stats"][:, dh:]
        raw = front["raw"]
        g_row = jnp.maximum(a_run, m_st)
        jj = lax.broadcasted_iota(jnp.int32, (dh, dh), 0)
        ss = lax.broadcasted_iota(jnp.int32, (dh, dh), 1)
        diag_seen = (ss >= jj) if reverse else (ss <= jj)
        n_blk = chunk // dh
        score_rows = []
        for rb in range(n_blk):
            r_sl = slice(rb * dh, (rb + 1) * dh)
            blocks = []
            for cb in range(n_blk):
                c_sl = slice(cb * dh, (cb + 1) * dh)
                if (cb < rb) if reverse else (cb > rb):
                    blocks.append(jnp.zeros((dh, dh), BF16))
                    continue
                arg = a_row[:, c_sl] - g_row[r_sl, :]
                if cb == rb:
                    arg = jnp.where(diag_seen, arg, MASKED_LOG)
                blocks.append((raw[r_sl, c_sl] * jnp.exp(arg)).astype(BF16))
            score_rows.append(jnp.concatenate(blocks, axis=1))
        scores = jnp.concatenate(score_rows, axis=0)
        q_inter = (q.astype(F32) * jnp.exp(m_st - g_row)).astype(BF16)
        both = jnp.dot(jnp.concatenate([scores, q_inter], axis=1),
                       jnp.concatenate([v_ext, c_ext.astype(BF16)], axis=0),
                       preferred_element_type=F32)
        floor = jnp.exp(-(b_col + g_row))
        h = both[:, :dh] / jnp.maximum(jnp.abs(both[:, dh:]), floor)

    g_last = jnp.maximum(a_max, m_st)
    c_new = (_twice(jnp.exp(m_st - g_last)) * c_ext
             + _twice(jnp.exp(a_max - g_last)) * front["update"])
    return h, c_new, b_last + g_last


def _mlstm_kernel(*refs, chunk, n_lat, n_ctx, ctx_out, heads, n_cast):
    n_in = 13 if ctx_out else 11
    cast_in, refs = refs[n_in:n_in + n_cast], refs[:n_in] + refs[n_in + n_cast:]
    n_io = n_in + (2 if ctx_out else 1)
    cast_out, refs = refs[n_io:n_io + n_cast], refs[:n_io] + refs[n_io + n_cast:]
    if ctx_out:
        (ql, kl, vl, ol, qc, kc, vc, oc, xsl, rwl, xsc, rwc, nw,
         out_l, out_c, hf_l, hb_l, hf_c, hb_c, c_ref, m_ref) = refs
    else:
        (ql, kl, vl, ol, kc, vc, xsl, rwl, xsc, rwc, nw,
         out_l, hf_l, hb_l, c_ref, m_ref) = refs
        qc = oc = out_c = hf_c = hb_c = None
    for w_ref, wb_ref in zip(cast_in, cast_out):
        wb_ref[...] = w_ref[...].astype(BF16)
    dh = nw.shape[-1]
    head = pl.program_id(1)

    c_ref[...] = jnp.zeros_like(c_ref)
    m_ref[...] = jnp.zeros_like(m_ref)

    kk = lax.broadcasted_iota(jnp.int32, (LANE, 2 * dh), 0)
    nn = lax.broadcasted_iota(jnp.int32, (LANE, 2 * dh), 1)
    piece = jnp.right_shift(kk, COMBOS.bit_length() - 1)
    first_piece = jnp.where(nn < dh, 0, N_SPLIT)
    target = (piece >= first_piece) & (piece < first_piece + N_SPLIT)
    pair_lane = jnp.bitwise_and(kk, COMBOS - 1)
    sels = [jnp.where(target & (pair_lane == direction * heads + head), 1.0, 0.0).astype(BF16)
            for direction in (0, 1)]

    def scan(q_ref, k_ref, v_ref, xs_ref, rw_ref, hf_ref, hb_ref, n_chunks, need_h):
        group = SCAN_GROUP if n_chunks % SCAN_GROUP == 0 else 1

        def front(i, direction):
            rows = pl.ds(pl.multiple_of(i * chunk, chunk), chunk)
            pair = direction * heads + head
            a_row = rw_ref[i, pl.ds(pair, 1), :]
            a_max = rw_ref[i, pl.ds(COMBOS + pair, 1), :][:, 0:dh]
            return _chunk_front(q_ref[rows, :] if need_h else None, k_ref[rows, :], v_ref[rows, :],
                                xs_ref[rows, :], a_row, a_max, sels[direction], need_h=need_h)

        def back(i, direction, fr):
            pair = direction * heads + head
            b_last = rw_ref[i, pl.ds(2 * COMBOS + pair, 1), :][:, 0:dh]
            h, c_new, m_new = _chunk_back(fr, b_last, c_ref[direction], m_ref[direction],
                                          reverse=direction == 1, need_h=need_h)
            c_ref[direction] = c_new
            m_ref[direction] = m_new
            if need_h:
                rows = pl.ds(pl.multiple_of(i * chunk, chunk), chunk)
                (hf_ref, hb_ref)[direction][rows, :] = h

        def body(gi, carry):
            steps = []
            for u in range(group):
                i = gi * group + u
                steps += [(i, 0), (n_chunks - 1 - i, 1)]
            fr = front(*steps[0])
            for s_idx, (i, direction) in enumerate(steps):
                fr_next = front(*steps[s_idx + 1]) if s_idx + 1 < len(steps) else None
                back(i, direction, fr)
                fr = fr_next
            return carry

        lax.fori_loop(0, n_chunks // group, body, 0)

    def finish(hf_ref, hb_ref, o_ref, out_ref, n_chunks):
        mean_w = jnp.full((dh, dh), 1.0 / dh, BF16)

        def body(i, carry):
            rows = pl.ds(pl.multiple_of(i * chunk, chunk), chunk)
            h = hf_ref[rows, :] + hb_ref[rows, :]
            mu = jnp.dot(h.astype(BF16), mean_w, preferred_element_type=F32)
            dlt = h - mu
            var = jnp.dot((dlt * dlt).astype(BF16), mean_w, preferred_element_type=F32)
            y = dlt * lax.rsqrt(var + NORM_EPS) * nw[...]
            out_ref[rows, :] = (y * jax.nn.sigmoid(o_ref[rows, :].astype(F32))).astype(BF16)
            return carry

        lax.fori_loop(0, n_chunks, body, 0, unroll=True)

    scan(qc, kc, vc, xsc, rwc, hf_c, hb_c, n_ctx, ctx_out)
    scan(ql, kl, vl, xsl, rwl, hf_l, hb_l, n_lat, True)
    finish(hf_l, hb_l, ol, out_l, n_lat)
    if ctx_out:
        finish(hf_c, hb_c, oc, out_c, n_ctx)


def _mlstm(p_lat, p_ctx, stats_lat, stats_ctx, norm_w, casts, *, layer, n_seq, t_lat, t_ctx, heads, dh,
           chunk, lat_blk, ctx_blk, ctx_out, name):
    assert chunk % dh == 0
    n_steps = n_seq * heads
    n_lat, n_ctx = t_lat // chunk, t_ctx // chunk

    def tok_spec(t, blk):
        return pl.BlockSpec((t, dh), lambda b, h: (b, blk + h))

    def stat_specs(t, n_chunks):
        return [pl.BlockSpec((t, LANE), lambda b, h: (b, 0)),
                pl.BlockSpec((n_chunks, 3 * COMBOS, chunk), lambda b, h: (b, 0, 0))]

    lat_in = [tok_spec(t_lat, lat_blk[name_]) for name_ in "qkvo"]
    ctx_names = "qkvo" if ctx_out else "kv"
    ctx_in = [tok_spec(t_ctx, ctx_blk[name_]) for name_ in ctx_names]
    in_specs = (lat_in + ctx_in + stat_specs(t_lat, n_lat) + stat_specs(t_ctx, n_ctx)
                + [pl.BlockSpec((1, dh), lambda b, h: (0, h))])
    args = ([p_lat] * 4 + [p_ctx] * len(ctx_names) + list(stats_lat) + list(stats_ctx)
            + [norm_w.reshape(1, heads * dh)] + list(casts))
    for w in casts:
        slab = w.shape[1] // n_steps
        assert w.shape[1] % n_steps == 0 and slab % (2 * SUBLANE) == 0
        in_specs.append(pl.BlockSpec((None, slab, w.shape[2]), lambda b, h: (layer, b * heads + h, 0)))

    out_specs = [pl.BlockSpec((t_lat, dh), lambda b, h: (b, h))]
    out_shape = [jax.ShapeDtypeStruct((n_seq * t_lat, heads * dh), BF16)]
    scratch = [pltpu.VMEM((t_lat, dh), F32), pltpu.VMEM((t_lat, dh), F32)]
    if ctx_out:
        out_specs.append(pl.BlockSpec((t_ctx, dh), lambda b, h: (b, h)))
        out_shape.append(jax.ShapeDtypeStruct((n_seq * t_ctx, heads * dh), BF16))
        scratch += [pltpu.VMEM((t_ctx, dh), F32), pltpu.VMEM((t_ctx, dh), F32)]
    scratch += [pltpu.VMEM((2, dh, 2 * dh), F32), pltpu.VMEM((2, 1, dh), F32)]
    for w in casts:
        out_specs.append(pl.BlockSpec((w.shape[1] // n_steps, w.shape[2]), lambda b, h: (b * heads + h, 0)))
        out_shape.append(jax.ShapeDtypeStruct(w.shape[1:], BF16))

    kern = functools.partial(_mlstm_kernel, chunk=chunk, n_lat=n_lat, n_ctx=n_ctx, ctx_out=ctx_out,
                             heads=heads, n_cast=len(casts))
    return pl.pallas_call(
        kern,
        grid=(n_seq, heads),
        in_specs=in_specs,
        out_specs=out_specs,
        out_shape=out_shape,
        scratch_shapes=scratch,
        compiler_params=_params(2),
        name=name,
    )(*args)


OUT_PROJ_SUB_ROWS = 256


def _out_proj_kernel(u_ref, m_ref, c_ref, w_ref, x_ref, gpost_ref, g1_ref, gpre_ref, sh2_ref, sc2_ref,
                     cw_ref, cb_ref, lnw_ref, lnb_ref, x1_ref, h2_ref, upad_ref, y_ref, a_ref,
                     *, sub_rows, conv_row_len, conv_taps):
    da, dm = u_ref.shape[-1], m_ref.shape[-1]
    tm = x_ref.shape[0]
    n_sub = tm // sub_rows
    gain1 = g1_ref[0] * gpost_ref[...]
    gain2 = gpre_ref[...] * (1.0 + sc2_ref[0])

    zeros = jnp.zeros((CONV_A_PAD, da), F32)
    for row in range(tm // conv_row_len):
        upad_ref[row, 0:CONV_A_PAD, :] = zeros
        upad_ref[row, CONV_A_PAD:CONV_A_PAD + conv_row_len, :] = (
            u_ref[row * conv_row_len:(row + 1) * conv_row_len, :].astype(F32))
        upad_ref[row, CONV_A_PAD + conv_row_len:CONV_A_PAD + conv_row_len + CONV_A_PAD, :] = zeros

    items = []
    for row in range(tm // conv_row_len):
        for sb in range(conv_row_len // CONV_SUB):
            items += [(row, sb, c) for c in range(da // LANE)] + [(row, sb, None)]
    d_out = x_ref.shape[1]
    col_tiles = [slice(c0, c0 + da) for c0 in range(0, d_out, da)]
    slots = [(r, cols) for r in range(n_sub) for cols in col_tiles]
    per_slot = -(-len(items) // len(slots))
    for s_idx, (r, cols) in enumerate(slots):
        rows = slice(r * sub_rows, (r + 1) * sub_rows)
        x1_ref[rows, cols] = (jnp.dot(m_ref[rows, :], w_ref[da:da + dm, cols], preferred_element_type=F32)
                              + jnp.dot(c_ref[rows, :], w_ref[da + dm:, cols],
                                        preferred_element_type=F32))
        for row, sb, c in items[s_idx * per_slot:(s_idx + 1) * per_slot]:
            y_blk = y_ref.at[row % y_ref.shape[0]]
            if c is None:
                _conv_a_finish(y_blk, lnw_ref, lnb_ref, a_ref, row * conv_row_len + sb * CONV_SUB)
            else:
                _conv_a_piece(upad_ref, y_blk, cw_ref, cb_ref, row, sb, c, conv_taps)
    for r in range(n_sub):
        rows = slice(r * sub_rows, (r + 1) * sub_rows)
        y = x1_ref[rows, :] + jnp.dot(a_ref[rows, :], w_ref[0:da, :], preferred_element_type=F32)
        x1 = x_ref[rows, :] + _rms(y) * gain1
        x1_ref[rows, :] = x1
        h2_ref[rows, :] = (_rms(x1) * gain2 + sh2_ref[0]).astype(BF16)


def _out_proj(u, m, cc, w_out, x2d, mod3, mod_row, g_post, g_pre_ffn, conv, *, tm, name):
    n, d = x2d.shape
    da, dm, dc = u.shape[1], m.shape[1], cc.shape[1]
    row_len, conv_w, conv_b, ln_w, ln_b = conv
    taps = conv_w.shape[0]
    assert tm % row_len == 0 and row_len % CONV_SUB == 0 and taps // 2 < CONV_A_PAD
    row = lambda i: (i, 0)
    fixed = lambda i: (0, 0)
    mod_spec = lambda k: pl.BlockSpec((1, 1, d), lambda i: (mod_row(i), 0, k))
    conv_rows = tm // row_len
    return pl.pallas_call(
        functools.partial(_out_proj_kernel, sub_rows=min(tm, OUT_PROJ_SUB_ROWS), conv_row_len=row_len,
                          conv_taps=taps),
        grid=(n // tm,),
        in_specs=[pl.BlockSpec((tm, da), row), pl.BlockSpec((tm, dm), row), pl.BlockSpec((tm, dc), row),
                  pl.BlockSpec((da + dm + dc, d), fixed, pipeline_mode=pl.Buffered(1)),
                  pl.BlockSpec((tm, d), row),
                  pl.BlockSpec((1, d), fixed),
                  mod_spec(2),
                  pl.BlockSpec((1, d), fixed),
                  mod_spec(3), mod_spec(4),
                  pl.BlockSpec((taps, da), fixed)] + [pl.BlockSpec((1, da), fixed)] * 3,
        out_specs=[pl.BlockSpec((tm, d), row), pl.BlockSpec((tm, d), row)],
        out_shape=[jax.ShapeDtypeStruct((n, d), F32), jax.ShapeDtypeStruct((n, d), BF16)],
        scratch_shapes=[pltpu.VMEM((conv_rows, row_len + 2 * CONV_A_PAD, da), F32),
                        pltpu.VMEM((min(conv_rows, 4), CONV_SUB, da), F32),
                        pltpu.VMEM((tm, da), BF16)],
        compiler_params=_params(1),
        name=name,
    )(u, m, cc, w_out, x2d, g_post.reshape(1, d), mod3, g_pre_ffn.reshape(1, d), mod3, mod3,
      conv_w, conv_b.reshape(1, da), ln_w.reshape(1, da), ln_b.reshape(1, da))


FFN_TF = 1024


FFN_SUB_ROWS = 256


def _ffn_kernel(h_ref, w1_ref, w2_ref, x_ref, gpost_ref, g2_ref, o_ref, acc_ref, *, sub_rows):
    j = pl.program_id(1)
    last = pl.num_programs(1) - 1

    @pl.when(j == 0)
    def _():
        acc_ref[...] = jnp.zeros_like(acc_ref)

    def hidden_step(rows):
        u = jnp.maximum(jnp.dot(h_ref[rows, :], w1_ref[...], preferred_element_type=F32), 0.0)
        return jnp.dot((u * u).astype(BF16), w2_ref[...], preferred_element_type=F32)

    @pl.when(j < last)
    def _():
        acc_ref[...] += hidden_step(slice(None))

    @pl.when(j == last)
    def _():
        gain = g2_ref[0] * gpost_ref[...]
        for r in range(h_ref.shape[0] // sub_rows):
            rows = slice(r * sub_rows, (r + 1) * sub_rows)
            y = acc_ref[rows, :] + hidden_step(rows)
            o_ref[rows, :] = x_ref[rows, :] + _rms(y) * gain


def _ffn(h2, w1, w2, x1, mod3, mod_row, g_post, *, tm, tf, name):
    n, d = x1.shape
    dff = w1.shape[1]
    n_hidden = dff // tf

    def hidden(i, j):
        return jnp.where(i % 2 == 0, j, n_hidden - 1 - j)

    return pl.pallas_call(
        functools.partial(_ffn_kernel, sub_rows=min(tm, FFN_SUB_ROWS)),
        grid=(n // tm, n_hidden),
        in_specs=[pl.BlockSpec((tm, d), lambda i, j: (i, 0)),
                  pl.BlockSpec((d, tf), lambda i, j: (0, hidden(i, j))),
                  pl.BlockSpec((tf, d), lambda i, j: (hidden(i, j), 0)),
                  pl.BlockSpec((tm, d), lambda i, j: (i, 0)),
                  pl.BlockSpec((1, d), lambda i, j: (0, 0)),
                  pl.BlockSpec((1, 1, d), lambda i, j: (mod_row(i), 0, 5))],
        out_specs=pl.BlockSpec((tm, d), lambda i, j: (i, 0)),
        out_shape=jax.ShapeDtypeStruct((n, d), F32),
        scratch_shapes=[pltpu.VMEM((tm, d), F32)],
        compiler_params=_params(2),
        name=name,
    )(h2, w1, w2, x1, g_post.reshape(1, d), mod3)


def _tile(n, target):
    t = min(n, target)
    while n % t or (t % 8 and t != n):
        t -= 1
    return t


def kernel(x, c, ctx, c_ctx, w_ada, b_ada, g_pre_mix, g_post_mix, g_pre_ffn, g_post_ffn, w_in, b_gates,
           conv_a_w, conv_a_b, ln_a_w, ln_a_b, mlstm_norm_w, conv_c_w, w_out, w_ff1, w_ff2):
    bsz, t_lat, d = x.shape
    t_ctx = ctx.shape[1]
    depth = w_ada.shape[0]
    d_conv = conv_a_w.shape[-1]
    d_short = conv_c_w.shape[-1]
    d_mlstm = mlstm_norm_w.shape[-1]
    heads = b_gates.shape[-1] // 4
    dh = d_mlstm // heads
    n_gates = 4 * heads
    assert dh == LANE and d_conv == d_short and d_conv % LANE == 0
    assert bsz + 1 <= MOD_ROWS and t_lat % GRID_W == 0
    assert w_in.shape[-1] == 2 * d_conv + 4 * d_mlstm + n_gates + 3 * d_short

    tn = d_conv
    chunk = min(256, t_ctx)
    assert d_mlstm % tn == 0 and t_ctx % chunk == 0 and t_lat % chunk == 0
    gate_lo = 2 * d_conv + 4 * d_mlstm
    mt = d_mlstm // tn
    tile = {"a_val": 0, "a_gate": 1, "q": 2, "k": 2 + mt, "v": 2 + 2 * mt, "o": 2 + 3 * mt,
            "s_in": 2 + 4 * mt, "s_b": 3 + 4 * mt, "s_c": 4 + 4 * mt}
    assert tile["a_gate"] == CONV_IN_TILES - 1
    stored = {k_: v_ - CONV_IN_TILES for k_, v_ in tile.items() if v_ >= CONV_IN_TILES}
    lane_blk = {k_: v_ * (tn // dh) for k_, v_ in stored.items()}
    q_scale = float(dh) ** -0.5

    xl = x.reshape(bsz * t_lat, d)
    xc = ctx.reshape(bsz * t_ctx, d)

    cond = jnp.concatenate([c, c_ctx[None, :], jnp.zeros((MOD_ROWS - bsz - 1, d), F32)], axis=0)
    mod3 = _modulation(cond, w_ada, b_ada).reshape(depth * MOD_ROWS, 1, 6 * d)

    tm_in = _tile(t_lat, 512)
    tm_in_ctx = _tile(bsz * t_ctx, 512)
    tm_lat = _tile(t_lat, 512)
    tm_ctx = _tile(bsz * t_ctx, 512)
    tm_out = _tile(t_lat, 512)
    tm_out_ctx = _tile(bsz * t_ctx, 512)

    def gate_cols(g):
        i_f, f_f, i_b, f_b = jnp.split(g, 4, axis=-1)
        pad = jnp.zeros(g.shape[:-1] + (LANE - 2 * heads,), g.dtype)
        return jnp.concatenate([i_f, i_b, pad, f_f, f_b, pad], axis=-1)

    w_head, wg, w_tail = _split_w_in(w_in, gate_lo, heads)
    kv_cols = (tile["o"] - tile["k"]) * tn
    assert (tile["k"] * tn) % kv_cols == 0
    full_parts = [(w_head, gate_lo, 0), (w_tail, w_tail.shape[2], 0)]
    kv_parts = [(w_head, kv_cols, tile["k"] * tn // kv_cols)]

    for layer in range(depth):
        last = layer == depth - 1
        base = layer * MOD_ROWS
        bg = gate_cols(b_gates[layer]).reshape(1, 2 * LANE)
        conv_params = (conv_a_w[layer], conv_a_b[layer], ln_a_w[layer], ln_a_b[layer])

        def lat_row(tm):
            return lambda i: base + (i * tm) // t_lat
        ctx_row = lambda i: base + bsz

        proj = functools.partial(_in_proj, mod3=mod3, g_pre=g_pre_mix[layer], wg=wg, bg=bg, layer=layer,
                                 chunk=chunk, heads=heads, tn=tn, q_scale=q_scale)
        p_lat, xs_lat, rows_lat, u_lat = proj(
            xl, mod_row=lat_row(tm_in), w_parts=full_parts, glu=True, tm=tm_in,
            q_tile_lo=tile["q"], q_tile_hi=tile["k"], name=f"in_proj_lat_{layer}")
        if last:
            p_ctx, xs_ctx, rows_ctx = proj(
                xc, mod_row=ctx_row, w_parts=kv_parts, glu=False, tm=tm_in_ctx,
                q_tile_lo=0, q_tile_hi=0, name=f"in_proj_ctx_{layer}")
            ctx_blk = {"k": 0, "v": lane_blk["v"] - lane_blk["k"]}
        else:
            p_ctx, xs_ctx, rows_ctx, u_ctx = proj(
                xc, mod_row=ctx_row, w_parts=full_parts, glu=True, tm=tm_in_ctx,
                q_tile_lo=tile["q"], q_tile_hi=tile["k"], name=f"in_proj_ctx_{layer}")
            ctx_blk = lane_blk

        *mres, w_out_b, w1_b, w2_b = _mlstm(
                      p_lat, p_ctx, (xs_lat, rows_lat), (xs_ctx, rows_ctx), mlstm_norm_w[layer],
                      (w_out, w_ff1, w_ff2), layer=layer, n_seq=bsz, t_lat=t_lat, t_ctx=t_ctx, heads=heads, dh=dh, chunk=chunk,
                      lat_blk=lane_blk, ctx_blk=ctx_blk, ctx_out=not last, name=f"mlstm_{layer}")

        conv_c = functools.partial(_conv_c, conv_w=conv_c_w[layer], n_seq=bsz, in_blk=stored["s_in"],
                                   b_blk=stored["s_b"], c_blk=stored["s_c"])
        c_lat = conv_c(p_lat, seq_len=t_lat, shift=GRID_W, name=f"conv_c_lat_{layer}")
        x1, h2 = _out_proj(u_lat, mres[0], c_lat, w_out_b, xl, mod3, lat_row(tm_out), g_post_mix[layer],
                           g_pre_ffn[layer], (GRID_W,) + conv_params, tm=tm_out,
                           name=f"out_proj_lat_{layer}")
        xl = _ffn(h2, w1_b, w2_b, x1, mod3, lat_row(tm_lat), g_post_ffn[layer], tm=tm_lat,
                  tf=FFN_TF,
                  name=f"ffn_lat_{layer}")

        if not last:
            c_ctx_mix = conv_c(p_ctx, seq_len=t_ctx, shift=1, name=f"conv_c_ctx_{layer}")
            x1c, h2c = _out_proj(u_ctx, mres[1], c_ctx_mix, w_out_b, xc, mod3, ctx_row, g_post_mix[layer],
                                 g_pre_ffn[layer], (t_ctx,) + conv_params, tm=tm_out_ctx,
                                 name=f"out_proj_ctx_{layer}")
            xc = _ffn(h2c, w1_b, w2_b, x1c, mod3, ctx_row, g_post_ffn[layer], tm=tm_ctx,
                      tf=FFN_TF,
                      name=f"ffn_ctx_{layer}")

    return xl.reshape(bsz, t_lat, d)
```
